```python
import jax, jax.numpy as jnp
from jax import lax
import numpy as np

D_MODEL = 1024
BATCH = 4
SEQ = 8192
DEPTH = 2

GRID_W = 64
CTX_LEN = 256
D_MIX = D_MODEL
SSD_WIDTH = D_MIX // 2
SSD_HEADDIM = 64
SSD_HEADS = SSD_WIDTH // SSD_HEADDIM
SSD_GROUPS = 2
HEADS_PER_GROUP = SSD_HEADS // SSD_GROUPS
SSD_STATE = 128
SSD_CONV = 3
SSD_CHUNK = 128
XBC_WIDTH = SSD_WIDTH + 2 * SSD_GROUPS * SSD_STATE
FNET_WIDTH = D_MIX // 4
FNET_GROUPS = 4
FNET_GDIM = FNET_WIDTH // FNET_GROUPS
POOL_WINDOWS = (2, 4, 8, 16)
POOL_WIDTH = D_MIX // 4
POOL_GDIM = POOL_WIDTH // len(POOL_WINDOWS)
D_FF = 256 * ((8 * D_MODEL // 3 + 255) // 256)
FFN_CONV = 3
OFF_Z = 0
OFF_XBC = OFF_Z + SSD_WIDTH
OFF_DT = OFF_XBC + XBC_WIDTH
OFF_FNET = OFF_DT + 2 * SSD_HEADS
OFF_POOL = OFF_FNET + FNET_WIDTH
N_IN = OFF_POOL + POOL_WIDTH
ALPHA = (2.0 * DEPTH) ** 0.25
BETA = (8.0 * DEPTH) ** -0.25
EPS = 1e-6

kernel_name = "hybrid_ssd_fourier_pool_convffn_prefix"


def layer_norm(x):
    xf = x.astype(jnp.float32)
    mu = jnp.mean(xf, -1, keepdims=True)
    var = jnp.mean(jnp.square(xf - mu), -1, keepdims=True)
    return ((xf - mu) * lax.rsqrt(var + EPS)).astype(x.dtype)


def ada(cvec, w, b):
    m = jax.nn.silu(cvec) @ w + b
    return [mi[:, None, :] for mi in jnp.split(m, 6, axis=-1)]


def modulate(x, shift, scale):
    return layer_norm(x) * (1 + scale) + shift


def post_norm(x, y, gate, g, b):
    return layer_norm(ALPHA * x + gate * y) * g + b


def dwconv1d(x, w, b):
    k = w.shape[0]
    left = (k - 1) // 2
    y = lax.conv_general_dilated(x, w[:, None, :], window_strides=(1,), padding=[(left, k - 1 - left)],
                                 dimension_numbers=("NWC", "WIO", "NWC"), feature_group_count=x.shape[-1])
    return y + b


def dwconv2d_grid(x, w, b):
    bsz, L, C = x.shape
    rows = L // GRID_W
    xg = x.reshape(bsz, rows, GRID_W, C)
    y = lax.conv_general_dilated(xg, w[:, :, None, :], window_strides=(1, 1), padding=[(1, 1), (1, 1)],
                                 dimension_numbers=("NHWC", "HWIO", "NHWC"), feature_group_count=C)
    return y.reshape(bsz, L, C) + b


def segsum(a):
    T = a.shape[-1]
    cs = jnp.cumsum(a, axis=-1)
    seg = cs[..., :, None] - cs[..., None, :]
    mask = jnp.tril(jnp.ones((T, T), dtype=bool))
    return jnp.where(mask, seg, -jnp.inf)


def ssd_chunked(X, dA, B, C, init_state):
    b, L, h, p = X.shape
    n = B.shape[-1]
    nc = L // SSD_CHUNK
    X = X.astype(jnp.float32).reshape(b, nc, SSD_CHUNK, h, p)
    B = B.astype(jnp.float32).reshape(b, nc, SSD_CHUNK, h, n)
    C = C.astype(jnp.float32).reshape(b, nc, SSD_CHUNK, h, n)
    A = dA.astype(jnp.float32).reshape(b, nc, SSD_CHUNK, h).transpose(0, 3, 1, 2)
    A_cs = jnp.cumsum(A, axis=-1)
    Lmat = jnp.exp(segsum(A))
    y_diag = jnp.einsum("bclhn,bcshn,bhcls,bcshp->bclhp", C, B, Lmat, X)
    decay_states = jnp.exp(A_cs[..., -1:] - A_cs)
    states = jnp.einsum("bclhn,bhcl,bclhp->bchpn", B, decay_states, X)
    states = jnp.concatenate([init_state[:, None].astype(jnp.float32), states], axis=1)
    chunk_decay = jnp.exp(segsum(jnp.pad(A_cs[..., -1], ((0, 0), (0, 0), (1, 0)))))
    new_states = jnp.einsum("bhzc,bchpn->bzhpn", chunk_decay, states)
    states, final_state = new_states[:, :-1], new_states[:, -1]
    y_off = jnp.einsum("bclhn,bchpn,bhcl->bclhp", C, states, jnp.exp(A_cs))
    return (y_diag + y_off).reshape(b, L, h, p), final_state


def ssd_final_state(X, dA, B):
    cs = jnp.cumsum(dA, axis=1)
    return jnp.einsum("blh,blhn,blhp->bhpn", jnp.exp(cs[:, -1:] - cs), B.astype(jnp.float32), X)


def rev(a):
    return jnp.flip(a, axis=1)


def ssd_prep(u, conv_w, conv_b, dt_bias, a_log):
    bsz, L, _ = u.shape
    xbc = jax.nn.silu(dwconv1d(u[..., OFF_XBC:OFF_DT], conv_w, conv_b))
    xs = xbc[..., :SSD_WIDTH].reshape(bsz, L, SSD_HEADS, SSD_HEADDIM)
    gn = SSD_GROUPS * SSD_STATE
    bm = xbc[..., SSD_WIDTH:SSD_WIDTH + gn].reshape(bsz, L, SSD_GROUPS, SSD_STATE)
    cm = xbc[..., SSD_WIDTH + gn:].reshape(bsz, L, SSD_GROUPS, SSD_STATE)
    bm = jnp.repeat(bm, HEADS_PER_GROUP, axis=2)
    cm = jnp.repeat(cm, HEADS_PER_GROUP, axis=2)
    dt_raw = u[..., OFF_DT:OFF_FNET].reshape(bsz, L, 2, SSD_HEADS)
    dt = jax.nn.softplus((dt_raw + dt_bias).astype(jnp.float32))
    dA = dt * (-jnp.exp(a_log.astype(jnp.float32)))
    return xs, bm, cm, dt, dA


def ssd_bidir(xs, bm, cm, dt, dA, init_f, init_b):
    x32 = xs.astype(jnp.float32)
    y_f, s_f = ssd_chunked(x32 * dt[:, :, 0, :, None], dA[:, :, 0], bm, cm, init_f)
    y_b, s_b = ssd_chunked(rev(x32 * dt[:, :, 1, :, None]), rev(dA[:, :, 1]), rev(bm), rev(cm), init_b)
    return y_f + rev(y_b), s_f, s_b


def ssd_gate_norm(y, xs, z, d_skip, norm_w):
    bsz, L = z.shape[:2]
    y = (y + xs.astype(jnp.float32) * d_skip.astype(jnp.float32)[:, None]).reshape(bsz, L, SSD_WIDTH)
    y = (y * jax.nn.silu(z.astype(jnp.float32))).reshape(bsz, L, SSD_GROUPS, SSD_WIDTH // SSD_GROUPS)
    y = (y * lax.rsqrt(jnp.mean(y * y, -1, keepdims=True) + EPS)).reshape(bsz, L, SSD_WIDTH)
    return (y * norm_w).astype(z.dtype)


def fourier_mix(f, w):
    bsz, L, _ = f.shape
    fg = f.astype(jnp.float32).reshape(bsz, L, FNET_GROUPS, FNET_GDIM)
    spec = jnp.fft.fft2(fg, axes=(1, 3), norm="ortho").real
    y = jnp.einsum("blgc,gcd->blgd", spec.astype(f.dtype), w)
    return y.reshape(bsz, L, FNET_WIDTH)


def pool_mix(p, w, scale):
    bsz, L, _ = p.shape
    pf = p.astype(jnp.float32)
    cs = jnp.pad(jnp.cumsum(pf, axis=1), ((0, 0), (1, 0), (0, 0)))
    pos = jnp.arange(L)
    outs = []
    for gi, win in enumerate(POOL_WINDOWS):
        left = win // 2
        right = win - 1 - left
        hi = jnp.minimum(pos + right + 1, L)
        lo = jnp.maximum(pos - left, 0)
        sl = slice(gi * POOL_GDIM, (gi + 1) * POOL_GDIM)
        csg = cs[..., sl]
        mean = (csg[:, hi] - csg[:, lo]) / (hi - lo).astype(jnp.float32)[None, :, None]
        outs.append(mean - pf[..., sl])
    pooled = jnp.stack(outs, axis=2).astype(p.dtype)
    y = jnp.einsum("blgc,gcd->blgd", pooled, w).reshape(bsz, L, POOL_WIDTH)
    return y * scale


def token_mixer(u, init_f, init_b, conv_w, conv_b, dt_bias, a_log, d_skip, norm_w, fnet_w, pool_w, pool_scale, w_out):
    xs, bm, cm, dt, dA = ssd_prep(u, conv_w, conv_b, dt_bias, a_log)
    y, s_f, s_b = ssd_bidir(xs, bm, cm, dt, dA, init_f, init_b)
    y_ssd = ssd_gate_norm(y, xs, u[..., OFF_Z:OFF_XBC], d_skip, norm_w)
    y_fnet = fourier_mix(u[..., OFF_FNET:OFF_POOL], fnet_w)
    y_pool = pool_mix(u[..., OFF_POOL:N_IN], pool_w, pool_scale)
    return jnp.concatenate([y_ssd, y_fnet, y_pool], axis=-1) @ w_out, s_f, s_b


def ctx_scan_states(u, conv_w, conv_b, dt_bias, a_log):
    xs, bm, _, dt, dA = ssd_prep(u, conv_w, conv_b, dt_bias, a_log)
    x32 = xs.astype(jnp.float32)
    s_f = ssd_final_state(x32 * dt[:, :, 0, :, None], dA[:, :, 0], bm)
    s_b = ssd_final_state(rev(x32 * dt[:, :, 1, :, None]), rev(dA[:, :, 1]), rev(bm))
    return s_f, s_b


def conv_ffn(h, w_up, conv_w, conv_b, w_down, grid):
    a = h @ w_up
    a = dwconv2d_grid(a, conv_w, conv_b) if grid else dwconv1d(a, conv_w[1], conv_b)
    val, gate = jnp.split(a, 2, axis=-1)
    return (jax.nn.gelu(gate) * val) @ w_down


def setup_inputs(seed: int = 0) -> dict:
    key = jax.random.key(seed)
    ks = jax.random.split(key, 25)
    f32 = jnp.float32

    def nrm(k, shape, s):
        return s * jax.random.normal(k, shape, f32)

    dt0 = jnp.exp(jax.random.uniform(ks[9], (DEPTH, 2, SSD_HEADS), f32, np.log(1e-3), np.log(1e-1)))
    return {
        "x": nrm(ks[0], (BATCH, SEQ, D_MODEL), 1.0),
        "c": nrm(ks[1], (BATCH, D_MODEL), 1.0),
        "ctx": nrm(ks[2], (BATCH, CTX_LEN, D_MODEL), 1.0),
        "c_ctx": nrm(ks[3], (D_MODEL,), 1.0),
        "w_ada": nrm(ks[4], (DEPTH, D_MODEL, 6 * D_MODEL), 0.5 * D_MODEL ** -0.5),
        "b_ada": nrm(ks[5], (DEPTH, 6 * D_MODEL), 0.02),
        "w_in": nrm(ks[6], (DEPTH, D_MODEL, N_IN), D_MODEL ** -0.5),
        "ssd_conv_w": nrm(ks[7], (DEPTH, SSD_CONV, XBC_WIDTH), SSD_CONV ** -0.5),
        "ssd_conv_b": nrm(ks[8], (DEPTH, XBC_WIDTH), 0.02),
        "ssd_dt_bias": dt0 + jnp.log(-jnp.expm1(-dt0)),
        "ssd_a_log": jnp.log(jax.random.uniform(ks[10], (DEPTH, 2, SSD_HEADS), f32, 1.0, 16.0)),
        "ssd_d": 1.0 + nrm(ks[11], (DEPTH, SSD_HEADS), 0.1),
        "ssd_norm_w": 1.0 + nrm(ks[12], (DEPTH, SSD_WIDTH), 0.1),
        "fnet_w": nrm(ks[13], (DEPTH, FNET_GROUPS, FNET_GDIM, FNET_GDIM), FNET_GDIM ** -0.5),
        "pool_w": nrm(ks[14], (DEPTH, len(POOL_WINDOWS), POOL_GDIM, POOL_GDIM), POOL_GDIM ** -0.5),
        "pool_scale": 1.0 + nrm(ks[15], (DEPTH, POOL_WIDTH), 0.1),
        "w_out": nrm(ks[16], (DEPTH, D_MIX, D_MODEL), BETA * D_MIX ** -0.5),
        "ln1_g": 1.0 + nrm(ks[17], (DEPTH, D_MODEL), 0.1),
        "ln1_b": nrm(ks[18], (DEPTH, D_MODEL), 0.02),
        "ffn_w_up": nrm(ks[19], (DEPTH, D_MODEL, 2 * D_FF), D_MODEL ** -0.5),
        "ffn_conv_w": nrm(ks[20], (DEPTH, FFN_CONV, FFN_CONV, 2 * D_FF), 1.0 / FFN_CONV),
        "ffn_conv_b": nrm(ks[21], (DEPTH, 2 * D_FF), 0.02),
        "ffn_w_down": nrm(ks[22], (DEPTH, D_FF, D_MODEL), BETA * D_FF ** -0.5),
        "ln2_g": 1.0 + nrm(ks[23], (DEPTH, D_MODEL), 0.1),
        "ln2_b": nrm(ks[24], (DEPTH, D_MODEL), 0.02),
    }


def reference(x, c, ctx, c_ctx, w_ada, b_ada, w_in, ssd_conv_w, ssd_conv_b, ssd_dt_bias, ssd_a_log, ssd_d,
              ssd_norm_w, fnet_w, pool_w, pool_scale, w_out, ln1_g, ln1_b, ffn_w_up, ffn_conv_w, ffn_conv_b,
              ffn_w_down, ln2_g, ln2_b):
    bsz = x.shape[0]
    zero_state = jnp.zeros((bsz, SSD_HEADS, SSD_HEADDIM, SSD_STATE), jnp.float32)
    for l in range(DEPTH):
        last = l == DEPTH - 1
        sh1, sc1, g1, sh2, sc2, g2 = ada(c, w_ada[l], b_ada[l])
        csh1, csc1, cg1, csh2, csc2, cg2 = ada(c_ctx[None, :], w_ada[l], b_ada[l])
        mixer_p = (ssd_conv_w[l], ssd_conv_b[l], ssd_dt_bias[l], ssd_a_log[l], ssd_d[l], ssd_norm_w[l],
                   fnet_w[l], pool_w[l], pool_scale[l], w_out[l])
        u_ctx = modulate(ctx, csh1, csc1) @ w_in[l]
        u_lat = modulate(x, sh1, sc1) @ w_in[l]
        if last:
            s_f, s_b = ctx_scan_states(u_ctx, ssd_conv_w[l], ssd_conv_b[l], ssd_dt_bias[l], ssd_a_log[l])
        else:
            mix_ctx, s_f, s_b = token_mixer(u_ctx, zero_state, zero_state, *mixer_p)
        mix_lat, _, _ = token_mixer(u_lat, s_f, s_b, *mixer_p)
        x = post_norm(x, mix_lat, g1, ln1_g[l], ln1_b[l])
        ffn_lat = conv_ffn(modulate(x, sh2, sc2), ffn_w_up[l], ffn_conv_w[l], ffn_conv_b[l], ffn_w_down[l], True)
        x = post_norm(x, ffn_lat, g2, ln2_g[l], ln2_b[l])
        if not last:
            ctx = post_norm(ctx, mix_ctx, cg1, ln1_g[l], ln1_b[l])
            ffn_ctx = conv_ffn(modulate(ctx, csh2, csc2), ffn_w_up[l], ffn_conv_w[l], ffn_conv_b[l], ffn_w_down[l], False)
            ctx = post_norm(ctx, ffn_ctx, cg2, ln2_g[l], ln2_b[l])
    return x
```

```python
import functools
import math

import numpy as np
import jax
import jax.numpy as jnp
from jax import lax
from jax.experimental import pallas as pl
from jax.experimental.pallas import tpu as pltpu

F32 = jnp.float32
BF16 = jnp.bfloat16
HIGHEST = lax.Precision.HIGHEST

D_MODEL = 1024
DEPTH = 2
GRID_W = 64
SSD_WIDTH = 512
SSD_HEADDIM = 64
SSD_HEADS = 8
SSD_GROUPS = 2
HEADS_PER_GROUP = SSD_HEADS // SSD_GROUPS
SSD_STATE = 128
SSD_CHUNK = 128
XBC_WIDTH = SSD_WIDTH + 2 * SSD_GROUPS * SSD_STATE
BC_WIDTH = 2 * SSD_GROUPS * SSD_STATE
FNET_WIDTH = 256
FNET_GROUPS = 4
FNET_GDIM = 64
POOL_WINDOWS = (2, 4, 8, 16)
POOL_WIDTH = 256
POOL_GDIM = 64
POOL_HALO = 8
D_FF = 2816
ALPHA = (2.0 * DEPTH) ** 0.25
EPS = 1e-6

OFF_XBC = SSD_WIDTH
OFF_DT = OFF_XBC + XBC_WIDTH
OFF_FNET = OFF_DT + 2 * SSD_HEADS
OFF_POOL = OFF_FNET + FNET_WIDTH
N_IN = OFF_POOL + POOL_WIDTH

LANE = 128
C_Z = 0
C_XBC = C_Z + SSD_WIDTH
C_FNET = C_XBC + XBC_WIDTH
C_POOL = C_FNET + FNET_WIDTH
C_DT = C_POOL + POOL_WIDTH
N_CAT = C_DT + 2 * LANE

DFT_N2 = 64
DFT_NB = 8
FFN_CB = 256
FFN_HALO = GRID_W
FFN_PAD = 8
FFN_ROW_CHUNKS = 4
VMEM_LIMIT = 56 * 1024 * 1024


def _ln(x):
    mu = jnp.mean(x, axis=-1, keepdims=True)
    xc = x - mu
    var = jnp.mean(xc * xc, axis=-1, keepdims=True)
    return xc * lax.rsqrt(var + EPS)


def _bdot(a, b):
    return jnp.dot(a.astype(BF16), b.astype(BF16), preferred_element_type=F32)


def _params(*sem):
    return pltpu.CompilerParams(dimension_semantics=sem, vmem_limit_bytes=VMEM_LIMIT)


def _ada_kernel(c_ref, w_ref, b_ref, o_ref):
    cv = c_ref[...]
    s = cv * jax.nn.sigmoid(cv)
    o_ref[0] = jnp.dot(s, w_ref[0], preferred_element_type=F32, precision=HIGHEST) + b_ref[0]


def _ada_call(cv, w_ada, b_ada):
    depth, d, n = w_ada.shape
    tn = 1536
    return pl.pallas_call(
        _ada_kernel,
        grid=(depth, n // tn),
        in_specs=[
            pl.BlockSpec((8, d), lambda l, j: (0, 0)),
            pl.BlockSpec((1, d, tn), lambda l, j: (l, 0, j)),
            pl.BlockSpec((1, 1, tn), lambda l, j: (l, 0, j)),
        ],
        out_specs=pl.BlockSpec((1, 8, tn), lambda l, j: (l, 0, j)),
        out_shape=jax.ShapeDtypeStruct((depth, 8, n), F32),
        compiler_params=_params("parallel", "parallel"),
        name="ada",
    )(cv, w_ada, b_ada.reshape(depth, 1, n))


def _inproj_kernel(x_ref, xp_ref, xn_ref, sh_ref, sc_ref, w_ref, cw_ref, cb_ref, dtb_ref, alog_ref,
                   cc_ref, cs_ref, fw_ref, pw_ref, ps_ref,
                   z_ref, xs_ref, bc_ref, dt_ref, da_ref, pq_ref, yp_ref, *, tm, seq_len):
    i = pl.program_id(1)
    nt = pl.num_programs(1)
    sh = sh_ref[0]
    sc = sc_ref[0]

    def modulated(xv):
        return (_ln(xv) * (1.0 + sc) + sh).astype(BF16)

    u = jnp.dot(modulated(x_ref[0]), w_ref[...], preferred_element_type=F32)
    w_halo = w_ref[:, C_XBC:C_DT]
    up = jnp.dot(modulated(xp_ref[0]), w_halo, preferred_element_type=F32)
    un = jnp.dot(modulated(xn_ref[0]), w_halo, preferred_element_type=F32)
    up = jnp.where(i > 0, up, 0.0)
    un = jnp.where(i < nt - 1, un, 0.0)

    z_ref[0] = u[:, C_Z:C_XBC]

    xbc = u[:, C_XBC:C_FNET]
    row = lax.broadcasted_iota(jnp.int32, (tm, 1), 0)
    prev_row = up[POOL_HALO - 1:POOL_HALO, 0:XBC_WIDTH]
    next_row = un[0:1, 0:XBC_WIDTH]
    xm1 = jnp.where(row == 0, prev_row, pltpu.roll(xbc, 1, 0))
    xp1 = jnp.where(row == tm - 1, next_row, pltpu.roll(xbc, tm - 1, 0))
    cw = cw_ref[...]
    conv = xm1 * cw[0:1] + xbc * cw[1:2] + xp1 * cw[2:3] + cb_ref[...]
    act = conv * jax.nn.sigmoid(conv)
    xs_ref[0] = act[:, 0:SSD_WIDTH]
    bc_ref[0] = act[:, SSD_WIDTH:XBC_WIDTH]

    lane = lax.broadcasted_iota(jnp.int32, (1, LANE), 1)
    for d in range(2):
        raw = u[:, C_DT + d * LANE:C_DT + (d + 1) * LANE] + dtb_ref[d:d + 1, :]
        sp = jnp.maximum(raw, 0.0) + jnp.log1p(jnp.exp(-jnp.abs(raw)))
        dt = jnp.where(lane < SSD_HEADS, sp, 0.0)
        dt_ref[0, d] = dt
        da_ref[0, d] = dt * (-jnp.exp(alog_ref[d:d + 1, :]))

    uf = u[:, C_FNET:C_POOL]
    fw = fw_ref[...]
    cw_mix = jnp.dot(cc_ref[...], fw, preferred_element_type=F32, precision=HIGHEST)
    sw_mix = jnp.dot(cs_ref[...], fw, preferred_element_type=F32, precision=HIGHEST)
    pq_ref[0, 0] = _bdot(uf, cw_mix)
    pq_ref[0, 1] = _bdot(uf, sw_mix)

    off_pool = C_POOL - C_XBC
    ext = jnp.concatenate([up[:, off_pool:off_pool + POOL_WIDTH], u[:, C_POOL:C_DT],
                           un[:, off_pool:off_pool + POOL_WIDTH]], axis=0)
    n_ext = tm + 2 * POOL_HALO
    sums = {}
    cur = ext
    width = 1
    while width < POOL_WINDOWS[-1]:
        cur = cur + pltpu.roll(cur, n_ext - width, 0)
        width *= 2
        sums[width] = cur
    lane_p = lax.broadcasted_iota(jnp.int32, (1, POOL_WIDTH), 1)
    pos = i * tm + lax.broadcasted_iota(jnp.int32, (tm, 1), 0)
    wsum = None
    cnt = None
    for gi, win in enumerate(POOL_WINDOWS):
        left = win // 2
        start = POOL_HALO - left
        s_w = sums[win]
        if start:
            s_w = pltpu.roll(s_w, n_ext - start, 0)
        s_w = s_w[0:tm]
        c_w = jnp.minimum(pos + (win - left), seq_len) - jnp.maximum(pos - left, 0)
        if wsum is None:
            wsum, cnt = s_w, c_w
        else:
            sel = lane_p >= gi * POOL_GDIM
            wsum = jnp.where(sel, s_w, wsum)
            cnt = jnp.where(sel, c_w, cnt)
    pooled = wsum / cnt.astype(F32) - u[:, C_POOL:C_DT]
    yp_ref[0] = _bdot(pooled, pw_ref[...]) * ps_ref[...]


def _inproj_call(x, sh, sc, w_cat, conv_w, conv_b, dt_bias, a_log, cc, cs, fw, pw, ps, *, tm):
    bsz, seq, d = x.shape
    nt = seq // tm
    nb8 = seq // 8
    tb = tm // 8
    mod_map = (lambda b, i: (b, 0, 0)) if sh.shape[0] > 1 else (lambda b, i: (0, 0, 0))

    def full(a):
        return pl.BlockSpec(a.shape, lambda b, i: (0,) * a.ndim)

    in_specs = [
        pl.BlockSpec((1, tm, d), lambda b, i: (b, i, 0)),
        pl.BlockSpec((1, 8, d), lambda b, i: (b, jnp.maximum(i * tb - 1, 0), 0)),
        pl.BlockSpec((1, 8, d), lambda b, i: (b, jnp.minimum((i + 1) * tb, nb8 - 1), 0)),
        pl.BlockSpec((1, 1, d), mod_map),
        pl.BlockSpec((1, 1, d), mod_map),
    ] + [full(a) for a in (w_cat, conv_w, conv_b, dt_bias, a_log, cc, cs, fw, pw, ps)]
    out_shape = [
        jax.ShapeDtypeStruct((bsz, seq, SSD_WIDTH), F32),
        jax.ShapeDtypeStruct((bsz, seq, SSD_WIDTH), F32),
        jax.ShapeDtypeStruct((bsz, seq, BC_WIDTH), F32),
        jax.ShapeDtypeStruct((bsz, 2, seq, LANE), F32),
        jax.ShapeDtypeStruct((bsz, 2, seq, LANE), F32),
        jax.ShapeDtypeStruct((bsz, 2, seq, FNET_WIDTH), F32),
        jax.ShapeDtypeStruct((bsz, seq, POOL_WIDTH), F32),
    ]
    out_specs = [
        pl.BlockSpec((1, tm, SSD_WIDTH), lambda b, i: (b, i, 0)),
        pl.BlockSpec((1, tm, SSD_WIDTH), lambda b, i: (b, i, 0)),
        pl.BlockSpec((1, tm, BC_WIDTH), lambda b, i: (b, i, 0)),
        pl.BlockSpec((1, 2, tm, LANE), lambda b, i: (b, 0, i, 0)),
        pl.BlockSpec((1, 2, tm, LANE), lambda b, i: (b, 0, i, 0)),
        pl.BlockSpec((1, 2, tm, FNET_WIDTH), lambda b, i: (b, 0, i, 0)),
        pl.BlockSpec((1, tm, POOL_WIDTH), lambda b, i: (b, i, 0)),
    ]
    return pl.pallas_call(
        functools.partial(_inproj_kernel, tm=tm, seq_len=seq),
        grid=(bsz, nt),
        in_specs=in_specs,
        out_specs=out_specs,
        out_shape=out_shape,
        compiler_params=_params("parallel", "parallel"),
        name="inproj",
    )(x, x, x, sh, sc, w_cat, conv_w, conv_b, dt_bias, a_log, cc, cs, fw, pw, ps)


def _ssd_kernel(xs_ref, bc_ref, dt_ref, da_ref, tri_ref, init_ref, y_ref, fin_ref, st_ref):
    s = pl.program_id(2)
    ns = pl.num_programs(2)

    @pl.when(s == 0)
    def _():
        st_ref[...] = init_ref[0, 0]

    tri = tri_ref[0]
    a = da_ref[0, 0]
    dt = dt_ref[0, 0]
    cs = jnp.dot(tri, a, preferred_element_type=F32, precision=HIGHEST)
    cs_t = cs.T
    tot = jnp.sum(a, axis=0, keepdims=True)
    mask = tri > 0.5
    e_cs = jnp.exp(cs)
    e_dec = jnp.exp(tot - cs)
    e_tot = jnp.exp(tot)
    xs = xs_ref[0]
    bc = bc_ref[0]
    gn = SSD_GROUPS * SSD_STATE
    for g in range(SSD_GROUPS):
        bg_t = bc[:, g * SSD_STATE:(g + 1) * SSD_STATE].T.astype(BF16)
        cg = bc[:, gn + g * SSD_STATE:gn + (g + 1) * SSD_STATE].astype(BF16)
        cb = jnp.dot(cg, bg_t, preferred_element_type=F32)
        for hh in range(HEADS_PER_GROUP):
            h = g * HEADS_PER_GROUP + hh
            seg = cs[:, h:h + 1] - cs_t[h:h + 1, :]
            lmat = jnp.where(mask, jnp.exp(jnp.minimum(seg, 0.0)), 0.0)
            xd = xs[:, h * SSD_HEADDIM:(h + 1) * SSD_HEADDIM] * dt[:, h:h + 1]
            st = st_ref[h]
            y_diag = _bdot(cb * lmat, xd)
            y_off = jnp.dot(cg, st.astype(BF16), preferred_element_type=F32) * e_cs[:, h:h + 1]
            y_ref[0, 0, :, h * SSD_HEADDIM:(h + 1) * SSD_HEADDIM] = y_diag + y_off
            upd = jnp.dot(bg_t, (xd * e_dec[:, h:h + 1]).astype(BF16), preferred_element_type=F32)
            st_ref[h] = st * e_tot[:, h:h + 1] + upd

    @pl.when(s == ns - 1)
    def _():
        fin_ref[0, 0] = st_ref[...]


def _ssd_call(xs, bc, dt, da, tri, init):
    bsz, seq, _ = xs.shape
    nc = seq // SSD_CHUNK

    def chunk(d, s):
        return s + d * (nc - 1 - 2 * s)

    st_block = (1, 1, SSD_HEADS, SSD_STATE, SSD_HEADDIM)
    return pl.pallas_call(
        _ssd_kernel,
        grid=(bsz, 2, nc),
        in_specs=[
            pl.BlockSpec((1, SSD_CHUNK, SSD_WIDTH), lambda b, d, s: (b, chunk(d, s), 0)),
            pl.BlockSpec((1, SSD_CHUNK, BC_WIDTH), lambda b, d, s: (b, chunk(d, s), 0)),
            pl.BlockSpec((1, 1, SSD_CHUNK, LANE), lambda b, d, s: (b, d, chunk(d, s), 0)),
            pl.BlockSpec((1, 1, SSD_CHUNK, LANE), lambda b, d, s: (b, d, chunk(d, s), 0)),
            pl.BlockSpec((1, SSD_CHUNK, SSD_CHUNK), lambda b, d, s: (d, 0, 0)),
            pl.BlockSpec(st_block, lambda b, d, s: (b, d, 0, 0, 0)),
        ],
        out_specs=[
            pl.BlockSpec((1, 1, SSD_CHUNK, SSD_WIDTH), lambda b, d, s: (d, b, chunk(d, s), 0)),
            pl.BlockSpec(st_block, lambda b, d, s: (b, d, 0, 0, 0)),
        ],
        out_shape=[
            jax.ShapeDtypeStruct((2, bsz, seq, SSD_WIDTH), F32),
            jax.ShapeDtypeStruct((bsz, 2, SSD_HEADS, SSD_STATE, SSD_HEADDIM), F32),
        ],
        scratch_shapes=[pltpu.VMEM((SSD_HEADS, SSD_STATE, SSD_HEADDIM), F32)],
        compiler_params=_params("parallel", "parallel", "arbitrary"),
        name="ssd",
    )(xs, bc, dt, da, tri, init)


def _dft1_kernel(p_ref, q_ref, m_ref, t_ref, *, n1):
    for j in range(DFT_NB):
        m = m_ref[j]
        rp = _bdot(m, p_ref[0, 0, :, j * FNET_WIDTH:(j + 1) * FNET_WIDTH])
        rq = _bdot(m, q_ref[0, 0, :, j * FNET_WIDTH:(j + 1) * FNET_WIDTH])
        t_ref[0, 0, j] = rp[:n1] + rq[n1:]
        t_ref[0, 1, j] = rp[n1:] - rq[:n1]


def _dft1_call(pq, mtab):
    bsz, _, seq, w = pq.shape
    n1 = seq // DFT_N2
    pqv = pq.reshape(bsz, 2, n1, DFT_N2 * w)
    cols = DFT_NB * w
    return pl.pallas_call(
        functools.partial(_dft1_kernel, n1=n1),
        grid=(bsz, DFT_N2 // DFT_NB),
        in_specs=[
            pl.BlockSpec((1, 1, n1, cols), lambda b, j: (b, 0, 0, j)),
            pl.BlockSpec((1, 1, n1, cols), lambda b, j: (b, 1, 0, j)),
            pl.BlockSpec((DFT_NB, 2 * n1, n1), lambda b, j: (j, 0, 0)),
        ],
        out_specs=pl.BlockSpec((1, 2, DFT_NB, n1, w), lambda b, j: (b, 0, j, 0, 0)),
        out_shape=jax.ShapeDtypeStruct((bsz, 2, DFT_N2, n1, w), F32),
        compiler_params=_params("parallel", "parallel"),
        name="dft1",
    )(pqv, pqv, mtab)


def _dft2_kernel(d_ref, t_ref, o_ref):
    o_ref[0] = _bdot(d_ref[...], t_ref[0])


def _dft2_call(dmat, t, *, tn):
    bsz, k, n = t.shape
    mo = dmat.shape[0]
    return pl.pallas_call(
        _dft2_kernel,
        grid=(bsz, n // tn),
        in_specs=[
            pl.BlockSpec((mo, k), lambda b, j: (0, 0)),
            pl.BlockSpec((1, k, tn), lambda b, j: (b, 0, j)),
        ],
        out_specs=pl.BlockSpec((1, mo, tn), lambda b, j: (b, 0, j)),
        out_shape=jax.ShapeDtypeStruct((bsz, mo, n), F32),
        compiler_params=_params("parallel", "parallel"),
        name="dft2",
    )(dmat, t)


def _mix_kernel(yf_ref, yb_ref, xs_ref, z_ref, yfn_ref, ypl_ref, x_ref, dsk_ref, nw_ref, wo_ref, g1_ref,
                lg_ref, lb_ref, sh2_ref, sc2_ref, x1_ref, h2_ref):
    y = yf_ref[0, 0] + yb_ref[0, 0] + xs_ref[0] * dsk_ref[...]
    z = z_ref[0]
    y = y * (z * jax.nn.sigmoid(z))
    gw = SSD_WIDTH // SSD_GROUPS
    parts = []
    for g in range(SSD_GROUPS):
        yg = y[:, g * gw:(g + 1) * gw]
        ms = jnp.mean(yg * yg, axis=-1, keepdims=True)
        parts.append(yg * lax.rsqrt(ms + EPS))
    yn = jnp.concatenate(parts, axis=-1) * nw_ref[...]
    o_f = SSD_WIDTH
    o_p = SSD_WIDTH + FNET_WIDTH
    mix = (_bdot(yn, wo_ref[0:o_f, :]) + _bdot(yfn_ref[0], wo_ref[o_f:o_p, :])
           + _bdot(ypl_ref[0], wo_ref[o_p:, :]))
    x1 = _ln(ALPHA * x_ref[0] + g1_ref[0] * mix) * lg_ref[...] + lb_ref[...]
    x1_ref[0] = x1
    h2_ref[0] = (_ln(x1) * (1.0 + sc2_ref[0]) + sh2_ref[0]).astype(BF16)


def _mix_call(y, xs, z, yfn, ypl, x, dsk, nw, wo, g1, lg, lb, sh2, sc2, *, tm):
    bsz, seq, d = x.shape
    mod_map = (lambda b, i: (b, 0, 0)) if g1.shape[0] > 1 else (lambda b, i: (0, 0, 0))

    def full(a):
        return pl.BlockSpec(a.shape, lambda b, i: (0,) * a.ndim)

    def tok(w):
        return pl.BlockSpec((1, tm, w), lambda b, i: (b, i, 0))

    mod = pl.BlockSpec((1, 1, d), mod_map)
    return pl.pallas_call(
        _mix_kernel,
        grid=(bsz, seq // tm),
        in_specs=[
            pl.BlockSpec((1, 1, tm, SSD_WIDTH), lambda b, i: (0, b, i, 0)),
            pl.BlockSpec((1, 1, tm, SSD_WIDTH), lambda b, i: (1, b, i, 0)),
            tok(SSD_WIDTH), tok(SSD_WIDTH), tok(FNET_WIDTH), tok(POOL_WIDTH), tok(d),
            full(dsk), full(nw), full(wo), mod, full(lg), full(lb), mod, mod,
        ],
        out_specs=[tok(d), tok(d)],
        out_shape=[jax.ShapeDtypeStruct((bsz, seq, d), F32), jax.ShapeDtypeStruct((bsz, seq, d), BF16)],
        compiler_params=_params("parallel", "parallel"),
        name="mix",
    )(y, y, xs, z, yfn, ypl, x, dsk, nw, wo, g1, lg, lb, sh2, sc2)


def _ffn_kernel(h_ref, hp_ref, hn_ref, wv_ref, wg_ref, cwv_ref, cwg_ref, cbv_ref, cbg_ref, wd_ref, x1_ref,
                g2_ref, lg_ref, lb_ref, o_ref, hbuf, av0, ag0, av1, ag1, acc, *, tm, on_grid):
    i = pl.program_id(1)
    j = pl.program_id(2)
    nt = pl.num_programs(1)
    nj = pl.num_programs(2) - 1
    ext = tm + 2 * FFN_HALO
    ra = ext // FFN_ROW_CHUNKS
    rb = tm // FFN_ROW_CHUNKS
    slots = ((av0, ag0), (av1, ag1))
    base = FFN_PAD + FFN_HALO
    rows = (-1, 0, 1) if on_grid else (0,)

    def up_chunk(slot, r):
        av, ag = slot
        hb = hbuf[r * ra:(r + 1) * ra, :]
        av[FFN_PAD + r * ra:FFN_PAD + (r + 1) * ra, :] = jnp.dot(hb, wv_ref[...], preferred_element_type=F32)
        ag[FFN_PAD + r * ra:FFN_PAD + (r + 1) * ra, :] = jnp.dot(hb, wg_ref[...], preferred_element_type=F32)

    def conv(a_ref, cw_ref, cb_ref, r):
        cw = cw_ref[...]
        col = (lax.broadcasted_iota(jnp.int32, (rb, 1), 0) + r * rb) & (GRID_W - 1)
        side = []
        for dc in (-1, 0, 1):
            acc_dc = None
            for dr in rows:
                k = (dr + 1) * 3 + dc + 1
                term = a_ref[pl.ds(base + r * rb + dr * GRID_W + dc, rb), :] * cw[k:k + 1]
                acc_dc = term if acc_dc is None else acc_dc + term
            side.append(acc_dc)
        left, mid, right = side
        if on_grid:
            left = jnp.where(col != 0, left, 0.0)
            right = jnp.where(col != GRID_W - 1, right, 0.0)
        return left + mid + right + cb_ref[...]

    def down_chunk(slot, r):
        av, ag = slot
        val = conv(av, cwv_ref, cbv_ref, r)
        gate = conv(ag, cwg_ref, cbg_ref, r)
        hidden = (jax.nn.gelu(gate, approximate=True) * val).astype(BF16)
        acc[r * rb:(r + 1) * rb, :] += jnp.dot(hidden, wd_ref[...], preferred_element_type=F32)

    @pl.when(j == 0)
    def _():
        hbuf[0:FFN_HALO] = jnp.where(i > 0, hp_ref[0], jnp.zeros_like(hp_ref[0]))
        hbuf[FFN_HALO:FFN_HALO + tm] = h_ref[0]
        hbuf[FFN_HALO + tm:] = jnp.where(i < nt - 1, hn_ref[0], jnp.zeros_like(hn_ref[0]))
        acc[...] = jnp.zeros_like(acc)
        for a in (av0, ag0, av1, ag1):
            a[0:FFN_PAD] = jnp.zeros((FFN_PAD, FFN_CB), F32)
            a[FFN_PAD + ext:] = jnp.zeros((FFN_PAD, FFN_CB), F32)
        for r in range(FFN_ROW_CHUNKS):
            up_chunk(slots[0], r)

    for parity in (0, 1):
        @pl.when((j > 0) & (j < nj) & (j % 2 == parity))
        def _():
            for r in range(FFN_ROW_CHUNKS):
                up_chunk(slots[parity], r)
                down_chunk(slots[1 - parity], r)

    @pl.when(j == nj)
    def _():
        for r in range(FFN_ROW_CHUNKS):
            down_chunk(slots[(D_FF // FFN_CB - 1) % 2], r)
        o_ref[0] = _ln(ALPHA * x1_ref[0] + g2_ref[0] * acc[...]) * lg_ref[...] + lb_ref[...]


def _ffn_call(h, w_up, conv_w9, conv_b, w_down, x1, g2, lg, lb, *, tm, on_grid):
    bsz, seq, d = h.shape
    nt = seq // tm
    nj = D_FF // FFN_CB
    th = tm // FFN_HALO
    nh = seq // FFN_HALO
    ext = tm + 2 * FFN_HALO
    mod_map = (lambda b, i, j: (b, 0, 0)) if g2.shape[0] > 1 else (lambda b, i, j: (0, 0, 0))

    def full(a):
        return pl.BlockSpec(a.shape, lambda b, i, j: (0,) * a.ndim)

    def up_blk(j):
        return jnp.minimum(j, nj - 1)

    def dn_blk(j):
        return jnp.maximum(j - 1, 0)

    a_scratch = pltpu.VMEM((ext + 2 * FFN_PAD, FFN_CB), F32)
    return pl.pallas_call(
        functools.partial(_ffn_kernel, tm=tm, on_grid=on_grid),
        grid=(bsz, nt, nj + 1),
        in_specs=[
            pl.BlockSpec((1, tm, d), lambda b, i, j: (b, i, 0)),
            pl.BlockSpec((1, FFN_HALO, d), lambda b, i, j: (b, jnp.maximum(i * th - 1, 0), 0)),
            pl.BlockSpec((1, FFN_HALO, d), lambda b, i, j: (b, jnp.minimum((i + 1) * th, nh - 1), 0)),
            pl.BlockSpec((d, FFN_CB), lambda b, i, j: (0, up_blk(j))),
            pl.BlockSpec((d, FFN_CB), lambda b, i, j: (0, nj + up_blk(j))),
            pl.BlockSpec((9, FFN_CB), lambda b, i, j: (0, dn_blk(j))),
            pl.BlockSpec((9, FFN_CB), lambda b, i, j: (0, nj + dn_blk(j))),
            pl.BlockSpec((1, FFN_CB), lambda b, i, j: (0, dn_blk(j))),
            pl.BlockSpec((1, FFN_CB), lambda b, i, j: (0, nj + dn_blk(j))),
            pl.BlockSpec((FFN_CB, d), lambda b, i, j: (dn_blk(j), 0)),
            pl.BlockSpec((1, tm, d), lambda b, i, j: (b, i, 0)),
            pl.BlockSpec((1, 1, d), mod_map),
            full(lg), full(lb),
        ],
        out_specs=pl.BlockSpec((1, tm, d), lambda b, i, j: (b, i, 0)),
        out_shape=jax.ShapeDtypeStruct((bsz, seq, d), F32),
        scratch_shapes=[pltpu.VMEM((ext, d), BF16), a_scratch, a_scratch, a_scratch, a_scratch,
                        pltpu.VMEM((tm, d), F32)],
        compiler_params=_params("parallel", "parallel", "arbitrary"),
        name="ffn",
    )(h, h, h, w_up, w_up, conv_w9, conv_w9, conv_b, conv_b, w_down, x1, g2, lg, lb)


def _block_diag(blocks):
    g, n, m = blocks.shape
    out = jnp.zeros((g * n, g * m), blocks.dtype)
    for k in range(g):
        out = out.at[k * n:(k + 1) * n, k * m:(k + 1) * m].set(blocks[k])
    return out


def _dft_tables(seq):
    n1 = seq // DFT_N2
    k1 = np.arange(n1, dtype=np.int64)[:, None]
    nn1 = np.arange(n1, dtype=np.int64)[None, :]
    tabs = []
    for n2 in range(DFT_N2):
        th = 2.0 * np.pi * ((k1 * (DFT_N2 * nn1 + n2)) % seq) / seq
        tabs.append(np.concatenate([np.cos(th), -np.sin(th)], axis=0))
    mtab = np.stack(tabs)
    k2 = np.arange(DFT_N2, dtype=np.int64)[:, None]
    nn2 = np.arange(DFT_N2, dtype=np.int64)[None, :]
    th2 = 2.0 * np.pi * ((k2 * nn2) % DFT_N2) / DFT_N2
    scale = 1.0 / math.sqrt(seq * FNET_GDIM)
    dmat = scale * np.concatenate([np.cos(th2), np.sin(th2)], axis=1)
    return jnp.asarray(mtab, F32), jnp.asarray(dmat, F32)


def _dense_dft_table(seq):
    k = np.arange(seq, dtype=np.int64)
    th = 2.0 * np.pi * ((k[:, None] * k[None, :]) % seq) / seq
    scale = 1.0 / math.sqrt(seq * FNET_GDIM)
    return jnp.asarray(scale * np.concatenate([np.cos(th), -np.sin(th)], axis=1), F32)


def _channel_dft_tables():
    k = np.arange(FNET_GDIM, dtype=np.int64)
    th = 2.0 * np.pi * ((k[:, None] * k[None, :]) % FNET_GDIM) / FNET_GDIM
    eye = np.eye(FNET_GROUPS)
    return jnp.asarray(np.kron(eye, np.cos(th)), F32), jnp.asarray(np.kron(eye, np.sin(th)), F32)


def _scan_matrices():
    r = np.arange(SSD_CHUNK)
    lower = (r[:, None] >= r[None, :]).astype(np.float32)
    return jnp.asarray(np.stack([lower, lower.T]), F32)


def _token_mixer(x, sh1, sc1, lw, init, tri, cc, cs, *, tm):
    bsz, seq, _ = x.shape
    z, xs, bc, dt, da, pq, ypl = _inproj_call(
        x, sh1, sc1, lw["w_cat"], lw["ssd_conv_w"], lw["ssd_conv_b"], lw["dt_bias"], lw["a_log"], cc, cs,
        lw["fnet_bd"], lw["pool_bd"], lw["pool_scale"], tm=tm)
    y, fin = _ssd_call(xs, bc, dt, da, tri, init)
    return z, xs, y, pq, ypl, fin


def _fourier_positions(pq):
    bsz, _, seq, w = pq.shape
    if seq % (DFT_N2 * 8) == 0:
        n1 = seq // DFT_N2
        mtab, dmat = _dft_tables(seq)
        t = _dft1_call(pq, mtab).reshape(bsz, 2 * DFT_N2, n1 * w)
        out = _dft2_call(dmat, t, tn=min(4096, n1 * w))
        return out.reshape(bsz, seq, w)
    return _dft2_call(_dense_dft_table(seq), pq.reshape(bsz, 2 * seq, w), tn=w)


def kernel(x, c, ctx, c_ctx, w_ada, b_ada, w_in, ssd_conv_w, ssd_conv_b, ssd_dt_bias, ssd_a_log, ssd_d,
           ssd_norm_w, fnet_w, pool_w, pool_scale, w_out, ln1_g, ln1_b, ffn_w_up, ffn_conv_w, ffn_conv_b,
           ffn_w_down, ln2_g, ln2_b):
    bsz, seq, d = x.shape
    n_ctx = ctx.shape[1]
    assert d == D_MODEL and bsz <= 7 and seq % (GRID_W * 8) == 0 and n_ctx % SSD_CHUNK == 0
    tm_lat = min(512, seq)
    tm_ffn = min(1024, seq)

    cv = jnp.zeros((8, d), F32).at[0:bsz].set(c).at[bsz].set(c_ctx)
    mods = _ada_call(cv, w_ada, b_ada)
    tri = _scan_matrices()
    cc, cs = _channel_dft_tables()
    zero_state = jnp.zeros((bsz, 2, SSD_HEADS, SSD_STATE, SSD_HEADDIM), F32)
    lane_pad = jnp.zeros((d, LANE - SSD_HEADS), F32)

    for l in range(DEPTH):
        last = l == DEPTH - 1
        m = mods[l].reshape(8, 6, d)
        lat = [m[0:bsz, k][:, None, :] for k in range(6)]
        cx = [m[bsz:bsz + 1, k][:, None, :] for k in range(6)]
        w = w_in[l]
        w_cat = jnp.concatenate(
            [w[:, 0:OFF_DT], w[:, OFF_FNET:N_IN], w[:, OFF_DT:OFF_DT + SSD_HEADS], lane_pad,
             w[:, OFF_DT + SSD_HEADS:OFF_FNET], lane_pad], axis=1).astype(BF16)
        pad16 = ((0, 0), (0, LANE - SSD_HEADS))
        lw = {
            "w_cat": w_cat,
            "ssd_conv_w": ssd_conv_w[l],
            "ssd_conv_b": ssd_conv_b[l][None, :],
            "dt_bias": jnp.pad(ssd_dt_bias[l], pad16),
            "a_log": jnp.pad(ssd_a_log[l], pad16),
            "fnet_bd": _block_diag(fnet_w[l]),
            "pool_bd": _block_diag(pool_w[l]).astype(BF16),
            "pool_scale": pool_scale[l][None, :],
        }
        dsk = jnp.repeat(ssd_d[l], SSD_HEADDIM)[None, :]
        nw = ssd_norm_w[l][None, :]
        wo = w_out[l].astype(BF16)
        lg1, lb1 = ln1_g[l][None, :], ln1_b[l][None, :]
        lg2, lb2 = ln2_g[l][None, :], ln2_b[l][None, :]
        w_up = ffn_w_up[l].astype(BF16)
        w_down = ffn_w_down[l].astype(BF16)
        conv9 = ffn_conv_w[l].reshape(9, 2 * D_FF)
        conv_b = ffn_conv_b[l][None, :]

        cz, cxs, cy, cpq, cypl, c_fin = _token_mixer(ctx, cx[0], cx[1], lw, zero_state, tri, cc, cs, tm=n_ctx)
        z, xs, y, pq, ypl, _ = _token_mixer(x, lat[0], lat[1], lw, c_fin, tri, cc, cs, tm=tm_lat)

        yfn = _fourier_positions(pq)
        x1, h2 = _mix_call(y, xs, z, yfn, ypl, x, dsk, nw, wo, lat[2], lg1, lb1, lat[3], lat[4], tm=tm_lat)
        x = _ffn_call(h2, w_up, conv9, conv_b, w_down, x1, lat[5], lg2, lb2, tm=tm_ffn, on_grid=True)

        if not last:
            cyfn = _fourier_positions(cpq)
            c1, ch2 = _mix_call(cy, cxs, cz, cyfn, cypl, ctx, dsk, nw, wo, cx[2], lg1, lb1, cx[3], cx[4],
                                tm=n_ctx)
            ctx = _ffn_call(ch2, w_up, conv9, conv_b, w_down, c1, cx[5], lg2, lb2, tm=n_ctx, on_grid=False)
    return x
```

```python
import functools
import math

import numpy as np
import jax
import jax.numpy as jnp
from jax import lax
from jax.experimental import pallas as pl
from jax.experimental.pallas import tpu as pltpu

F32 = jnp.float32
BF16 = jnp.bfloat16
HIGHEST = lax.Precision.HIGHEST

D_MODEL = 1024
DEPTH = 2
GRID_W = 64
SSD_WIDTH = 512
SSD_HEADDIM = 64
SSD_HEADS = 8
SSD_GROUPS = 2
HEADS_PER_GROUP = SSD_HEADS // SSD_GROUPS
SSD_STATE = 128
SSD_CHUNK = 128
SSD_CHUNKS_PER_STEP = 4
XBC_WIDTH = SSD_WIDTH + 2 * SSD_GROUPS * SSD_STATE
BC_WIDTH = 2 * SSD_GROUPS * SSD_STATE
SSD_STATE_SHAPE = (2, SSD_HEADS // 2, SSD_STATE, 2 * SSD_HEADDIM)
FNET_WIDTH = 256
FNET_GROUPS = 4
FNET_GDIM = 64
POOL_WINDOWS = (2, 4, 8, 16)
POOL_WIDTH = 256
POOL_GDIM = 64
POOL_HALO = 8
D_FF = 2816
ALPHA = (2.0 * DEPTH) ** 0.25
EPS = 1e-6

OFF_XBC = SSD_WIDTH
OFF_DT = OFF_XBC + XBC_WIDTH
OFF_FNET = OFF_DT + 2 * SSD_HEADS
OFF_POOL = OFF_FNET + FNET_WIDTH
N_IN = OFF_POOL + POOL_WIDTH

LANE = 128
C_Z = 0
C_XBC = C_Z + SSD_WIDTH
C_FNET = C_XBC + XBC_WIDTH
C_POOL = C_FNET + FNET_WIDTH
C_DT = C_POOL + POOL_WIDTH
N_CAT = C_DT + 2 * LANE

DFT_N2 = 64
DFT_NB = 8
FFN_CB = 256
FFN_HALO = GRID_W
FFN_PAD = 8
FFN_ROW_CHUNKS = 4
VMEM_LIMIT = 56 * 1024 * 1024


def _ln(x):
    mu = jnp.mean(x, axis=-1, keepdims=True)
    xc = x - mu
    var = jnp.mean(xc * xc, axis=-1, keepdims=True)
    return xc * lax.rsqrt(var + EPS)


def _bdot(a, b):
    return jnp.dot(a.astype(BF16), b.astype(BF16), preferred_element_type=F32)


def _params(*sem):
    return pltpu.CompilerParams(dimension_semantics=sem, vmem_limit_bytes=VMEM_LIMIT)


def _ada_kernel(c_ref, w_ref, b_ref, o_ref):
    cv = c_ref[...]
    s = cv * jax.nn.sigmoid(cv)
    o_ref[0] = jnp.dot(s, w_ref[0], preferred_element_type=F32, precision=HIGHEST) + b_ref[0]


def _ada_call(cv, w_ada, b_ada):
    depth, d, n = w_ada.shape
    tn = 1536
    return pl.pallas_call(
        _ada_kernel,
        grid=(depth, n // tn),
        in_specs=[
            pl.BlockSpec((8, d), lambda l, j: (0, 0)),
            pl.BlockSpec((1, d, tn), lambda l, j: (l, 0, j)),
            pl.BlockSpec((1, 1, tn), lambda l, j: (l, 0, j)),
        ],
        out_specs=pl.BlockSpec((1, 8, tn), lambda l, j: (l, 0, j)),
        out_shape=jax.ShapeDtypeStruct((depth, 8, n), F32),
        compiler_params=_params("parallel", "parallel"),
        name="ada",
    )(cv, w_ada, b_ada.reshape(depth, 1, n))


def _inproj_kernel(x_ref, xp_ref, xn_ref, sh_ref, sc_ref, w_ref, cw_ref, cb_ref, dtb_ref, alog_ref,
                   cc_ref, cs_ref, fw_ref, pw_ref, ps_ref,
                   z_ref, xs_ref, bc_ref, dt_ref, da_ref, pq_ref, yp_ref, *, tm, seq_len):
    i = pl.program_id(1)
    nt = pl.num_programs(1)
    sh = sh_ref[0]
    sc = sc_ref[0]

    def modulated(xv):
        return (_ln(xv) * (1.0 + sc) + sh).astype(BF16)

    u = jnp.dot(modulated(x_ref[0]), w_ref[...], preferred_element_type=F32)
    w_halo = w_ref[:, C_XBC:C_DT]
    up = jnp.dot(modulated(xp_ref[0]), w_halo, preferred_element_type=F32)
    un = jnp.dot(modulated(xn_ref[0]), w_halo, preferred_element_type=F32)
    up = jnp.where(i > 0, up, 0.0)
    un = jnp.where(i < nt - 1, un, 0.0)

    z_ref[0] = u[:, C_Z:C_XBC]

    xbc = u[:, C_XBC:C_FNET]
    row = lax.broadcasted_iota(jnp.int32, (tm, 1), 0)
    prev_row = up[POOL_HALO - 1:POOL_HALO, 0:XBC_WIDTH]
    next_row = un[0:1, 0:XBC_WIDTH]
    xm1 = jnp.where(row == 0, prev_row, pltpu.roll(xbc, 1, 0))
    xp1 = jnp.where(row == tm - 1, next_row, pltpu.roll(xbc, tm - 1, 0))
    cw = cw_ref[...]
    conv = xm1 * cw[0:1] + xbc * cw[1:2] + xp1 * cw[2:3] + cb_ref[...]
    act = conv * jax.nn.sigmoid(conv)
    xs_ref[0] = act[:, 0:SSD_WIDTH]
    bc_ref[0] = act[:, SSD_WIDTH:XBC_WIDTH]

    lane = lax.broadcasted_iota(jnp.int32, (1, LANE), 1)
    for d in range(2):
        raw = u[:, C_DT + d * LANE:C_DT + (d + 1) * LANE] + dtb_ref[d:d + 1, :]
        sp = jnp.maximum(raw, 0.0) + jnp.log1p(jnp.exp(-jnp.abs(raw)))
        dt = jnp.where(lane < SSD_HEADS, sp, 0.0)
        dt_ref[0, d] = dt
        da_ref[0, d] = dt * (-jnp.exp(alog_ref[d:d + 1, :]))

    uf = u[:, C_FNET:C_POOL]
    fw = fw_ref[...]
    cw_mix = jnp.dot(cc_ref[...], fw, preferred_element_type=F32, precision=HIGHEST)
    sw_mix = jnp.dot(cs_ref[...], fw, preferred_element_type=F32, precision=HIGHEST)
    pq_ref[0, 0] = _bdot(uf, cw_mix)
    pq_ref[0, 1] = _bdot(uf, sw_mix)

    off_pool = C_POOL - C_XBC
    ext = jnp.concatenate([up[:, off_pool:off_pool + POOL_WIDTH], u[:, C_POOL:C_DT],
                           un[:, off_pool:off_pool + POOL_WIDTH]], axis=0)
    n_ext = tm + 2 * POOL_HALO
    sums = {}
    cur = ext
    width = 1
    while width < POOL_WINDOWS[-1]:
        cur = cur + pltpu.roll(cur, n_ext - width, 0)
        width *= 2
        sums[width] = cur
    lane_p = lax.broadcasted_iota(jnp.int32, (1, POOL_WIDTH), 1)
    pos = i * tm + lax.broadcasted_iota(jnp.int32, (tm, 1), 0)
    wsum = None
    cnt = None
    for gi, win in enumerate(POOL_WINDOWS):
        left = win // 2
        start = POOL_HALO - left
        s_w = sums[win]
        if start:
            s_w = pltpu.roll(s_w, n_ext - start, 0)
        s_w = s_w[0:tm]
        c_w = jnp.minimum(pos + (win - left), seq_len) - jnp.maximum(pos - left, 0)
        if wsum is None:
            wsum, cnt = s_w, c_w
        else:
            sel = lane_p >= gi * POOL_GDIM
            wsum = jnp.where(sel, s_w, wsum)
            cnt = jnp.where(sel, c_w, cnt)
    pooled = wsum / cnt.astype(F32) - u[:, C_POOL:C_DT]
    yp_ref[0] = _bdot(pooled, pw_ref[...]) * ps_ref[...]


def _inproj_call(x, sh, sc, w_cat, conv_w, conv_b, dt_bias, a_log, cc, cs, fw, pw, ps, *, tm):
    bsz, seq, d = x.shape
    nt = seq // tm
    nb8 = seq // 8
    tb = tm // 8
    mod_map = (lambda b, i: (b, 0, 0)) if sh.shape[0] > 1 else (lambda b, i: (0, 0, 0))

    def full(a):
        return pl.BlockSpec(a.shape, lambda b, i: (0,) * a.ndim)

    in_specs = [
        pl.BlockSpec((1, tm, d), lambda b, i: (b, i, 0)),
        pl.BlockSpec((1, 8, d), lambda b, i: (b, jnp.maximum(i * tb - 1, 0), 0)),
        pl.BlockSpec((1, 8, d), lambda b, i: (b, jnp.minimum((i + 1) * tb, nb8 - 1), 0)),
        pl.BlockSpec((1, 1, d), mod_map),
        pl.BlockSpec((1, 1, d), mod_map),
    ] + [full(a) for a in (w_cat, conv_w, conv_b, dt_bias, a_log, cc, cs, fw, pw, ps)]
    out_shape = [
        jax.ShapeDtypeStruct((bsz, seq, SSD_WIDTH), F32),
        jax.ShapeDtypeStruct((bsz, seq, SSD_WIDTH), F32),
        jax.ShapeDtypeStruct((bsz, seq, BC_WIDTH), F32),
        jax.ShapeDtypeStruct((bsz, 2, seq, LANE), F32),
        jax.ShapeDtypeStruct((bsz, 2, seq, LANE), F32),
        jax.ShapeDtypeStruct((bsz, 2, seq, FNET_WIDTH), F32),
        jax.ShapeDtypeStruct((bsz, seq, POOL_WIDTH), F32),
    ]
    out_specs = [
        pl.BlockSpec((1, tm, SSD_WIDTH), lambda b, i: (b, i, 0)),
        pl.BlockSpec((1, tm, SSD_WIDTH), lambda b, i: (b, i, 0)),
        pl.BlockSpec((1, tm, BC_WIDTH), lambda b, i: (b, i, 0)),
        pl.BlockSpec((1, 2, tm, LANE), lambda b, i: (b, 0, i, 0)),
        pl.BlockSpec((1, 2, tm, LANE), lambda b, i: (b, 0, i, 0)),
        pl.BlockSpec((1, 2, tm, FNET_WIDTH), lambda b, i: (b, 0, i, 0)),
        pl.BlockSpec((1, tm, POOL_WIDTH), lambda b, i: (b, i, 0)),
    ]
    return pl.pallas_call(
        functools.partial(_inproj_kernel, tm=tm, seq_len=seq),
        grid=(bsz, nt),
        in_specs=in_specs,
        out_specs=out_specs,
        out_shape=out_shape,
        compiler_params=_params("parallel", "parallel"),
        name="inproj",
    )(x, x, x, sh, sc, w_cat, conv_w, conv_b, dt_bias, a_log, cc, cs, fw, pw, ps)


def _bf16_pieces(x, terms):
    pieces = []
    rem = x
    for t in range(terms):
        piece = rem.astype(BF16)
        pieces.append(piece)
        if t + 1 < terms:
            rem = rem - piece.astype(F32)
    return pieces


def _ssd_chunk(d, k, xs_ref, bc_ref, dt_ref, da_ref, tri_ref, ex_ref, y_ref, st_ref):
    rows = slice(k * SSD_CHUNK, (k + 1) * SSD_CHUNK)
    tri = tri_ref[d]
    a = da_ref[0, 0, rows, :]
    a_parts = jnp.concatenate(_bf16_pieces(a, 3), axis=1)
    cs3 = jnp.dot(tri.astype(BF16), a_parts, preferred_element_type=F32)
    cs = cs3[:, 0:LANE] + cs3[:, LANE:2 * LANE] + cs3[:, 2 * LANE:3 * LANE]
    cs_t = cs.T
    mask = tri > 0.5
    stack = jnp.concatenate(_bf16_pieces(cs, 2) + _bf16_pieces(dt_ref[0, 0, rows, :], 2), axis=0)
    ex4 = jnp.dot(stack, ex_ref[...], preferred_element_type=F32)
    c = SSD_CHUNK
    cs_x = ex4[0:c] + ex4[c:2 * c]
    dt_x = ex4[2 * c:3 * c] + ex4[3 * c:4 * c]
    last = SSD_CHUNK - 1 if d == 0 else 0
    tot_x = cs_x[last:last + 1, :]
    e_cs = jnp.exp(cs_x)
    e_tot = jnp.exp(tot_x)
    xd = xs_ref[0, rows, :] * dt_x
    xd_dec = (xd * jnp.exp(tot_x - cs_x)).astype(BF16)
    xd = xd.astype(BF16)
    bc = bc_ref[0, rows, :]
    gn = SSD_GROUPS * SSD_STATE
    pw = 2 * SSD_HEADDIM
    low_half = lax.broadcasted_iota(jnp.int32, (1, pw), 1) < SSD_HEADDIM
    zero = jnp.zeros((SSD_CHUNK, pw), BF16)
    for g in range(SSD_GROUPS):
        bg_t = bc[:, g * SSD_STATE:(g + 1) * SSD_STATE].T.astype(BF16)
        cg = bc[:, gn + g * SSD_STATE:gn + (g + 1) * SSD_STATE].astype(BF16)
        cb = jnp.dot(cg, bg_t, preferred_element_type=F32)
        for pp in range(HEADS_PER_GROUP // 2):
            p = g * (HEADS_PER_GROUP // 2) + pp
            lanes = slice(p * pw, (p + 1) * pw)
            decay = []
            for h in (2 * p, 2 * p + 1):
                seg = cs[:, h:h + 1] - cs_t[h:h + 1, :]
                decay.append(cb * jnp.where(mask, jnp.exp(jnp.minimum(seg, 0.0)), 0.0))
            m = jnp.concatenate(decay, axis=1).astype(BF16)
            xp = xd[:, lanes]
            x_bd = jnp.concatenate([jnp.where(low_half, xp, zero), jnp.where(low_half, zero, xp)], axis=0)
            st = st_ref[d, p]
            y_diag = jnp.dot(m, x_bd, preferred_element_type=F32)
            y_off = jnp.dot(cg, st.astype(BF16), preferred_element_type=F32) * e_cs[:, lanes]
            y_ref[0, rows, lanes] = y_diag + y_off
            st_ref[d, p] = st * e_tot[:, lanes] + jnp.dot(bg_t, xd_dec[:, lanes], preferred_element_type=F32)


def _ssd_kernel(xsf_ref, bcf_ref, dtf_ref, daf_ref, xsb_ref, bcb_ref, dtb_ref, dab_ref, tri_ref, ex_ref, init_ref,
                yf_ref, yb_ref, fin_ref, st_ref, *, kc):
    s = pl.program_id(1)
    ns = pl.num_programs(1)

    @pl.when(s == 0)
    def _():
        st_ref[...] = init_ref[0]

    for k in range(kc):
        _ssd_chunk(0, k, xsf_ref, bcf_ref, dtf_ref, daf_ref, tri_ref, ex_ref, yf_ref, st_ref)
        _ssd_chunk(1, kc - 1 - k, xsb_ref, bcb_ref, dtb_ref, dab_ref, tri_ref, ex_ref, yb_ref, st_ref)

    @pl.when(s == ns - 1)
    def _():
        fin_ref[0] = st_ref[...]


def _ssd_call(xs, bc, dt, da, tri, ex, init):
    bsz, seq, _ = xs.shape
    kc = min(SSD_CHUNKS_PER_STEP, seq // SSD_CHUNK)
    rows = kc * SSD_CHUNK
    ns = seq // rows
    st_block = (1,) + SSD_STATE_SHAPE

    def tok(w, rev):
        return pl.BlockSpec((1, rows, w), lambda b, s: (b, ns - 1 - s if rev else s, 0))

    def per_dir(w, d):
        return pl.BlockSpec((1, 1, rows, w), lambda b, s: (b, d, ns - 1 - s if d else s, 0))

    return pl.pallas_call(
        functools.partial(_ssd_kernel, kc=kc),
        grid=(bsz, ns),
        in_specs=[
            tok(SSD_WIDTH, False), tok(BC_WIDTH, False), per_dir(LANE, 0), per_dir(LANE, 0),
            tok(SSD_WIDTH, True), tok(BC_WIDTH, True), per_dir(LANE, 1), per_dir(LANE, 1),
            pl.BlockSpec(tri.shape, lambda b, s: (0, 0, 0)),
            pl.BlockSpec(ex.shape, lambda b, s: (0, 0)),
            pl.BlockSpec(st_block, lambda b, s: (b, 0, 0, 0, 0)),
        ],
        out_specs=[tok(SSD_WIDTH, False), tok(SSD_WIDTH, True), pl.BlockSpec(st_block, lambda b, s: (b, 0, 0, 0, 0))],
        out_shape=[
            jax.ShapeDtypeStruct((bsz, seq, SSD_WIDTH), F32),
            jax.ShapeDtypeStruct((bsz, seq, SSD_WIDTH), F32),
            jax.ShapeDtypeStruct((bsz,) + SSD_STATE_SHAPE, F32),
        ],
        scratch_shapes=[pltpu.VMEM(SSD_STATE_SHAPE, F32)],
        compiler_params=_params("parallel", "arbitrary"),
        name="ssd",
    )(xs, bc, dt, da, xs, bc, dt, da, tri, ex, init)


def _dft1_kernel(p_ref, q_ref, m_ref, t_ref, *, n1):
    for j in range(DFT_NB):
        m = m_ref[j]
        rp = _bdot(m, p_ref[0, 0, :, j * FNET_WIDTH:(j + 1) * FNET_WIDTH])
        rq = _bdot(m, q_ref[0, 0, :, j * FNET_WIDTH:(j + 1) * FNET_WIDTH])
        t_ref[0, 0, j] = rp[:n1] + rq[n1:]
        t_ref[0, 1, j] = rp[n1:] - rq[:n1]


def _dft1_call(pq, mtab):
    bsz, _, seq, w = pq.shape
    n1 = seq // DFT_N2
    pqv = pq.reshape(bsz, 2, n1, DFT_N2 * w)
    cols = DFT_NB * w
    return pl.pallas_call(
        functools.partial(_dft1_kernel, n1=n1),
        grid=(bsz, DFT_N2 // DFT_NB),
        in_specs=[
            pl.BlockSpec((1, 1, n1, cols), lambda b, j: (b, 0, 0, j)),
            pl.BlockSpec((1, 1, n1, cols), lambda b, j: (b, 1, 0, j)),
            pl.BlockSpec((DFT_NB, 2 * n1, n1), lambda b, j: (j, 0, 0)),
        ],
        out_specs=pl.BlockSpec((1, 2, DFT_NB, n1, w), lambda b, j: (b, 0, j, 0, 0)),
        out_shape=jax.ShapeDtypeStruct((bsz, 2, DFT_N2, n1, w), F32),
        compiler_params=_params("parallel", "parallel"),
        name="dft1",
    )(pqv, pqv, mtab)


def _dft2_kernel(d_ref, t_ref, o_ref):
    o_ref[0] = _bdot(d_ref[...], t_ref[0])


def _dft2_call(dmat, t, *, tn):
    bsz, k, n = t.shape
    mo = dmat.shape[0]
    return pl.pallas_call(
        _dft2_kernel,
        grid=(bsz, n // tn),
        in_specs=[
            pl.BlockSpec((mo, k), lambda b, j: (0, 0)),
            pl.BlockSpec((1, k, tn), lambda b, j: (b, 0, j)),
        ],
        out_specs=pl.BlockSpec((1, mo, tn), lambda b, j: (b, 0, j)),
        out_shape=jax.ShapeDtypeStruct((bsz, mo, n), F32),
        compiler_params=_params("parallel", "parallel"),
        name="dft2",
    )(dmat, t)


def _mix_kernel(yf_ref, yb_ref, xs_ref, z_ref, yfn_ref, ypl_ref, x_ref, dsk_ref, nw_ref, wo_ref, g1_ref,
                lg_ref, lb_ref, sh2_ref, sc2_ref, x1_ref, h2_ref):
    y = yf_ref[0] + yb_ref[0] + xs_ref[0] * dsk_ref[...]
    z = z_ref[0]
    y = y * (z * jax.nn.sigmoid(z))
    gw = SSD_WIDTH // SSD_GROUPS
    parts = []
    for g in range(SSD_GROUPS):
        yg = y[:, g * gw:(g + 1) * gw]
        ms = jnp.mean(yg * yg, axis=-1, keepdims=True)
        parts.append(yg * lax.rsqrt(ms + EPS))
    yn = jnp.concatenate(parts, axis=-1) * nw_ref[...]
    o_f = SSD_WIDTH
    o_p = SSD_WIDTH + FNET_WIDTH
    mix = (_bdot(yn, wo_ref[0:o_f, :]) + _bdot(yfn_ref[0], wo_ref[o_f:o_p, :])
           + _bdot(ypl_ref[0], wo_ref[o_p:, :]))
    x1 = _ln(ALPHA * x_ref[0] + g1_ref[0] * mix) * lg_ref[...] + lb_ref[...]
    x1_ref[0] = x1
    h2_ref[0] = (_ln(x1) * (1.0 + sc2_ref[0]) + sh2_ref[0]).astype(BF16)


def _mix_call(yf, yb, xs, z, yfn, ypl, x, dsk, nw, wo, g1, lg, lb, sh2, sc2, *, tm):
    bsz, seq, d = x.shape
    mod_map = (lambda b, i: (b, 0, 0)) if g1.shape[0] > 1 else (lambda b, i: (0, 0, 0))

    def full(a):
        return pl.BlockSpec(a.shape, lambda b, i: (0,) * a.ndim)

    def tok(w):
        return pl.BlockSpec((1, tm, w), lambda b, i: (b, i, 0))

    mod = pl.BlockSpec((1, 1, d), mod_map)
    return pl.pallas_call(
        _mix_kernel,
        grid=(bsz, seq // tm),
        in_specs=[
            tok(SSD_WIDTH), tok(SSD_WIDTH), tok(SSD_WIDTH), tok(SSD_WIDTH), tok(FNET_WIDTH), tok(POOL_WIDTH), tok(d),
            full(dsk), full(nw), full(wo), mod, full(lg), full(lb), mod, mod,
        ],
        out_specs=[tok(d), tok(d)],
        out_shape=[jax.ShapeDtypeStruct((bsz, seq, d), F32), jax.ShapeDtypeStruct((bsz, seq, d), BF16)],
        compiler_params=_params("parallel", "parallel"),
        name="mix",
    )(yf, yb, xs, z, yfn, ypl, x, dsk, nw, wo, g1, lg, lb, sh2, sc2)


def _ffn_kernel(h_ref, hp_ref, hn_ref, wv_ref, wg_ref, cwv_ref, cwg_ref, cbv_ref, cbg_ref, wd_ref, x1_ref,
                g2_ref, lg_ref, lb_ref, o_ref, hbuf, av0, ag0, av1, ag1, acc, *, tm, on_grid):
    i = pl.program_id(1)
    j = pl.program_id(2)
    nt = pl.num_programs(1)
    nj = pl.num_programs(2) - 1
    ext = tm + 2 * FFN_HALO
    ra = ext // FFN_ROW_CHUNKS
    rb = tm // FFN_ROW_CHUNKS
    slots = ((av0, ag0), (av1, ag1))
    base = FFN_PAD + FFN_HALO
    rows = (-1, 0, 1) if on_grid else (0,)

    def up_chunk(slot, r):
        av, ag = slot
        hb = hbuf[r * ra:(r + 1) * ra, :]
        av[FFN_PAD + r * ra:FFN_PAD + (r + 1) * ra, :] = jnp.dot(hb, wv_ref[...], preferred_element_type=F32)
        ag[FFN_PAD + r * ra:FFN_PAD + (r + 1) * ra, :] = jnp.dot(hb, wg_ref[...], preferred_element_type=F32)

    def conv(a_ref, cw_ref, cb_ref, r):
        cw = cw_ref[...]
        col = (lax.broadcasted_iota(jnp.int32, (rb, 1), 0) + r * rb) & (GRID_W - 1)
        side = []
        for dc in (-1, 0, 1):
            acc_dc = None
            for dr in rows:
                k = (dr + 1) * 3 + dc + 1
                term = a_ref[pl.ds(base + r * rb + dr * GRID_W + dc, rb), :] * cw[k:k + 1]
                acc_dc = term if acc_dc is None else acc_dc + term
            side.append(acc_dc)
        left, mid, right = side
        if on_grid:
            left = jnp.where(col != 0, left, 0.0)
            right = jnp.where(col != GRID_W - 1, right, 0.0)
        return left + mid + right + cb_ref[...]

    def down_chunk(slot, r):
        av, ag = slot
        val = conv(av, cwv_ref, cbv_ref, r)
        gate = conv(ag, cwg_ref, cbg_ref, r)
        hidden = (jax.nn.gelu(gate, approximate=True) * val).astype(BF16)
        acc[r * rb:(r + 1) * rb, :] += jnp.dot(hidden, wd_ref[...], preferred_element_type=F32)

    @pl.when(j == 0)
    def _():
        hbuf[0:FFN_HALO] = jnp.where(i > 0, hp_ref[0], jnp.zeros_like(hp_ref[0]))
        hbuf[FFN_HALO:FFN_HALO + tm] = h_ref[0]
        hbuf[FFN_HALO + tm:] = jnp.where(i < nt - 1, hn_ref[0], jnp.zeros_like(hn_ref[0]))
        acc[...] = jnp.zeros_like(acc)
        for a in (av0, ag0, av1, ag1):
            a[0:FFN_PAD] = jnp.zeros((FFN_PAD, FFN_CB), F32)
            a[FFN_PAD + ext:] = jnp.zeros((FFN_PAD, FFN_CB), F32)
        for r in range(FFN_ROW_CHUNKS):
            up_chunk(slots[0], r)

    for parity in (0, 1):
        @pl.when((j > 0) & (j < nj) & (j % 2 == parity))
        def _():
            for r in range(FFN_ROW_CHUNKS):
                up_chunk(slots[parity], r)
                down_chunk(slots[1 - parity], r)

    @pl.when(j == nj)
    def _():
        for r in range(FFN_ROW_CHUNKS):
            down_chunk(slots[(D_FF // FFN_CB - 1) % 2], r)
        o_ref[0] = _ln(ALPHA * x1_ref[0] + g2_ref[0] * acc[...]) * lg_ref[...] + lb_ref[...]


def _ffn_call(h, w_up, conv_w9, conv_b, w_down, x1, g2, lg, lb, *, tm, on_grid):
    bsz, seq, d = h.shape
    nt = seq // tm
    nj = D_FF // FFN_CB
    th = tm // FFN_HALO
    nh = seq // FFN_HALO
    ext = tm + 2 * FFN_HALO
    mod_map = (lambda b, i, j: (b, 0, 0)) if g2.shape[0] > 1 else (lambda b, i, j: (0, 0, 0))

    def full(a):
        return pl.BlockSpec(a.shape, lambda b, i, j: (0,) * a.ndim)

    def up_blk(j):
        return jnp.minimum(j, nj - 1)

    def dn_blk(j):
        return jnp.maximum(j - 1, 0)

    a_scratch = pltpu.VMEM((ext + 2 * FFN_PAD, FFN_CB), F32)
    return pl.pallas_call(
        functools.partial(_ffn_kernel, tm=tm, on_grid=on_grid),
        grid=(bsz, nt, nj + 1),
        in_specs=[
            pl.BlockSpec((1, tm, d), lambda b, i, j: (b, i, 0)),
            pl.BlockSpec((1, FFN_HALO, d), lambda b, i, j: (b, jnp.maximum(i * th - 1, 0), 0)),
            pl.BlockSpec((1, FFN_HALO, d), lambda b, i, j: (b, jnp.minimum((i + 1) * th, nh - 1), 0)),
            pl.BlockSpec((d, FFN_CB), lambda b, i, j: (0, up_blk(j))),
            pl.BlockSpec((d, FFN_CB), lambda b, i, j: (0, nj + up_blk(j))),
            pl.BlockSpec((9, FFN_CB), lambda b, i, j: (0, dn_blk(j))),
            pl.BlockSpec((9, FFN_CB), lambda b, i, j: (0, nj + dn_blk(j))),
            pl.BlockSpec((1, FFN_CB), lambda b, i, j: (0, dn_blk(j))),
            pl.BlockSpec((1, FFN_CB), lambda b, i, j: (0, nj + dn_blk(j))),
            pl.BlockSpec((FFN_CB, d), lambda b, i, j: (dn_blk(j), 0)),
            pl.BlockSpec((1, tm, d), lambda b, i, j: (b, i, 0)),
            pl.BlockSpec((1, 1, d), mod_map),
            full(lg), full(lb),
        ],
        out_specs=pl.BlockSpec((1, tm, d), lambda b, i, j: (b, i, 0)),
        out_shape=jax.ShapeDtypeStruct((bsz, seq, d), F32),
        scratch_shapes=[pltpu.VMEM((ext, d), BF16), a_scratch, a_scratch, a_scratch, a_scratch,
                        pltpu.VMEM((tm, d), F32)],
        compiler_params=_params("parallel", "parallel", "arbitrary"),
        name="ffn",
    )(h, h, h, w_up, w_up, conv_w9, conv_w9, conv_b, conv_b, w_down, x1, g2, lg, lb)


def _block_diag(blocks):
    g, n, m = blocks.shape
    out = jnp.zeros((g * n, g * m), blocks.dtype)
    for k in range(g):
        out = out.at[k * n:(k + 1) * n, k * m:(k + 1) * m].set(blocks[k])
    return out


def _dft_tables(seq):
    n1 = seq // DFT_N2
    k1 = np.arange(n1, dtype=np.int64)[:, None]
    nn1 = np.arange(n1, dtype=np.int64)[None, :]
    tabs = []
    for n2 in range(DFT_N2):
        th = 2.0 * np.pi * ((k1 * (DFT_N2 * nn1 + n2)) % seq) / seq
        tabs.append(np.concatenate([np.cos(th), -np.sin(th)], axis=0))
    mtab = np.stack(tabs)
    k2 = np.arange(DFT_N2, dtype=np.int64)[:, None]
    nn2 = np.arange(DFT_N2, dtype=np.int64)[None, :]
    th2 = 2.0 * np.pi * ((k2 * nn2) % DFT_N2) / DFT_N2
    scale = 1.0 / math.sqrt(seq * FNET_GDIM)
    dmat = scale * np.concatenate([np.cos(th2), np.sin(th2)], axis=1)
    return jnp.asarray(mtab, F32), jnp.asarray(dmat, F32)


def _dense_dft_table(seq):
    k = np.arange(seq, dtype=np.int64)
    th = 2.0 * np.pi * ((k[:, None] * k[None, :]) % seq) / seq
    scale = 1.0 / math.sqrt(seq * FNET_GDIM)
    return jnp.asarray(scale * np.concatenate([np.cos(th), -np.sin(th)], axis=1), F32)


def _channel_dft_tables():
    k = np.arange(FNET_GDIM, dtype=np.int64)
    th = 2.0 * np.pi * ((k[:, None] * k[None, :]) % FNET_GDIM) / FNET_GDIM
    eye = np.eye(FNET_GROUPS)
    return jnp.asarray(np.kron(eye, np.cos(th)), F32), jnp.asarray(np.kron(eye, np.sin(th)), F32)


def _scan_matrices():
    r = np.arange(SSD_CHUNK)
    lower = (r[:, None] >= r[None, :]).astype(np.float32)
    return jnp.asarray(np.stack([lower, lower.T]), F32)


def _head_expansion():
    ex = np.zeros((LANE, SSD_WIDTH), np.float32)
    for h in range(SSD_HEADS):
        ex[h, h * SSD_HEADDIM:(h + 1) * SSD_HEADDIM] = 1.0
    return jnp.asarray(ex, BF16)


def _token_mixer(x, sh1, sc1, lw, init, tri, ex, cc, cs, *, tm):
    bsz, seq, _ = x.shape
    z, xs, bc, dt, da, pq, ypl = _inproj_call(
        x, sh1, sc1, lw["w_cat"], lw["ssd_conv_w"], lw["ssd_conv_b"], lw["dt_bias"], lw["a_log"], cc, cs,
        lw["fnet_bd"], lw["pool_bd"], lw["pool_scale"], tm=tm)
    yf, yb, fin = _ssd_call(xs, bc, dt, da, tri, ex, init)
    return z, xs, yf, yb, pq, ypl, fin


def _fourier_positions(pq):
    bsz, _, seq, w = pq.shape
    if seq % (DFT_N2 * 8) == 0:
        n1 = seq // DFT_N2
        mtab, dmat = _dft_tables(seq)
        t = _dft1_call(pq, mtab).reshape(bsz, 2 * DFT_N2, n1 * w)
        out = _dft2_call(dmat, t, tn=min(4096, n1 * w))
        return out.reshape(bsz, seq, w)
    return _dft2_call(_dense_dft_table(seq), pq.reshape(bsz, 2 * seq, w), tn=w)


def kernel(x, c, ctx, c_ctx, w_ada, b_ada, w_in, ssd_conv_w, ssd_conv_b, ssd_dt_bias, ssd_a_log, ssd_d,
           ssd_norm_w, fnet_w, pool_w, pool_scale, w_out, ln1_g, ln1_b, ffn_w_up, ffn_conv_w, ffn_conv_b,
           ffn_w_down, ln2_g, ln2_b):
    bsz, seq, d = x.shape
    n_ctx = ctx.shape[1]
    assert d == D_MODEL and bsz <= 7 and seq % (GRID_W * 8) == 0 and n_ctx % SSD_CHUNK == 0
    tm_lat = min(512, seq)
    tm_ffn = min(1024, seq)

    cv = jnp.zeros((8, d), F32).at[0:bsz].set(c).at[bsz].set(c_ctx)
    mods = _ada_call(cv, w_ada, b_ada)
    tri = _scan_matrices()
    ex = _head_expansion()
    cc, cs = _channel_dft_tables()
    zero_state = jnp.zeros((bsz,) + SSD_STATE_SHAPE, F32)
    lane_pad = jnp.zeros((d, LANE - SSD_HEADS), F32)

    for l in range(DEPTH):
        last = l == DEPTH - 1
        m = mods[l].reshape(8, 6, d)
        lat = [m[0:bsz, k][:, None, :] for k in range(6)]
        cx = [m[bsz:bsz + 1, k][:, None, :] for k in range(6)]
        w = w_in[l]
        w_cat = jnp.concatenate(
            [w[:, 0:OFF_DT], w[:, OFF_FNET:N_IN], w[:, OFF_DT:OFF_DT + SSD_HEADS], lane_pad,
             w[:, OFF_DT + SSD_HEADS:OFF_FNET], lane_pad], axis=1).astype(BF16)
        pad16 = ((0, 0), (0, LANE - SSD_HEADS))
        lw = {
            "w_cat": w_cat,
            "ssd_conv_w": ssd_conv_w[l],
            "ssd_conv_b": ssd_conv_b[l][None, :],
            "dt_bias": jnp.pad(ssd_dt_bias[l], pad16),
            "a_log": jnp.pad(ssd_a_log[l], pad16),
            "fnet_bd": _block_diag(fnet_w[l]),
            "pool_bd": _block_diag(pool_w[l]).astype(BF16),
            "pool_scale": pool_scale[l][None, :],
        }
        dsk = jnp.repeat(ssd_d[l], SSD_HEADDIM)[None, :]
        nw = ssd_norm_w[l][None, :]
        wo = w_out[l].astype(BF16)
        lg1, lb1 = ln1_g[l][None, :], ln1_b[l][None, :]
        lg2, lb2 = ln2_g[l][None, :], ln2_b[l][None, :]
        w_up = ffn_w_up[l].astype(BF16)
        w_down = ffn_w_down[l].astype(BF16)
        conv9 = ffn_conv_w[l].reshape(9, 2 * D_FF)
        conv_b = ffn_conv_b[l][None, :]

        cz, cxs, cyf, cyb, cpq, cypl, c_fin = _token_mixer(ctx, cx[0], cx[1], lw, zero_state, tri, ex, cc, cs,
                                                           tm=n_ctx)
        z, xs, yf, yb, pq, ypl, _ = _token_mixer(x, lat[0], lat[1], lw, c_fin, tri, ex, cc, cs, tm=tm_lat)

        yfn = _fourier_positions(pq)
        x1, h2 = _mix_call(yf, yb, xs, z, yfn, ypl, x, dsk, nw, wo, lat[2], lg1, lb1, lat[3], lat[4], tm=tm_lat)
        x = _ffn_call(h2, w_up, conv9, conv_b, w_down, x1, lat[5], lg2, lb2, tm=tm_ffn, on_grid=True)

        if not last:
            cyfn = _fourier_positions(cpq)
            c1, ch2 = _mix_call(cyf, cyb, cxs, cz, cyfn, cypl, ctx, dsk, nw, wo, cx[2], lg1, lb1, cx[3], cx[4],
                                tm=n_ctx)
            ctx = _ffn_call(ch2, w_up, conv9, conv_b, w_down, c1, cx[5], lg2, lb2, tm=n_ctx, on_grid=False)
    return x
```

```python
import functools
import math

import numpy as np
import jax
import jax.numpy as jnp
from jax import lax
from jax.experimental import pallas as pl
from jax.experimental.pallas import tpu as pltpu

F32 = jnp.float32
BF16 = jnp.bfloat16
HIGHEST = lax.Precision.HIGHEST

D_MODEL = 1024
DEPTH = 2
GRID_W = 64
SSD_WIDTH = 512
SSD_HEADDIM = 64
SSD_HEADS = 8
SSD_GROUPS = 2
HEADS_PER_GROUP = SSD_HEADS // SSD_GROUPS
SSD_STATE = 128
SSD_CHUNK = 128
SSD_CHUNKS_PER_STEP = 4
XBC_WIDTH = SSD_WIDTH + 2 * SSD_GROUPS * SSD_STATE
BC_WIDTH = 2 * SSD_GROUPS * SSD_STATE
SSD_STATE_SHAPE = (2, SSD_HEADS // 2, SSD_STATE, 2 * SSD_HEADDIM)
FNET_WIDTH = 256
FNET_GROUPS = 4
FNET_GDIM = 64
POOL_WINDOWS = (2, 4, 8, 16)
POOL_WIDTH = 256
POOL_GDIM = 64
POOL_HALO = 8
D_FF = 2816
ALPHA = (2.0 * DEPTH) ** 0.25
EPS = 1e-6

OFF_XBC = SSD_WIDTH
OFF_DT = OFF_XBC + XBC_WIDTH
OFF_FNET = OFF_DT + 2 * SSD_HEADS
OFF_POOL = OFF_FNET + FNET_WIDTH
N_IN = OFF_POOL + POOL_WIDTH

LANE = 128
C_Z = 0
C_XBC = C_Z + SSD_WIDTH
C_FNET = C_XBC + XBC_WIDTH
C_POOL = C_FNET + FNET_WIDTH
C_DT = C_POOL + POOL_WIDTH
N_CAT = C_DT + 2 * LANE

DFT_N2 = 64
DFT_NB = 8
FFN_CB = 256
FFN_HALO = GRID_W
FFN_PAD = 8
FFN_ROW_CHUNKS = 4
VMEM_LIMIT = 56 * 1024 * 1024


def _ln(x):
    mu = jnp.mean(x, axis=-1, keepdims=True)
    xc = x - mu
    var = jnp.mean(xc * xc, axis=-1, keepdims=True)
    return xc * lax.rsqrt(var + EPS)


def _bdot(a, b):
    return jnp.dot(a.astype(BF16), b.astype(BF16), preferred_element_type=F32)


def _params(*sem):
    return pltpu.CompilerParams(dimension_semantics=sem, vmem_limit_bytes=VMEM_LIMIT)


def _ada_kernel(c_ref, w_ref, b_ref, o_ref):
    cv = c_ref[...]
    s = cv * jax.nn.sigmoid(cv)
    o_ref[0] = jnp.dot(s, w_ref[0], preferred_element_type=F32, precision=HIGHEST) + b_ref[0]


def _ada_call(cv, w_ada, b_ada):
    depth, d, n = w_ada.shape
    tn = 1536
    return pl.pallas_call(
        _ada_kernel,
        grid=(depth, n // tn),
        in_specs=[
            pl.BlockSpec((8, d), lambda l, j: (0, 0)),
            pl.BlockSpec((1, d, tn), lambda l, j: (l, 0, j)),
            pl.BlockSpec((1, 1, tn), lambda l, j: (l, 0, j)),
        ],
        out_specs=pl.BlockSpec((1, 8, tn), lambda l, j: (l, 0, j)),
        out_shape=jax.ShapeDtypeStruct((depth, 8, n), F32),
        compiler_params=_params("parallel", "parallel"),
        name="ada",
    )(cv, w_ada, b_ada.reshape(depth, 1, n))


def _inproj_kernel(x_ref, xp_ref, xn_ref, sh_ref, sc_ref, w_ref, cw_ref, cb_ref, dtb_ref, alog_ref,
                   cc_ref, cs_ref, fw_ref, pw_ref, ps_ref,
                   z_ref, xs_ref, bc_ref, dt_ref, da_ref, pq_ref, yp_ref, *, tm, seq_len):
    i = pl.program_id(1)
    nt = pl.num_programs(1)
    sh = sh_ref[0]
    sc = sc_ref[0]

    def modulated(xv):
        return (_ln(xv) * (1.0 + sc) + sh).astype(BF16)

    u = jnp.dot(modulated(x_ref[0]), w_ref[...], preferred_element_type=F32)
    w_halo = w_ref[:, C_XBC:C_DT]
    up = jnp.dot(modulated(xp_ref[0]), w_halo, preferred_element_type=F32)
    un = jnp.dot(modulated(xn_ref[0]), w_halo, preferred_element_type=F32)
    up = jnp.where(i > 0, up, 0.0)
    un = jnp.where(i < nt - 1, un, 0.0)

    z_ref[0] = u[:, C_Z:C_XBC].astype(BF16)

    xbc = u[:, C_XBC:C_FNET]
    row = lax.broadcasted_iota(jnp.int32, (tm, 1), 0)
    prev_row = up[POOL_HALO - 1:POOL_HALO, 0:XBC_WIDTH]
    next_row = un[0:1, 0:XBC_WIDTH]
    xm1 = jnp.where(row == 0, prev_row, pltpu.roll(xbc, 1, 0))
    xp1 = jnp.where(row == tm - 1, next_row, pltpu.roll(xbc, tm - 1, 0))
    cw = cw_ref[...]
    conv = xm1 * cw[0:1] + xbc * cw[1:2] + xp1 * cw[2:3] + cb_ref[...]
    act = conv * jax.nn.sigmoid(conv)
    xs_ref[0] = act[:, 0:SSD_WIDTH].astype(BF16)
    bc_ref[0] = act[:, SSD_WIDTH:XBC_WIDTH].astype(BF16)

    lane = lax.broadcasted_iota(jnp.int32, (1, LANE), 1)
    for d in range(2):
        raw = u[:, C_DT + d * LANE:C_DT + (d + 1) * LANE] + dtb_ref[d:d + 1, :]
        sp = jnp.maximum(raw, 0.0) + jnp.log1p(jnp.exp(-jnp.abs(raw)))
        dt = jnp.where(lane < SSD_HEADS, sp, 0.0)
        dt_ref[0, d] = dt
        da_ref[0, d] = dt * (-jnp.exp(alog_ref[d:d + 1, :]))

    uf = u[:, C_FNET:C_POOL]
    fw = fw_ref[...]
    cw_mix = jnp.dot(cc_ref[...], fw, preferred_element_type=F32, precision=HIGHEST)
    sw_mix = jnp.dot(cs_ref[...], fw, preferred_element_type=F32, precision=HIGHEST)
    pq_ref[0, 0] = _bdot(uf, cw_mix)
    pq_ref[0, 1] = _bdot(uf, sw_mix)

    off_pool = C_POOL - C_XBC
    ext = jnp.concatenate([up[:, off_pool:off_pool + POOL_WIDTH], u[:, C_POOL:C_DT],
                           un[:, off_pool:off_pool + POOL_WIDTH]], axis=0)
    n_ext = tm + 2 * POOL_HALO
    sums = {}
    cur = ext
    width = 1
    while width < POOL_WINDOWS[-1]:
        cur = cur + pltpu.roll(cur, n_ext - width, 0)
        width *= 2
        sums[width] = cur
    lane_p = lax.broadcasted_iota(jnp.int32, (1, POOL_WIDTH), 1)
    pos = i * tm + lax.broadcasted_iota(jnp.int32, (tm, 1), 0)
    wsum = None
    cnt = None
    for gi, win in enumerate(POOL_WINDOWS):
        left = win // 2
        start = POOL_HALO - left
        s_w = sums[win]
        if start:
            s_w = pltpu.roll(s_w, n_ext - start, 0)
        s_w = s_w[0:tm]
        c_w = jnp.minimum(pos + (win - left), seq_len) - jnp.maximum(pos - left, 0)
        if wsum is None:
            wsum, cnt = s_w, c_w
        else:
            sel = lane_p >= gi * POOL_GDIM
            wsum = jnp.where(sel, s_w, wsum)
            cnt = jnp.where(sel, c_w, cnt)
    pooled = wsum / cnt.astype(F32) - u[:, C_POOL:C_DT]
    yp_ref[0] = (_bdot(pooled, pw_ref[...]) * ps_ref[...]).astype(BF16)


def _inproj_call(x, sh, sc, w_cat, conv_w, conv_b, dt_bias, a_log, cc, cs, fw, pw, ps, *, tm):
    bsz, seq, d = x.shape
    nt = seq // tm
    nb8 = seq // 8
    tb = tm // 8
    mod_map = (lambda b, i: (b, 0, 0)) if sh.shape[0] > 1 else (lambda b, i: (0, 0, 0))

    def full(a):
        return pl.BlockSpec(a.shape, lambda b, i: (0,) * a.ndim)

    in_specs = [
        pl.BlockSpec((1, tm, d), lambda b, i: (b, i, 0)),
        pl.BlockSpec((1, 8, d), lambda b, i: (b, jnp.maximum(i * tb - 1, 0), 0)),
        pl.BlockSpec((1, 8, d), lambda b, i: (b, jnp.minimum((i + 1) * tb, nb8 - 1), 0)),
        pl.BlockSpec((1, 1, d), mod_map),
        pl.BlockSpec((1, 1, d), mod_map),
    ] + [full(a) for a in (w_cat, conv_w, conv_b, dt_bias, a_log, cc, cs, fw, pw, ps)]
    out_shape = [
        jax.ShapeDtypeStruct((bsz, seq, SSD_WIDTH), BF16),
        jax.ShapeDtypeStruct((bsz, seq, SSD_WIDTH), BF16),
        jax.ShapeDtypeStruct((bsz, seq, BC_WIDTH), BF16),
        jax.ShapeDtypeStruct((bsz, 2, seq, LANE), F32),
        jax.ShapeDtypeStruct((bsz, 2, seq, LANE), F32),
        jax.ShapeDtypeStruct((bsz, 2, seq, FNET_WIDTH), F32),
        jax.ShapeDtypeStruct((bsz, seq, POOL_WIDTH), BF16),
    ]
    out_specs = [
        pl.BlockSpec((1, tm, SSD_WIDTH), lambda b, i: (b, i, 0)),
        pl.BlockSpec((1, tm, SSD_WIDTH), lambda b, i: (b, i, 0)),
        pl.BlockSpec((1, tm, BC_WIDTH), lambda b, i: (b, i, 0)),
        pl.BlockSpec((1, 2, tm, LANE), lambda b, i: (b, 0, i, 0)),
        pl.BlockSpec((1, 2, tm, LANE), lambda b, i: (b, 0, i, 0)),
        pl.BlockSpec((1, 2, tm, FNET_WIDTH), lambda b, i: (b, 0, i, 0)),
        pl.BlockSpec((1, tm, POOL_WIDTH), lambda b, i: (b, i, 0)),
    ]
    return pl.pallas_call(
        functools.partial(_inproj_kernel, tm=tm, seq_len=seq),
        grid=(bsz, nt),
        in_specs=in_specs,
        out_specs=out_specs,
        out_shape=out_shape,
        compiler_params=_params("parallel", "parallel"),
        name="inproj",
    )(x, x, x, sh, sc, w_cat, conv_w, conv_b, dt_bias, a_log, cc, cs, fw, pw, ps)


def _bf16_pieces(x, terms):
    pieces = []
    rem = x
    for t in range(terms):
        piece = rem.astype(BF16)
        pieces.append(piece)
        if t + 1 < terms:
            rem = rem - piece.astype(F32)
    return pieces


def _ssd_chunk(d, k, xs_ref, bc_ref, dt_ref, da_ref, tri_ref, ex_ref, y_ref, st_ref):
    rows = slice(k * SSD_CHUNK, (k + 1) * SSD_CHUNK)
    tri = tri_ref[d]
    a = da_ref[0, 0, rows, :]
    a_parts = jnp.concatenate(_bf16_pieces(a, 3), axis=1)
    cs3 = jnp.dot(tri.astype(BF16), a_parts, preferred_element_type=F32)
    cs = cs3[:, 0:LANE] + cs3[:, LANE:2 * LANE] + cs3[:, 2 * LANE:3 * LANE]
    cs_t = cs.T
    mask = tri > 0.5
    stack = jnp.concatenate(_bf16_pieces(cs, 2) + _bf16_pieces(dt_ref[0, 0, rows, :], 2), axis=0)
    ex4 = jnp.dot(stack, ex_ref[...], preferred_element_type=F32)
    c = SSD_CHUNK
    cs_x = ex4[0:c] + ex4[c:2 * c]
    dt_x = ex4[2 * c:3 * c] + ex4[3 * c:4 * c]
    last = SSD_CHUNK - 1 if d == 0 else 0
    tot_x = cs_x[last:last + 1, :]
    e_cs = jnp.exp(cs_x)
    e_tot = jnp.exp(tot_x)
    xd = xs_ref[0, rows, :].astype(F32) * dt_x
    xd_dec = (xd * jnp.exp(tot_x - cs_x)).astype(BF16)
    xd = xd.astype(BF16)
    bc = bc_ref[0, rows, :]
    gn = SSD_GROUPS * SSD_STATE
    pw = 2 * SSD_HEADDIM
    low_half = lax.broadcasted_iota(jnp.int32, (1, pw), 1) < SSD_HEADDIM
    zero = jnp.zeros((SSD_CHUNK, pw), BF16)
    for g in range(SSD_GROUPS):
        bg_t = bc[:, g * SSD_STATE:(g + 1) * SSD_STATE].astype(F32).T.astype(BF16)
        cg = bc[:, gn + g * SSD_STATE:gn + (g + 1) * SSD_STATE]
        cb = jnp.dot(cg, bg_t, preferred_element_type=F32)
        for pp in range(HEADS_PER_GROUP // 2):
            p = g * (HEADS_PER_GROUP // 2) + pp
            lanes = slice(p * pw, (p + 1) * pw)
            decay = []
            for h in (2 * p, 2 * p + 1):
                seg = cs[:, h:h + 1] - cs_t[h:h + 1, :]
                decay.append(cb * jnp.where(mask, jnp.exp(jnp.minimum(seg, 0.0)), 0.0))
            m = jnp.concatenate(decay, axis=1).astype(BF16)
            xp = xd[:, lanes]
            x_bd = jnp.concatenate([jnp.where(low_half, xp, zero), jnp.where(low_half, zero, xp)], axis=0)
            st = st_ref[d, p]
            y_diag = jnp.dot(m, x_bd, preferred_element_type=F32)
            y_off = jnp.dot(cg, st.astype(BF16), preferred_element_type=F32) * e_cs[:, lanes]
            y_ref[0, rows, lanes] = (y_diag + y_off).astype(BF16)
            st_ref[d, p] = st * e_tot[:, lanes] + jnp.dot(bg_t, xd_dec[:, lanes], preferred_element_type=F32)


def _ssd_kernel(xsf_ref, bcf_ref, dtf_ref, daf_ref, xsb_ref, bcb_ref, dtb_ref, dab_ref, tri_ref, ex_ref, init_ref,
                yf_ref, yb_ref, fin_ref, st_ref, *, kc):
    s = pl.program_id(1)
    ns = pl.num_programs(1)

    @pl.when(s == 0)
    def _():
        st_ref[...] = init_ref[0]

    for k in range(kc):
        _ssd_chunk(0, k, xsf_ref, bcf_ref, dtf_ref, daf_ref, tri_ref, ex_ref, yf_ref, st_ref)
        _ssd_chunk(1, kc - 1 - k, xsb_ref, bcb_ref, dtb_ref, dab_ref, tri_ref, ex_ref, yb_ref, st_ref)

    @pl.when(s == ns - 1)
    def _():
        fin_ref[0] = st_ref[...]


def _ssd_call(xs, bc, dt, da, tri, ex, init):
    bsz, seq, _ = xs.shape
    kc = min(SSD_CHUNKS_PER_STEP, seq // SSD_CHUNK)
    rows = kc * SSD_CHUNK
    ns = seq // rows
    st_block = (1,) + SSD_STATE_SHAPE

    def tok(w, rev):
        return pl.BlockSpec((1, rows, w), lambda b, s: (b, ns - 1 - s if rev else s, 0))

    def per_dir(w, d):
        return pl.BlockSpec((1, 1, rows, w), lambda b, s: (b, d, ns - 1 - s if d else s, 0))

    return pl.pallas_call(
        functools.partial(_ssd_kernel, kc=kc),
        grid=(bsz, ns),
        in_specs=[
            tok(SSD_WIDTH, False), tok(BC_WIDTH, False), per_dir(LANE, 0), per_dir(LANE, 0),
            tok(SSD_WIDTH, True), tok(BC_WIDTH, True), per_dir(LANE, 1), per_dir(LANE, 1),
            pl.BlockSpec(tri.shape, lambda b, s: (0, 0, 0)),
            pl.BlockSpec(ex.shape, lambda b, s: (0, 0)),
            pl.BlockSpec(st_block, lambda b, s: (b, 0, 0, 0, 0)),
        ],
        out_specs=[tok(SSD_WIDTH, False), tok(SSD_WIDTH, True), pl.BlockSpec(st_block, lambda b, s: (b, 0, 0, 0, 0))],
        out_shape=[
            jax.ShapeDtypeStruct((bsz, seq, SSD_WIDTH), BF16),
            jax.ShapeDtypeStruct((bsz, seq, SSD_WIDTH), BF16),
            jax.ShapeDtypeStruct((bsz,) + SSD_STATE_SHAPE, F32),
        ],
        scratch_shapes=[pltpu.VMEM(SSD_STATE_SHAPE, F32)],
        compiler_params=_params("parallel", "arbitrary"),
        name="ssd",
    )(xs, bc, dt, da, xs, bc, dt, da, tri, ex, init)


def _dft1_kernel(p_ref, q_ref, m_ref, t_ref, *, n1):
    for j in range(DFT_NB):
        m = m_ref[j]
        rp = _bdot(m, p_ref[0, 0, :, j, :])
        rq = _bdot(m, q_ref[0, 0, :, j, :])
        t_ref[0, 0, j] = (rp[:n1] + rq[n1:]).astype(BF16)
        t_ref[0, 1, j] = (rp[n1:] - rq[:n1]).astype(BF16)


def _dft1_call(pq, mtab):
    bsz, _, seq, w = pq.shape
    n1 = seq // DFT_N2
    pqv = pq.reshape(bsz, 2, n1, DFT_N2, w)
    return pl.pallas_call(
        functools.partial(_dft1_kernel, n1=n1),
        grid=(bsz, DFT_N2 // DFT_NB),
        in_specs=[
            pl.BlockSpec((1, 1, n1, DFT_NB, w), lambda b, j: (b, 0, 0, j, 0)),
            pl.BlockSpec((1, 1, n1, DFT_NB, w), lambda b, j: (b, 1, 0, j, 0)),
            pl.BlockSpec((DFT_NB, 2 * n1, n1), lambda b, j: (j, 0, 0)),
        ],
        out_specs=pl.BlockSpec((1, 2, DFT_NB, n1, w), lambda b, j: (b, 0, j, 0, 0)),
        out_shape=jax.ShapeDtypeStruct((bsz, 2, DFT_N2, n1, w), BF16),
        compiler_params=_params("parallel", "parallel"),
        name="dft1",
    )(pqv, pqv, mtab)


def _dft2_kernel(d_ref, t_ref, o_ref):
    o_ref[0] = _bdot(d_ref[...], t_ref[0]).astype(BF16)


def _dft2_call(dmat, t, *, tn):
    bsz, k, n = t.shape
    mo = dmat.shape[0]
    return pl.pallas_call(
        _dft2_kernel,
        grid=(bsz, n // tn),
        in_specs=[
            pl.BlockSpec((mo, k), lambda b, j: (0, 0)),
            pl.BlockSpec((1, k, tn), lambda b, j: (b, 0, j)),
        ],
        out_specs=pl.BlockSpec((1, mo, tn), lambda b, j: (b, 0, j)),
        out_shape=jax.ShapeDtypeStruct((bsz, mo, n), BF16),
        compiler_params=_params("parallel", "parallel"),
        name="dft2",
    )(dmat, t)


def _mix_kernel(yf_ref, yb_ref, xs_ref, z_ref, yfn_ref, ypl_ref, x_ref, dsk_ref, nw_ref, wo_ref, g1_ref,
                lg_ref, lb_ref, sh2_ref, sc2_ref, x1_ref, h2_ref):
    y = yf_ref[0].astype(F32) + yb_ref[0].astype(F32) + xs_ref[0].astype(F32) * dsk_ref[...]
    z = z_ref[0].astype(F32)
    y = y * (z * jax.nn.sigmoid(z))
    gw = SSD_WIDTH // SSD_GROUPS
    parts = []
    for g in range(SSD_GROUPS):
        yg = y[:, g * gw:(g + 1) * gw]
        ms = jnp.mean(yg * yg, axis=-1, keepdims=True)
        parts.append(yg * lax.rsqrt(ms + EPS))
    yn = jnp.concatenate(parts, axis=-1) * nw_ref[...]
    o_f = SSD_WIDTH
    o_p = SSD_WIDTH + FNET_WIDTH
    mix = (_bdot(yn, wo_ref[0:o_f, :]) + _bdot(yfn_ref[0], wo_ref[o_f:o_p, :])
           + _bdot(ypl_ref[0], wo_ref[o_p:, :]))
    x1 = _ln(ALPHA * x_ref[0] + g1_ref[0] * mix) * lg_ref[...] + lb_ref[...]
    x1_ref[0] = x1
    h2_ref[0] = (_ln(x1) * (1.0 + sc2_ref[0]) + sh2_ref[0]).astype(BF16)


def _mix_call(yf, yb, xs, z, yfn, ypl, x, dsk, nw, wo, g1, lg, lb, sh2, sc2, *, tm):
    bsz, seq, d = x.shape
    mod_map = (lambda b, i: (b, 0, 0)) if g1.shape[0] > 1 else (lambda b, i: (0, 0, 0))

    def full(a):
        return pl.BlockSpec(a.shape, lambda b, i: (0,) * a.ndim)

    def tok(w):
        return pl.BlockSpec((1, tm, w), lambda b, i: (b, i, 0))

    mod = pl.BlockSpec((1, 1, d), mod_map)
    return pl.pallas_call(
        _mix_kernel,
        grid=(bsz, seq // tm),
        in_specs=[
            tok(SSD_WIDTH), tok(SSD_WIDTH), tok(SSD_WIDTH), tok(SSD_WIDTH), tok(FNET_WIDTH), tok(POOL_WIDTH), tok(d),
            full(dsk), full(nw), full(wo), mod, full(lg), full(lb), mod, mod,
        ],
        out_specs=[tok(d), tok(d)],
        out_shape=[jax.ShapeDtypeStruct((bsz, seq, d), F32), jax.ShapeDtypeStruct((bsz, seq, d), BF16)],
        compiler_params=_params("parallel", "parallel"),
        name="mix",
    )(yf, yb, xs, z, yfn, ypl, x, dsk, nw, wo, g1, lg, lb, sh2, sc2)


def _ffn_kernel(h_ref, hp_ref, hn_ref, wv_ref, wg_ref, cwv_ref, cwg_ref, cbv_ref, cbg_ref, wd_ref, x1_ref,
                g2_ref, lg_ref, lb_ref, o_ref, hbuf, av0, ag0, av1, ag1, acc, *, tm, on_grid):
    i = pl.program_id(1)
    j = pl.program_id(2)
    nt = pl.num_programs(1)
    nj = pl.num_programs(2) - 1
    ext = tm + 2 * FFN_HALO
    ra = ext // FFN_ROW_CHUNKS
    rb = tm // FFN_ROW_CHUNKS
    slots = ((av0, ag0), (av1, ag1))
    base = FFN_PAD + FFN_HALO
    rows = (-1, 0, 1) if on_grid else (0,)

    def up_chunk(slot, r):
        av, ag = slot
        hb = hbuf[r * ra:(r + 1) * ra, :]
        av[FFN_PAD + r * ra:FFN_PAD + (r + 1) * ra, :] = jnp.dot(hb, wv_ref[...], preferred_element_type=F32)
        ag[FFN_PAD + r * ra:FFN_PAD + (r + 1) * ra, :] = jnp.dot(hb, wg_ref[...], preferred_element_type=F32)

    def conv(a_ref, cw_ref, cb_ref, r):
        cw = cw_ref[...]
        col = (lax.broadcasted_iota(jnp.int32, (rb, 1), 0) + r * rb) & (GRID_W - 1)
        side = []
        for dc in (-1, 0, 1):
            acc_dc = None
            for dr in rows:
                k = (dr + 1) * 3 + dc + 1
                term = a_ref[pl.ds(base + r * rb + dr * GRID_W + dc, rb), :] * cw[k:k + 1]
                acc_dc = term if acc_dc is None else acc_dc + term
            side.append(acc_dc)
        left, mid, right = side
        if on_grid:
            left = jnp.where(col != 0, left, 0.0)
            right = jnp.where(col != GRID_W - 1, right, 0.0)
        return left + mid + right + cb_ref[...]

    def down_chunk(slot, r):
        av, ag = slot
        val = conv(av, cwv_ref, cbv_ref, r)
        gate = conv(ag, cwg_ref, cbg_ref, r)
        hidden = (jax.nn.gelu(gate, approximate=True) * val).astype(BF16)
        acc[r * rb:(r + 1) * rb, :] += jnp.dot(hidden, wd_ref[...], preferred_element_type=F32)

    @pl.when(j == 0)
    def _():
        hbuf[0:FFN_HALO] = jnp.where(i > 0, hp_ref[0], jnp.zeros_like(hp_ref[0]))
        hbuf[FFN_HALO:FFN_HALO + tm] = h_ref[0]
        hbuf[FFN_HALO + tm:] = jnp.where(i < nt - 1, hn_ref[0], jnp.zeros_like(hn_ref[0]))
        acc[...] = jnp.zeros_like(acc)
        for a in (av0, ag0, av1, ag1):
            a[0:FFN_PAD] = jnp.zeros((FFN_PAD, FFN_CB), F32)
            a[FFN_PAD + ext:] = jnp.zeros((FFN_PAD, FFN_CB), F32)
        for r in range(FFN_ROW_CHUNKS):
            up_chunk(slots[0], r)

    for parity in (0, 1):
        @pl.when((j > 0) & (j < nj) & (j % 2 == parity))
        def _():
            for r in range(FFN_ROW_CHUNKS):
                up_chunk(slots[parity], r)
                down_chunk(slots[1 - parity], r)

    @pl.when(j == nj)
    def _():
        for r in range(FFN_ROW_CHUNKS):
            down_chunk(slots[(D_FF // FFN_CB - 1) % 2], r)
        o_ref[0] = _ln(ALPHA * x1_ref[0] + g2_ref[0] * acc[...]) * lg_ref[...] + lb_ref[...]


def _ffn_call(h, w_up, conv_w9, conv_b, w_down, x1, g2, lg, lb, *, tm, on_grid):
    bsz, seq, d = h.shape
    nt = seq // tm
    nj = D_FF // FFN_CB
    th = tm // FFN_HALO
    nh = seq // FFN_HALO
    ext = tm + 2 * FFN_HALO
    mod_map = (lambda b, i, j: (b, 0, 0)) if g2.shape[0] > 1 else (lambda b, i, j: (0, 0, 0))

    def full(a):
        return pl.BlockSpec(a.shape, lambda b, i, j: (0,) * a.ndim)

    def up_blk(j):
        return jnp.minimum(j, nj - 1)

    def dn_blk(j):
        return jnp.maximum(j - 1, 0)

    a_scratch = pltpu.VMEM((ext + 2 * FFN_PAD, FFN_CB), F32)
    return pl.pallas_call(
        functools.partial(_ffn_kernel, tm=tm, on_grid=on_grid),
        grid=(bsz, nt, nj + 1),
        in_specs=[
            pl.BlockSpec((1, tm, d), lambda b, i, j: (b, i, 0)),
            pl.BlockSpec((1, FFN_HALO, d), lambda b, i, j: (b, jnp.maximum(i * th - 1, 0), 0)),
            pl.BlockSpec((1, FFN_HALO, d), lambda b, i, j: (b, jnp.minimum((i + 1) * th, nh - 1), 0)),
            pl.BlockSpec((d, FFN_CB), lambda b, i, j: (0, up_blk(j))),
            pl.BlockSpec((d, FFN_CB), lambda b, i, j: (0, nj + up_blk(j))),
            pl.BlockSpec((9, FFN_CB), lambda b, i, j: (0, dn_blk(j))),
            pl.BlockSpec((9, FFN_CB), lambda b, i, j: (0, nj + dn_blk(j))),
            pl.BlockSpec((1, FFN_CB), lambda b, i, j: (0, dn_blk(j))),
            pl.BlockSpec((1, FFN_CB), lambda b, i, j: (0, nj + dn_blk(j))),
            pl.BlockSpec((FFN_CB, d), lambda b, i, j: (dn_blk(j), 0)),
            pl.BlockSpec((1, tm, d), lambda b, i, j: (b, i, 0)),
            pl.BlockSpec((1, 1, d), mod_map),
            full(lg), full(lb),
        ],
        out_specs=pl.BlockSpec((1, tm, d), lambda b, i, j: (b, i, 0)),
        out_shape=jax.ShapeDtypeStruct((bsz, seq, d), F32),
        scratch_shapes=[pltpu.VMEM((ext, d), BF16), a_scratch, a_scratch, a_scratch, a_scratch,
                        pltpu.VMEM((tm, d), F32)],
        compiler_params=_params("parallel", "parallel", "arbitrary"),
        name="ffn",
    )(h, h, h, w_up, w_up, conv_w9, conv_w9, conv_b, conv_b, w_down, x1, g2, lg, lb)


def _block_diag(blocks):
    g, n, m = blocks.shape
    out = jnp.zeros((g * n, g * m), blocks.dtype)
    for k in range(g):
        out = out.at[k * n:(k + 1) * n, k * m:(k + 1) * m].set(blocks[k])
    return out


def _dft_tables(seq):
    n1 = seq // DFT_N2
    k1 = np.arange(n1, dtype=np.int64)[:, None]
    nn1 = np.arange(n1, dtype=np.int64)[None, :]
    tabs = []
    for n2 in range(DFT_N2):
        th = 2.0 * np.pi * ((k1 * (DFT_N2 * nn1 + n2)) % seq) / seq
        tabs.append(np.concatenate([np.cos(th), -np.sin(th)], axis=0))
    mtab = np.stack(tabs)
    k2 = np.arange(DFT_N2, dtype=np.int64)[:, None]
    nn2 = np.arange(DFT_N2, dtype=np.int64)[None, :]
    th2 = 2.0 * np.pi * ((k2 * nn2) % DFT_N2) / DFT_N2
    scale = 1.0 / math.sqrt(seq * FNET_GDIM)
    dmat = scale * np.concatenate([np.cos(th2), np.sin(th2)], axis=1)
    return jnp.asarray(mtab, F32), jnp.asarray(dmat, F32)


def _dense_dft_table(seq):
    k = np.arange(seq, dtype=np.int64)
    th = 2.0 * np.pi * ((k[:, None] * k[None, :]) % seq) / seq
    scale = 1.0 / math.sqrt(seq * FNET_GDIM)
    return jnp.asarray(scale * np.concatenate([np.cos(th), -np.sin(th)], axis=1), F32)


def _channel_dft_tables():
    k = np.arange(FNET_GDIM, dtype=np.int64)
    th = 2.0 * np.pi * ((k[:, None] * k[None, :]) % FNET_GDIM) / FNET_GDIM
    eye = np.eye(FNET_GROUPS)
    return jnp.asarray(np.kron(eye, np.cos(th)), F32), jnp.asarray(np.kron(eye, np.sin(th)), F32)


def _scan_matrices():
    r = np.arange(SSD_CHUNK)
    lower = (r[:, None] >= r[None, :]).astype(np.float32)
    return jnp.asarray(np.stack([lower, lower.T]), F32)


def _head_expansion():
    ex = np.zeros((LANE, SSD_WIDTH), np.float32)
    for h in range(SSD_HEADS):
        ex[h, h * SSD_HEADDIM:(h + 1) * SSD_HEADDIM] = 1.0
    return jnp.asarray(ex, BF16)


def _token_mixer(x, sh1, sc1, lw, init, tri, ex, cc, cs, *, tm):
    bsz, seq, _ = x.shape
    z, xs, bc, dt, da, pq, ypl = _inproj_call(
        x, sh1, sc1, lw["w_cat"], lw["ssd_conv_w"], lw["ssd_conv_b"], lw["dt_bias"], lw["a_log"], cc, cs,
        lw["fnet_bd"], lw["pool_bd"], lw["pool_scale"], tm=tm)
    yf, yb, fin = _ssd_call(xs, bc, dt, da, tri, ex, init)
    return z, xs, yf, yb, pq, ypl, fin


def _fourier_positions(pq):
    bsz, _, seq, w = pq.shape
    if seq % (DFT_N2 * 8) == 0:
        n1 = seq // DFT_N2
        mtab, dmat = _dft_tables(seq)
        t = _dft1_call(pq, mtab).reshape(bsz, 2 * DFT_N2, n1 * w)
        out = _dft2_call(dmat, t, tn=min(4096, n1 * w))
        return out.reshape(bsz, seq, w)
    return _dft2_call(_dense_dft_table(seq), pq.reshape(bsz, 2 * seq, w), tn=w)


def kernel(x, c, ctx, c_ctx, w_ada, b_ada, w_in, ssd_conv_w, ssd_conv_b, ssd_dt_bias, ssd_a_log, ssd_d,
           ssd_norm_w, fnet_w, pool_w, pool_scale, w_out, ln1_g, ln1_b, ffn_w_up, ffn_conv_w, ffn_conv_b,
           ffn_w_down, ln2_g, ln2_b):
    bsz, seq, d = x.shape
    n_ctx = ctx.shape[1]
    assert d == D_MODEL and bsz <= 7 and seq % (GRID_W * 8) == 0 and n_ctx % SSD_CHUNK == 0
    tm_lat = min(512, seq)
    tm_ffn = min(1024, seq)

    cv = jnp.zeros((8, d), F32).at[0:bsz].set(c).at[bsz].set(c_ctx)
    mods = _ada_call(cv, w_ada, b_ada)
    tri = _scan_matrices()
    ex = _head_expansion()
    cc, cs = _channel_dft_tables()
    zero_state = jnp.zeros((bsz,) + SSD_STATE_SHAPE, F32)
    lane_pad = jnp.zeros((d, LANE - SSD_HEADS), F32)

    for l in range(DEPTH):
        last = l == DEPTH - 1
        m = mods[l].reshape(8, 6, d)
        lat = [m[0:bsz, k][:, None, :] for k in range(6)]
        cx = [m[bsz:bsz + 1, k][:, None, :] for k in range(6)]
        w = w_in[l]
        w_cat = jnp.concatenate(
            [w[:, 0:OFF_DT], w[:, OFF_FNET:N_IN], w[:, OFF_DT:OFF_DT + SSD_HEADS], lane_pad,
             w[:, OFF_DT + SSD_HEADS:OFF_FNET], lane_pad], axis=1).astype(BF16)
        pad16 = ((0, 0), (0, LANE - SSD_HEADS))
        lw = {
            "w_cat": w_cat,
            "ssd_conv_w": ssd_conv_w[l],
            "ssd_conv_b": ssd_conv_b[l][None, :],
            "dt_bias": jnp.pad(ssd_dt_bias[l], pad16),
            "a_log": jnp.pad(ssd_a_log[l], pad16),
            "fnet_bd": _block_diag(fnet_w[l]),
            "pool_bd": _block_diag(pool_w[l]).astype(BF16),
            "pool_scale": pool_scale[l][None, :],
        }
        dsk = jnp.repeat(ssd_d[l], SSD_HEADDIM)[None, :]
        nw = ssd_norm_w[l][None, :]
        wo = w_out[l].astype(BF16)
        lg1, lb1 = ln1_g[l][None, :], ln1_b[l][None, :]
        lg2, lb2 = ln2_g[l][None, :], ln2_b[l][None, :]
        w_up = ffn_w_up[l].astype(BF16)
        w_down = ffn_w_down[l].astype(BF16)
        conv9 = ffn_conv_w[l].reshape(9, 2 * D_FF)
        conv_b = ffn_conv_b[l][None, :]

        cz, cxs, cyf, cyb, cpq, cypl, c_fin = _token_mixer(ctx, cx[0], cx[1], lw, zero_state, tri, ex, cc, cs,
                                                           tm=n_ctx)
        z, xs, yf, yb, pq, ypl, _ = _token_mixer(x, lat[0], lat[1], lw, c_fin, tri, ex, cc, cs, tm=tm_lat)

        yfn = _fourier_positions(pq)
        x1, h2 = _mix_call(yf, yb, xs, z, yfn, ypl, x, dsk, nw, wo, lat[2], lg1, lb1, lat[3], lat[4], tm=tm_lat)
        x = _ffn_call(h2, w_up, conv9, conv_b, w_down, x1, lat[5], lg2, lb2, tm=tm_ffn, on_grid=True)

        if not last:
            cyfn = _fourier_positions(cpq)
            c1, ch2 = _mix_call(cyf, cyb, cxs, cz, cyfn, cypl, ctx, dsk, nw, wo, cx[2], lg1, lb1, cx[3], cx[4],
                                tm=n_ctx)
            ctx = _ffn_call(ch2, w_up, conv9, conv_b, w_down, c1, cx[5], lg2, lb2, tm=n_ctx, on_grid=False)
    return x
```

```python
import functools
import math

import numpy as np
import jax
import jax.numpy as jnp
from jax import lax
from jax.experimental import pallas as pl
from jax.experimental.pallas import tpu as pltpu

F32 = jnp.float32
BF16 = jnp.bfloat16
HIGHEST = lax.Precision.HIGHEST

D_MODEL = 1024
DEPTH = 2
GRID_W = 64
SSD_WIDTH = 512
SSD_HEADDIM = 64
SSD_HEADS = 8
SSD_GROUPS = 2
HEADS_PER_GROUP = SSD_HEADS // SSD_GROUPS
SSD_STATE = 128
SSD_CHUNK = 128
SSD_CHUNKS_PER_STEP = 4
XBC_WIDTH = SSD_WIDTH + 2 * SSD_GROUPS * SSD_STATE
BC_WIDTH = 2 * SSD_GROUPS * SSD_STATE
SSD_STATE_SHAPE = (2, SSD_HEADS // 2, SSD_STATE, 2 * SSD_HEADDIM)
FNET_WIDTH = 256
FNET_GROUPS = 4
FNET_GDIM = 64
POOL_WINDOWS = (2, 4, 8, 16)
POOL_WIDTH = 256
POOL_GDIM = 64
POOL_HALO = 8
D_FF = 2816
ALPHA = (2.0 * DEPTH) ** 0.25
EPS = 1e-6

OFF_XBC = SSD_WIDTH
OFF_DT = OFF_XBC + XBC_WIDTH
OFF_FNET = OFF_DT + 2 * SSD_HEADS
OFF_POOL = OFF_FNET + FNET_WIDTH
N_IN = OFF_POOL + POOL_WIDTH

LANE = 128
C_Z = 0
C_XBC = C_Z + SSD_WIDTH
C_FNET = C_XBC + XBC_WIDTH
C_POOL = C_FNET + FNET_WIDTH
C_DT = C_POOL + POOL_WIDTH
N_CAT = C_DT + 2 * LANE

DFT_N2 = 64
DFT_NB = 8
FFN_CB = 256
FFN_HALO = GRID_W
FFN_PAD = 16
FFN_ROW_CHUNKS = 4
FFN_UP_CHUNKS = 2
VMEM_LIMIT = 56 * 1024 * 1024


def _ln(x):
    mu = jnp.mean(x, axis=-1, keepdims=True)
    xc = x - mu
    var = jnp.mean(xc * xc, axis=-1, keepdims=True)
    return xc * lax.rsqrt(var + EPS)


def _bdot(a, b):
    return jnp.dot(a.astype(BF16), b.astype(BF16), preferred_element_type=F32)


def _params(*sem):
    return pltpu.CompilerParams(dimension_semantics=sem, vmem_limit_bytes=VMEM_LIMIT)


def _ada_kernel(c_ref, w_ref, b_ref, o_ref):
    cv = c_ref[...]
    s = cv * jax.nn.sigmoid(cv)
    o_ref[0] = jnp.dot(s, w_ref[0], preferred_element_type=F32, precision=HIGHEST) + b_ref[0]


def _ada_call(cv, w_ada, b_ada):
    depth, d, n = w_ada.shape
    tn = 1536
    return pl.pallas_call(
        _ada_kernel,
        grid=(depth, n // tn),
        in_specs=[
            pl.BlockSpec((8, d), lambda l, j: (0, 0)),
            pl.BlockSpec((1, d, tn), lambda l, j: (l, 0, j)),
            pl.BlockSpec((1, 1, tn), lambda l, j: (l, 0, j)),
        ],
        out_specs=pl.BlockSpec((1, 8, tn), lambda l, j: (l, 0, j)),
        out_shape=jax.ShapeDtypeStruct((depth, 8, n), F32),
        compiler_params=_params("parallel", "parallel"),
        name="ada",
    )(cv, w_ada, b_ada.reshape(depth, 1, n))


def _inproj_kernel(x_ref, xp_ref, xn_ref, sh_ref, sc_ref, w_ref, cw_ref, cb_ref, dtb_ref, alog_ref,
                   cc_ref, cs_ref, fw_ref, pw_ref, ps_ref,
                   z_ref, xs_ref, bc_ref, dt_ref, da_ref, pq_ref, yp_ref, *, tm, seq_len):
    i = pl.program_id(1)
    nt = pl.num_programs(1)
    sh = sh_ref[0]
    sc = sc_ref[0]

    def modulated(xv):
        return (_ln(xv) * (1.0 + sc) + sh).astype(BF16)

    u = jnp.dot(modulated(x_ref[0]), w_ref[...], preferred_element_type=F32)
    w_halo = w_ref[:, C_XBC:C_DT]
    up = jnp.dot(modulated(xp_ref[0]), w_halo, preferred_element_type=F32)
    un = jnp.dot(modulated(xn_ref[0]), w_halo, preferred_element_type=F32)
    up = jnp.where(i > 0, up, 0.0)
    un = jnp.where(i < nt - 1, un, 0.0)

    z_ref[0] = u[:, C_Z:C_XBC].astype(BF16)

    xbc = u[:, C_XBC:C_FNET]
    row = lax.broadcasted_iota(jnp.int32, (tm, 1), 0)
    prev_row = up[POOL_HALO - 1:POOL_HALO, 0:XBC_WIDTH]
    next_row = un[0:1, 0:XBC_WIDTH]
    xm1 = jnp.where(row == 0, prev_row, pltpu.roll(xbc, 1, 0))
    xp1 = jnp.where(row == tm - 1, next_row, pltpu.roll(xbc, tm - 1, 0))
    cw = cw_ref[...]
    conv = xm1 * cw[0:1] + xbc * cw[1:2] + xp1 * cw[2:3] + cb_ref[...]
    act = conv * jax.nn.sigmoid(conv)
    xs_ref[0] = act[:, 0:SSD_WIDTH].astype(BF16)
    bc_ref[0] = act[:, SSD_WIDTH:XBC_WIDTH].astype(BF16)

    lane = lax.broadcasted_iota(jnp.int32, (1, LANE), 1)
    for d in range(2):
        raw = u[:, C_DT + d * LANE:C_DT + (d + 1) * LANE] + dtb_ref[d:d + 1, :]
        sp = jnp.maximum(raw, 0.0) + jnp.log1p(jnp.exp(-jnp.abs(raw)))
        dt = jnp.where(lane < SSD_HEADS, sp, 0.0)
        dt_ref[0, d] = dt
        da_ref[0, d] = dt * (-jnp.exp(alog_ref[d:d + 1, :]))

    uf = u[:, C_FNET:C_POOL]
    fw = fw_ref[...]
    cw_mix = jnp.dot(cc_ref[...], fw, preferred_element_type=F32, precision=HIGHEST)
    sw_mix = jnp.dot(cs_ref[...], fw, preferred_element_type=F32, precision=HIGHEST)
    pq_ref[0, 0] = _bdot(uf, cw_mix)
    pq_ref[0, 1] = _bdot(uf, sw_mix)

    off_pool = C_POOL - C_XBC
    ext = jnp.concatenate([up[:, off_pool:off_pool + POOL_WIDTH], u[:, C_POOL:C_DT],
                           un[:, off_pool:off_pool + POOL_WIDTH]], axis=0)
    n_ext = tm + 2 * POOL_HALO
    sums = {}
    cur = ext
    width = 1
    while width < POOL_WINDOWS[-1]:
        cur = cur + pltpu.roll(cur, n_ext - width, 0)
        width *= 2
        sums[width] = cur
    lane_p = lax.broadcasted_iota(jnp.int32, (1, POOL_WIDTH), 1)
    pos = i * tm + lax.broadcasted_iota(jnp.int32, (tm, 1), 0)
    wsum = None
    cnt = None
    for gi, win in enumerate(POOL_WINDOWS):
        left = win // 2
        start = POOL_HALO - left
        s_w = sums[win]
        if start:
            s_w = pltpu.roll(s_w, n_ext - start, 0)
        s_w = s_w[0:tm]
        c_w = jnp.minimum(pos + (win - left), seq_len) - jnp.maximum(pos - left, 0)
        if wsum is None:
            wsum, cnt = s_w, c_w
        else:
            sel = lane_p >= gi * POOL_GDIM
            wsum = jnp.where(sel, s_w, wsum)
            cnt = jnp.where(sel, c_w, cnt)
    pooled = wsum / cnt.astype(F32) - u[:, C_POOL:C_DT]
    yp_ref[0] = (_bdot(pooled, pw_ref[...]) * ps_ref[...]).astype(BF16)


def _inproj_call(x, sh, sc, w_cat, conv_w, conv_b, dt_bias, a_log, cc, cs, fw, pw, ps, *, tm):
    bsz, seq, d = x.shape
    nt = seq // tm
    nb8 = seq // 8
    tb = tm // 8
    mod_map = (lambda b, i: (b, 0, 0)) if sh.shape[0] > 1 else (lambda b, i: (0, 0, 0))

    def full(a):
        return pl.BlockSpec(a.shape, lambda b, i: (0,) * a.ndim)

    in_specs = [
        pl.BlockSpec((1, tm, d), lambda b, i: (b, i, 0)),
        pl.BlockSpec((1, 8, d), lambda b, i: (b, jnp.maximum(i * tb - 1, 0), 0)),
        pl.BlockSpec((1, 8, d), lambda b, i: (b, jnp.minimum((i + 1) * tb, nb8 - 1), 0)),
        pl.BlockSpec((1, 1, d), mod_map),
        pl.BlockSpec((1, 1, d), mod_map),
    ] + [full(a) for a in (w_cat, conv_w, conv_b, dt_bias, a_log, cc, cs, fw, pw, ps)]
    out_shape = [
        jax.ShapeDtypeStruct((bsz, seq, SSD_WIDTH), BF16),
        jax.ShapeDtypeStruct((bsz, seq, SSD_WIDTH), BF16),
        jax.ShapeDtypeStruct((bsz, seq, BC_WIDTH), BF16),
        jax.ShapeDtypeStruct((bsz, 2, seq, LANE), F32),
        jax.ShapeDtypeStruct((bsz, 2, seq, LANE), F32),
        jax.ShapeDtypeStruct((bsz, 2, seq, FNET_WIDTH), F32),
        jax.ShapeDtypeStruct((bsz, seq, POOL_WIDTH), BF16),
    ]
    out_specs = [
        pl.BlockSpec((1, tm, SSD_WIDTH), lambda b, i: (b, i, 0)),
        pl.BlockSpec((1, tm, SSD_WIDTH), lambda b, i: (b, i, 0)),
        pl.BlockSpec((1, tm, BC_WIDTH), lambda b, i: (b, i, 0)),
        pl.BlockSpec((1, 2, tm, LANE), lambda b, i: (b, 0, i, 0)),
        pl.BlockSpec((1, 2, tm, LANE), lambda b, i: (b, 0, i, 0)),
        pl.BlockSpec((1, 2, tm, FNET_WIDTH), lambda b, i: (b, 0, i, 0)),
        pl.BlockSpec((1, tm, POOL_WIDTH), lambda b, i: (b, i, 0)),
    ]
    return pl.pallas_call(
        functools.partial(_inproj_kernel, tm=tm, seq_len=seq),
        grid=(bsz, nt),
        in_specs=in_specs,
        out_specs=out_specs,
        out_shape=out_shape,
        compiler_params=_params("parallel", "parallel"),
        name="inproj",
    )(x, x, x, sh, sc, w_cat, conv_w, conv_b, dt_bias, a_log, cc, cs, fw, pw, ps)


def _bf16_pieces(x, terms):
    pieces = []
    rem = x
    for t in range(terms):
        piece = rem.astype(BF16)
        pieces.append(piece)
        if t + 1 < terms:
            rem = rem - piece.astype(F32)
    return pieces


def _ssd_chunk(d, k, xs_ref, bc_ref, dt_ref, da_ref, tri_ref, ex_ref, y_ref, st_ref):
    rows = slice(k * SSD_CHUNK, (k + 1) * SSD_CHUNK)
    tri = tri_ref[d]
    a = da_ref[0, 0, rows, :]
    a_parts = jnp.concatenate(_bf16_pieces(a, 3), axis=1)
    cs3 = jnp.dot(tri.astype(BF16), a_parts, preferred_element_type=F32)
    cs = cs3[:, 0:LANE] + cs3[:, LANE:2 * LANE] + cs3[:, 2 * LANE:3 * LANE]
    cs_t = cs.T
    mask = tri > 0.5
    stack = jnp.concatenate(_bf16_pieces(cs, 2) + _bf16_pieces(dt_ref[0, 0, rows, :], 2), axis=0)
    ex4 = jnp.dot(stack, ex_ref[...], preferred_element_type=F32)
    c = SSD_CHUNK
    cs_x = ex4[0:c] + ex4[c:2 * c]
    dt_x = ex4[2 * c:3 * c] + ex4[3 * c:4 * c]
    last = SSD_CHUNK - 1 if d == 0 else 0
    tot_x = cs_x[last:last + 1, :]
    e_cs = jnp.exp(cs_x)
    e_tot = jnp.exp(tot_x)
    xd = xs_ref[0, rows, :].astype(F32) * dt_x
    xd_dec = (xd * jnp.exp(tot_x - cs_x)).astype(BF16)
    xd = xd.astype(BF16)
    bc = bc_ref[0, rows, :]
    gn = SSD_GROUPS * SSD_STATE
    pw = 2 * SSD_HEADDIM
    low_half = lax.broadcasted_iota(jnp.int32, (1, pw), 1) < SSD_HEADDIM
    zero = jnp.zeros((SSD_CHUNK, pw), BF16)
    for g in range(SSD_GROUPS):
        bg_t = bc[:, g * SSD_STATE:(g + 1) * SSD_STATE].astype(F32).T.astype(BF16)
        cg = bc[:, gn + g * SSD_STATE:gn + (g + 1) * SSD_STATE]
        cb = jnp.dot(cg, bg_t, preferred_element_type=F32)
        for pp in range(HEADS_PER_GROUP // 2):
            p = g * (HEADS_PER_GROUP // 2) + pp
            lanes = slice(p * pw, (p + 1) * pw)
            decay = []
            for h in (2 * p, 2 * p + 1):
                seg = cs[:, h:h + 1] - cs_t[h:h + 1, :]
                decay.append(cb * jnp.where(mask, jnp.exp(jnp.minimum(seg, 0.0)), 0.0))
            m = jnp.concatenate(decay, axis=1).astype(BF16)
            xp = xd[:, lanes]
            x_bd = jnp.concatenate([jnp.where(low_half, xp, zero), jnp.where(low_half, zero, xp)], axis=0)
            st = st_ref[d, p]
            y_diag = jnp.dot(m, x_bd, preferred_element_type=F32)
            y_off = jnp.dot(cg, st.astype(BF16), preferred_element_type=F32) * e_cs[:, lanes]
            y_ref[0, rows, lanes] = (y_diag + y_off).astype(BF16)
            st_ref[d, p] = st * e_tot[:, lanes] + jnp.dot(bg_t, xd_dec[:, lanes], preferred_element_type=F32)


def _ssd_kernel(xsf_ref, bcf_ref, dtf_ref, daf_ref, xsb_ref, bcb_ref, dtb_ref, dab_ref, tri_ref, ex_ref, init_ref,
                yf_ref, yb_ref, fin_ref, st_ref, *, kc):
    s = pl.program_id(1)
    ns = pl.num_programs(1)

    @pl.when(s == 0)
    def _():
        st_ref[...] = init_ref[0]

    for k in range(kc):
        _ssd_chunk(0, k, xsf_ref, bcf_ref, dtf_ref, daf_ref, tri_ref, ex_ref, yf_ref, st_ref)
        _ssd_chunk(1, kc - 1 - k, xsb_ref, bcb_ref, dtb_ref, dab_ref, tri_ref, ex_ref, yb_ref, st_ref)

    @pl.when(s == ns - 1)
    def _():
        fin_ref[0] = st_ref[...]


def _ssd_call(xs, bc, dt, da, tri, ex, init):
    bsz, seq, _ = xs.shape
    kc = min(SSD_CHUNKS_PER_STEP, seq // SSD_CHUNK)
    rows = kc * SSD_CHUNK
    ns = seq // rows
    st_block = (1,) + SSD_STATE_SHAPE

    def tok(w, rev):
        return pl.BlockSpec((1, rows, w), lambda b, s: (b, ns - 1 - s if rev else s, 0))

    def per_dir(w, d):
        return pl.BlockSpec((1, 1, rows, w), lambda b, s: (b, d, ns - 1 - s if d else s, 0))

    return pl.pallas_call(
        functools.partial(_ssd_kernel, kc=kc),
        grid=(bsz, ns),
        in_specs=[
            tok(SSD_WIDTH, False), tok(BC_WIDTH, False), per_dir(LANE, 0), per_dir(LANE, 0),
            tok(SSD_WIDTH, True), tok(BC_WIDTH, True), per_dir(LANE, 1), per_dir(LANE, 1),
            pl.BlockSpec(tri.shape, lambda b, s: (0, 0, 0)),
            pl.BlockSpec(ex.shape, lambda b, s: (0, 0)),
            pl.BlockSpec(st_block, lambda b, s: (b, 0, 0, 0, 0)),
        ],
        out_specs=[tok(SSD_WIDTH, False), tok(SSD_WIDTH, True), pl.BlockSpec(st_block, lambda b, s: (b, 0, 0, 0, 0))],
        out_shape=[
            jax.ShapeDtypeStruct((bsz, seq, SSD_WIDTH), BF16),
            jax.ShapeDtypeStruct((bsz, seq, SSD_WIDTH), BF16),
            jax.ShapeDtypeStruct((bsz,) + SSD_STATE_SHAPE, F32),
        ],
        scratch_shapes=[pltpu.VMEM(SSD_STATE_SHAPE, F32)],
        compiler_params=_params("parallel", "arbitrary"),
        name="ssd",
    )(xs, bc, dt, da, xs, bc, dt, da, tri, ex, init)


def _dft1_kernel(p_ref, q_ref, m_ref, t_ref, *, n1):
    for j in range(DFT_NB):
        m = m_ref[j]
        rp = _bdot(m, p_ref[0, 0, :, j, :])
        rq = _bdot(m, q_ref[0, 0, :, j, :])
        t_ref[0, 0, j] = (rp[:n1] + rq[n1:]).astype(BF16)
        t_ref[0, 1, j] = (rp[n1:] - rq[:n1]).astype(BF16)


def _dft1_call(pq, mtab):
    bsz, _, seq, w = pq.shape
    n1 = seq // DFT_N2
    pqv = pq.reshape(bsz, 2, n1, DFT_N2, w)
    return pl.pallas_call(
        functools.partial(_dft1_kernel, n1=n1),
        grid=(bsz, DFT_N2 // DFT_NB),
        in_specs=[
            pl.BlockSpec((1, 1, n1, DFT_NB, w), lambda b, j: (b, 0, 0, j, 0)),
            pl.BlockSpec((1, 1, n1, DFT_NB, w), lambda b, j: (b, 1, 0, j, 0)),
            pl.BlockSpec((DFT_NB, 2 * n1, n1), lambda b, j: (j, 0, 0)),
        ],
        out_specs=pl.BlockSpec((1, 2, DFT_NB, n1, w), lambda b, j: (b, 0, j, 0, 0)),
        out_shape=jax.ShapeDtypeStruct((bsz, 2, DFT_N2, n1, w), BF16),
        compiler_params=_params("parallel", "parallel"),
        name="dft1",
    )(pqv, pqv, mtab)


def _dft2_kernel(d_ref, t_ref, o_ref):
    o_ref[0] = _bdot(d_ref[...], t_ref[0]).astype(BF16)


def _dft2_call(dmat, t, *, tn):
    bsz, k, n = t.shape
    mo = dmat.shape[0]
    return pl.pallas_call(
        _dft2_kernel,
        grid=(bsz, n // tn),
        in_specs=[
            pl.BlockSpec((mo, k), lambda b, j: (0, 0)),
            pl.BlockSpec((1, k, tn), lambda b, j: (b, 0, j)),
        ],
        out_specs=pl.BlockSpec((1, mo, tn), lambda b, j: (b, 0, j)),
        out_shape=jax.ShapeDtypeStruct((bsz, mo, n), BF16),
        compiler_params=_params("parallel", "parallel"),
        name="dft2",
    )(dmat, t)


def _mix_kernel(yf_ref, yb_ref, xs_ref, z_ref, yfn_ref, ypl_ref, x_ref, dsk_ref, nw_ref, wo_ref, g1_ref,
                lg_ref, lb_ref, sh2_ref, sc2_ref, x1_ref, h2_ref):
    y = yf_ref[0].astype(F32) + yb_ref[0].astype(F32) + xs_ref[0].astype(F32) * dsk_ref[...]
    z = z_ref[0].astype(F32)
    y = y * (z * jax.nn.sigmoid(z))
    gw = SSD_WIDTH // SSD_GROUPS
    parts = []
    for g in range(SSD_GROUPS):
        yg = y[:, g * gw:(g + 1) * gw]
        ms = jnp.mean(yg * yg, axis=-1, keepdims=True)
        parts.append(yg * lax.rsqrt(ms + EPS))
    yn = jnp.concatenate(parts, axis=-1) * nw_ref[...]
    o_f = SSD_WIDTH
    o_p = SSD_WIDTH + FNET_WIDTH
    mix = (_bdot(yn, wo_ref[0:o_f, :]) + _bdot(yfn_ref[0], wo_ref[o_f:o_p, :])
           + _bdot(ypl_ref[0], wo_ref[o_p:, :]))
    x1 = _ln(ALPHA * x_ref[0] + g1_ref[0] * mix) * lg_ref[...] + lb_ref[...]
    x1_ref[0] = x1
    h2_ref[0] = (_ln(x1) * (1.0 + sc2_ref[0]) + sh2_ref[0]).astype(BF16)


def _mix_call(yf, yb, xs, z, yfn, ypl, x, dsk, nw, wo, g1, lg, lb, sh2, sc2, *, tm):
    bsz, seq, d = x.shape
    mod_map = (lambda b, i: (b, 0, 0)) if g1.shape[0] > 1 else (lambda b, i: (0, 0, 0))

    def full(a):
        return pl.BlockSpec(a.shape, lambda b, i: (0,) * a.ndim)

    def tok(w):
        return pl.BlockSpec((1, tm, w), lambda b, i: (b, i, 0))

    mod = pl.BlockSpec((1, 1, d), mod_map)
    return pl.pallas_call(
        _mix_kernel,
        grid=(bsz, seq // tm),
        in_specs=[
            tok(SSD_WIDTH), tok(SSD_WIDTH), tok(SSD_WIDTH), tok(SSD_WIDTH), tok(FNET_WIDTH), tok(POOL_WIDTH), tok(d),
            full(dsk), full(nw), full(wo), mod, full(lg), full(lb), mod, mod,
        ],
        out_specs=[tok(d), tok(d)],
        out_shape=[jax.ShapeDtypeStruct((bsz, seq, d), F32), jax.ShapeDtypeStruct((bsz, seq, d), BF16)],
        compiler_params=_params("parallel", "parallel"),
        name="mix",
    )(yf, yb, xs, z, yfn, ypl, x, dsk, nw, wo, g1, lg, lb, sh2, sc2)


def _ffn_kernel(h_ref, hp_ref, hn_ref, wu_ref, cw_ref, cb_ref, wd_ref, x1_ref, g2_ref, lg_ref, lb_ref, o_ref,
                hbuf, a0, a1, hid, acc, *, tm, on_grid):
    i = pl.program_id(1)
    nt = pl.num_programs(1)
    nj = D_FF // FFN_CB
    ext = tm + 2 * FFN_HALO
    ra = ext // FFN_UP_CHUNKS
    rb = tm // FFN_ROW_CHUNKS
    slots = (a0, a1)
    base = FFN_PAD + FFN_HALO
    rows = (-1, 0, 1) if on_grid else (0,)

    def up_project(a_ref, j):
        for r in range(FFN_UP_CHUNKS):
            a = jnp.dot(hbuf[r * ra:(r + 1) * ra, :], wu_ref[j], preferred_element_type=F32)
            a_ref[FFN_PAD + r * ra:FFN_PAD + (r + 1) * ra, :] = a.astype(BF16)

    def conv(a_ref, j, r):
        cw = cw_ref[j].astype(BF16)
        m = FFN_PAD
        n = rb + 2 * m
        start = base + r * rb - m
        col = (lax.broadcasted_iota(jnp.int32, (n, 1), 0) + (r * rb - m)) & (GRID_W - 1)
        taps = {dr: a_ref[pl.ds(start + dr * GRID_W, n), :] for dr in rows}

        def column(dc, lo, hi):
            acc_dc = None
            for dr in rows:
                k = (dr + 1) * 3 + dc + 1
                term = taps[dr][lo:hi] * cw[k:k + 1]
                acc_dc = term if acc_dc is None else acc_dc + term
            return acc_dc

        left = column(-1, 0, n)
        right = column(1, 0, n)
        if on_grid:
            zero = jnp.zeros_like(left)
            left = jnp.where(col == GRID_W - 1, zero, left)
            right = jnp.where(col == 0, zero, right)
        left2 = jnp.concatenate([left[n - 2:], left[:n - 2]], axis=0)
        u = (left2 + right)[m + 1:m + 1 + rb]
        return column(0, m, m + rb) + u + cb_ref[j].astype(BF16)

    hbuf[0:FFN_HALO] = jnp.where(i > 0, hp_ref[0], jnp.zeros_like(hp_ref[0]))
    hbuf[FFN_HALO:FFN_HALO + tm] = h_ref[0]
    hbuf[FFN_HALO + tm:] = jnp.where(i < nt - 1, hn_ref[0], jnp.zeros_like(hn_ref[0]))
    for a in slots:
        a[0:FFN_PAD] = jnp.zeros((FFN_PAD, 2 * FFN_CB), BF16)
        a[FFN_PAD + ext:] = jnp.zeros((FFN_PAD, 2 * FFN_CB), BF16)

    up_project(slots[0], 0)
    for j in range(nj):
        if j + 1 < nj:
            up_project(slots[(j + 1) % 2], j + 1)
        half = j % 2
        for r in range(FFN_ROW_CHUNKS):
            c = conv(slots[half], j, r)
            hidden = jax.nn.gelu(c[:, FFN_CB:], approximate=True) * c[:, :FFN_CB]
            hid[r * rb:(r + 1) * rb, half * FFN_CB:(half + 1) * FFN_CB] = hidden
        if half == 1 or j == nj - 1:
            first = j - half
            k = (half + 1) * FFN_CB
            part = jnp.dot(hid[:, 0:k], wd_ref[first * FFN_CB:first * FFN_CB + k, :], preferred_element_type=F32)
            if first == 0:
                acc[...] = part
            else:
                acc[...] += part

    o_ref[0] = _ln(ALPHA * x1_ref[0] + g2_ref[0] * acc[...]) * lg_ref[...] + lb_ref[...]


def _ffn_call(h, w_up, conv_w9, conv_b, w_down, x1, g2, lg, lb, *, tm, on_grid):
    bsz, seq, d = h.shape
    nt = seq // tm
    th = tm // FFN_HALO
    nh = seq // FFN_HALO
    ext = tm + 2 * FFN_HALO
    mod_map = (lambda b, i: (b, 0, 0)) if g2.shape[0] > 1 else (lambda b, i: (0, 0, 0))

    def resident(a):
        return pl.BlockSpec(a.shape, lambda b, i: (0,) * a.ndim, pipeline_mode=pl.Buffered(1))

    a_scratch = pltpu.VMEM((ext + 2 * FFN_PAD, 2 * FFN_CB), BF16)
    return pl.pallas_call(
        functools.partial(_ffn_kernel, tm=tm, on_grid=on_grid),
        grid=(bsz, nt),
        in_specs=[
            pl.BlockSpec((1, tm, d), lambda b, i: (b, i, 0)),
            pl.BlockSpec((1, FFN_HALO, d), lambda b, i: (b, jnp.maximum(i * th - 1, 0), 0)),
            pl.BlockSpec((1, FFN_HALO, d), lambda b, i: (b, jnp.minimum((i + 1) * th, nh - 1), 0)),
            resident(w_up), resident(conv_w9), resident(conv_b), resident(w_down),
            pl.BlockSpec((1, tm, d), lambda b, i: (b, i, 0)),
            pl.BlockSpec((1, 1, d), mod_map),
            resident(lg), resident(lb),
        ],
        out_specs=pl.BlockSpec((1, tm, d), lambda b, i: (b, i, 0)),
        out_shape=jax.ShapeDtypeStruct((bsz, seq, d), F32),
        scratch_shapes=[pltpu.VMEM((ext, d), BF16), a_scratch, a_scratch, pltpu.VMEM((tm, 2 * FFN_CB), BF16),
                        pltpu.VMEM((tm, d), F32)],
        compiler_params=_params("parallel", "arbitrary"),
        name="ffn",
    )(h, h, h, w_up, conv_w9, conv_b, w_down, x1, g2, lg, lb)


def _block_diag(blocks):
    g, n, m = blocks.shape
    out = jnp.zeros((g * n, g * m), blocks.dtype)
    for k in range(g):
        out = out.at[k * n:(k + 1) * n, k * m:(k + 1) * m].set(blocks[k])
    return out


def _pair_blocks(w):
    rows = w.shape[0]
    halves = w.reshape(rows, 2, D_FF // FFN_CB, FFN_CB)
    return jnp.transpose(halves, (2, 0, 1, 3)).reshape(D_FF // FFN_CB, rows, 2 * FFN_CB)


def _dft_tables(seq):
    n1 = seq // DFT_N2
    k1 = np.arange(n1, dtype=np.int64)[:, None]
    nn1 = np.arange(n1, dtype=np.int64)[None, :]
    tabs = []
    for n2 in range(DFT_N2):
        th = 2.0 * np.pi * ((k1 * (DFT_N2 * nn1 + n2)) % seq) / seq
        tabs.append(np.concatenate([np.cos(th), -np.sin(th)], axis=0))
    mtab = np.stack(tabs)
    k2 = np.arange(DFT_N2, dtype=np.int64)[:, None]
    nn2 = np.arange(DFT_N2, dtype=np.int64)[None, :]
    th2 = 2.0 * np.pi * ((k2 * nn2) % DFT_N2) / DFT_N2
    scale = 1.0 / math.sqrt(seq * FNET_GDIM)
    dmat = scale * np.concatenate([np.cos(th2), np.sin(th2)], axis=1)
    return jnp.asarray(mtab, F32), jnp.asarray(dmat, F32)


def _dense_dft_table(seq):
    k = np.arange(seq, dtype=np.int64)
    th = 2.0 * np.pi * ((k[:, None] * k[None, :]) % seq) / seq
    scale = 1.0 / math.sqrt(seq * FNET_GDIM)
    return jnp.asarray(scale * np.concatenate([np.cos(th), -np.sin(th)], axis=1), F32)


def _channel_dft_tables():
    k = np.arange(FNET_GDIM, dtype=np.int64)
    th = 2.0 * np.pi * ((k[:, None] * k[None, :]) % FNET_GDIM) / FNET_GDIM
    eye = np.eye(FNET_GROUPS)
    return jnp.asarray(np.kron(eye, np.cos(th)), F32), jnp.asarray(np.kron(eye, np.sin(th)), F32)


def _scan_matrices():
    r = np.arange(SSD_CHUNK)
    lower = (r[:, None] >= r[None, :]).astype(np.float32)
    return jnp.asarray(np.stack([lower, lower.T]), F32)


def _head_expansion():
    ex = np.zeros((LANE, SSD_WIDTH), np.float32)
    for h in range(SSD_HEADS):
        ex[h, h * SSD_HEADDIM:(h + 1) * SSD_HEADDIM] = 1.0
    return jnp.asarray(ex, BF16)


def _token_mixer(x, sh1, sc1, lw, init, tri, ex, cc, cs, *, tm):
    bsz, seq, _ = x.shape
    z, xs, bc, dt, da, pq, ypl = _inproj_call(
        x, sh1, sc1, lw["w_cat"], lw["ssd_conv_w"], lw["ssd_conv_b"], lw["dt_bias"], lw["a_log"], cc, cs,
        lw["fnet_bd"], lw["pool_bd"], lw["pool_scale"], tm=tm)
    yf, yb, fin = _ssd_call(xs, bc, dt, da, tri, ex, init)
    return z, xs, yf, yb, pq, ypl, fin


def _fourier_positions(pq):
    bsz, _, seq, w = pq.shape
    if seq % (DFT_N2 * 8) == 0:
        n1 = seq // DFT_N2
        mtab, dmat = _dft_tables(seq)
        t = _dft1_call(pq, mtab).reshape(bsz, 2 * DFT_N2, n1 * w)
        out = _dft2_call(dmat, t, tn=min(4096, n1 * w))
        return out.reshape(bsz, seq, w)
    return _dft2_call(_dense_dft_table(seq), pq.reshape(bsz, 2 * seq, w), tn=w)


def kernel(x, c, ctx, c_ctx, w_ada, b_ada, w_in, ssd_conv_w, ssd_conv_b, ssd_dt_bias, ssd_a_log, ssd_d,
           ssd_norm_w, fnet_w, pool_w, pool_scale, w_out, ln1_g, ln1_b, ffn_w_up, ffn_conv_w, ffn_conv_b,
           ffn_w_down, ln2_g, ln2_b):
    bsz, seq, d = x.shape
    n_ctx = ctx.shape[1]
    assert d == D_MODEL and bsz <= 7 and seq % (GRID_W * 8) == 0 and n_ctx % SSD_CHUNK == 0
    tm_lat = min(512, seq)
    tm_ffn = min(1024, seq)

    cv = jnp.zeros((8, d), F32).at[0:bsz].set(c).at[bsz].set(c_ctx)
    mods = _ada_call(cv, w_ada, b_ada)
    tri = _scan_matrices()
    ex = _head_expansion()
    cc, cs = _channel_dft_tables()
    zero_state = jnp.zeros((bsz,) + SSD_STATE_SHAPE, F32)
    lane_pad = jnp.zeros((d, LANE - SSD_HEADS), F32)

    for l in range(DEPTH):
        last = l == DEPTH - 1
        m = mods[l].reshape(8, 6, d)
        lat = [m[0:bsz, k][:, None, :] for k in range(6)]
        cx = [m[bsz:bsz + 1, k][:, None, :] for k in range(6)]
        w = w_in[l]
        w_cat = jnp.concatenate(
            [w[:, 0:OFF_DT], w[:, OFF_FNET:N_IN], w[:, OFF_DT:OFF_DT + SSD_HEADS], lane_pad,
             w[:, OFF_DT + SSD_HEADS:OFF_FNET], lane_pad], axis=1).astype(BF16)
        pad16 = ((0, 0), (0, LANE - SSD_HEADS))
        lw = {
            "w_cat": w_cat,
            "ssd_conv_w": ssd_conv_w[l],
            "ssd_conv_b": ssd_conv_b[l][None, :],
            "dt_bias": jnp.pad(ssd_dt_bias[l], pad16),
            "a_log": jnp.pad(ssd_a_log[l], pad16),
            "fnet_bd": _block_diag(fnet_w[l]),
            "pool_bd": _block_diag(pool_w[l]).astype(BF16),
            "pool_scale": pool_scale[l][None, :],
        }
        dsk = jnp.repeat(ssd_d[l], SSD_HEADDIM)[None, :]
        nw = ssd_norm_w[l][None, :]
        wo = w_out[l].astype(BF16)
        lg1, lb1 = ln1_g[l][None, :], ln1_b[l][None, :]
        lg2, lb2 = ln2_g[l][None, :], ln2_b[l][None, :]
        w_up = _pair_blocks(ffn_w_up[l]).astype(BF16)
        w_down = ffn_w_down[l].astype(BF16)
        conv9 = _pair_blocks(ffn_conv_w[l].reshape(9, 2 * D_FF))
        conv_b = _pair_blocks(ffn_conv_b[l][None, :])

        cz, cxs, cyf, cyb, cpq, cypl, c_fin = _token_mixer(ctx, cx[0], cx[1], lw, zero_state, tri, ex, cc, cs,
                                                           tm=n_ctx)
        z, xs, yf, yb, pq, ypl, _ = _token_mixer(x, lat[0], lat[1], lw, c_fin, tri, ex, cc, cs, tm=tm_lat)

        yfn = _fourier_positions(pq)
        x1, h2 = _mix_call(yf, yb, xs, z, yfn, ypl, x, dsk, nw, wo, lat[2], lg1, lb1, lat[3], lat[4], tm=tm_lat)
        x = _ffn_call(h2, w_up, conv9, conv_b, w_down, x1, lat[5], lg2, lb2, tm=tm_ffn, on_grid=True)

        if not last:
            cyfn = _fourier_positions(cpq)
            c1, ch2 = _mix_call(cyf, cyb, cxs, cz, cyfn, cypl, ctx, dsk, nw, wo, cx[2], lg1, lb1, cx[3], cx[4],
                                tm=n_ctx)
            ctx = _ffn_call(ch2, w_up, conv9, conv_b, w_down, c1, cx[5], lg2, lb2, tm=n_ctx, on_grid=False)
    return x
```

```python
import functools
import math

import numpy as np
import jax
import jax.numpy as jnp
from jax import lax
from jax.experimental import pallas as pl
from jax.experimental.pallas import tpu as pltpu

F32 = jnp.float32
BF16 = jnp.bfloat16
HIGHEST = lax.Precision.HIGHEST

D_MODEL = 1024
DEPTH = 2
GRID_W = 64
SSD_WIDTH = 512
SSD_HEADDIM = 64
SSD_HEADS = 8
SSD_GROUPS = 2
HEADS_PER_GROUP = SSD_HEADS // SSD_GROUPS
SSD_STATE = 128
SSD_CHUNK = 128
SSD_CHUNKS_PER_STEP = 4
XBC_WIDTH = SSD_WIDTH + 2 * SSD_GROUPS * SSD_STATE
BC_WIDTH = 2 * SSD_GROUPS * SSD_STATE
SSD_STATE_SHAPE = (2, SSD_GROUPS, SSD_STATE, HEADS_PER_GROUP * SSD_HEADDIM)
FNET_WIDTH = 256
FNET_GROUPS = 4
FNET_GDIM = 64
POOL_WINDOWS = (2, 4, 8, 16)
POOL_WIDTH = 256
POOL_GDIM = 64
POOL_HALO = 8
D_FF = 2816
ALPHA = (2.0 * DEPTH) ** 0.25
EPS = 1e-6

OFF_XBC = SSD_WIDTH
OFF_DT = OFF_XBC + XBC_WIDTH
OFF_FNET = OFF_DT + 2 * SSD_HEADS
OFF_POOL = OFF_FNET + FNET_WIDTH
N_IN = OFF_POOL + POOL_WIDTH

LANE = 128
C_Z = 0
C_XBC = C_Z + SSD_WIDTH
C_FNET = C_XBC + XBC_WIDTH
C_POOL = C_FNET + FNET_WIDTH
C_DT = C_POOL + POOL_WIDTH
N_CAT = C_DT + 2 * LANE

DFT_N2 = 64
DFT_NB = 8
FFN_CB = 256
FFN_HALO = GRID_W
FFN_PAD = 16
FFN_ROW_CHUNKS = 4
FFN_UP_CHUNKS = 2
VMEM_LIMIT = 56 * 1024 * 1024


def _ln(x):
    mu = jnp.mean(x, axis=-1, keepdims=True)
    xc = x - mu
    var = jnp.mean(xc * xc, axis=-1, keepdims=True)
    return xc * lax.rsqrt(var + EPS)


def _bdot(a, b):
    return jnp.dot(a.astype(BF16), b.astype(BF16), preferred_element_type=F32)


def _params(*sem):
    return pltpu.CompilerParams(dimension_semantics=sem, vmem_limit_bytes=VMEM_LIMIT)


def _ada_kernel(c_ref, w_ref, b_ref, o_ref):
    cv = c_ref[...]
    s = cv * jax.nn.sigmoid(cv)
    o_ref[0] = jnp.dot(s, w_ref[0], preferred_element_type=F32, precision=HIGHEST) + b_ref[0]


def _ada_call(cv, w_ada, b_ada):
    depth, d, n = w_ada.shape
    tn = 1536
    return pl.pallas_call(
        _ada_kernel,
        grid=(depth, n // tn),
        in_specs=[
            pl.BlockSpec((8, d), lambda l, j: (0, 0)),
            pl.BlockSpec((1, d, tn), lambda l, j: (l, 0, j)),
            pl.BlockSpec((1, 1, tn), lambda l, j: (l, 0, j)),
        ],
        out_specs=pl.BlockSpec((1, 8, tn), lambda l, j: (l, 0, j)),
        out_shape=jax.ShapeDtypeStruct((depth, 8, n), F32),
        compiler_params=_params("parallel", "parallel"),
        name="ada",
    )(cv, w_ada, b_ada.reshape(depth, 1, n))


def _fmix_kernel(cc_ref, cs_ref, fw_ref, o_ref):
    fw = fw_ref[0]
    o_ref[0, 0] = jnp.dot(cc_ref[...], fw, preferred_element_type=F32, precision=HIGHEST).astype(BF16)
    o_ref[0, 1] = jnp.dot(cs_ref[...], fw, preferred_element_type=F32, precision=HIGHEST).astype(BF16)


def _fmix_call(cc, cs, fw_bd):
    depth, w, _ = fw_bd.shape
    return pl.pallas_call(
        _fmix_kernel,
        grid=(depth,),
        in_specs=[
            pl.BlockSpec((w, w), lambda l: (0, 0)),
            pl.BlockSpec((w, w), lambda l: (0, 0)),
            pl.BlockSpec((1, w, w), lambda l: (l, 0, 0)),
        ],
        out_specs=pl.BlockSpec((1, 2, w, w), lambda l: (l, 0, 0, 0)),
        out_shape=jax.ShapeDtypeStruct((depth, 2, w, w), BF16),
        compiler_params=_params("parallel"),
        name="fmix",
    )(cc, cs, fw_bd)


def _inproj_kernel(x_ref, xp_ref, xn_ref, sh_ref, sc_ref, w_ref, cw_ref, cb_ref, dtb_ref, alog_ref,
                   fm_ref, pw_ref, ps_ref,
                   z_ref, xs_ref, bc_ref, dt_ref, da_ref, pq_ref, yp_ref, *, tm, seq_len):
    i = pl.program_id(1)
    nt = pl.num_programs(1)
    sh = sh_ref[0]
    sc = sc_ref[0]

    def modulated(xv):
        return (_ln(xv) * (1.0 + sc) + sh).astype(BF16)

    x_ext = jnp.concatenate([xp_ref[0], x_ref[0], xn_ref[0]], axis=0)
    u_ext = jnp.dot(modulated(x_ext), w_ref[...], preferred_element_type=F32)
    u = u_ext[POOL_HALO:POOL_HALO + tm]
    up = jnp.where(i > 0, u_ext[0:POOL_HALO, C_XBC:C_DT], 0.0)
    un = jnp.where(i < nt - 1, u_ext[POOL_HALO + tm:, C_XBC:C_DT], 0.0)

    z_ref[0] = u[:, C_Z:C_XBC].astype(BF16)

    xbc = u[:, C_XBC:C_FNET]
    row = lax.broadcasted_iota(jnp.int32, (tm, 1), 0)
    prev_row = up[POOL_HALO - 1:POOL_HALO, 0:XBC_WIDTH]
    next_row = un[0:1, 0:XBC_WIDTH]
    xm1 = jnp.where(row == 0, prev_row, pltpu.roll(xbc, 1, 0))
    xp1 = jnp.where(row == tm - 1, next_row, pltpu.roll(xbc, tm - 1, 0))
    cw = cw_ref[...]
    conv = xm1 * cw[0:1] + xbc * cw[1:2] + xp1 * cw[2:3] + cb_ref[...]
    act = conv * jax.nn.sigmoid(conv)
    xs_ref[0] = act[:, 0:SSD_WIDTH].astype(BF16)
    bc_ref[0] = act[:, SSD_WIDTH:XBC_WIDTH].astype(BF16)

    lane = lax.broadcasted_iota(jnp.int32, (1, LANE), 1)
    for d in range(2):
        raw = u[:, C_DT + d * LANE:C_DT + (d + 1) * LANE] + dtb_ref[d:d + 1, :]
        sp = jnp.maximum(raw, 0.0) + jnp.log1p(jnp.exp(-jnp.abs(raw)))
        dt = jnp.where(lane < SSD_HEADS, sp, 0.0)
        dt_ref[0, d] = dt
        da_ref[0, d] = dt * (-jnp.exp(alog_ref[d:d + 1, :]))

    uf = u[:, C_FNET:C_POOL].astype(BF16)
    pq_ref[0, 0] = jnp.dot(uf, fm_ref[0], preferred_element_type=F32)
    pq_ref[0, 1] = jnp.dot(uf, fm_ref[1], preferred_element_type=F32)

    off_pool = C_POOL - C_XBC
    ext = jnp.concatenate([up[:, off_pool:off_pool + POOL_WIDTH], u[:, C_POOL:C_DT],
                           un[:, off_pool:off_pool + POOL_WIDTH]], axis=0)
    n_ext = tm + 2 * POOL_HALO
    sums = {}
    cur = ext
    width = 1
    while width < POOL_WINDOWS[-1]:
        cur = cur + pltpu.roll(cur, n_ext - width, 0)
        width *= 2
        sums[width] = cur
    lane_p = lax.broadcasted_iota(jnp.int32, (1, POOL_WIDTH), 1)
    pos = i * tm + lax.broadcasted_iota(jnp.int32, (tm, 1), 0)
    wsum = None
    cnt = None
    for gi, win in enumerate(POOL_WINDOWS):
        left = win // 2
        start = POOL_HALO - left
        s_w = sums[win]
        if start:
            s_w = pltpu.roll(s_w, n_ext - start, 0)
        s_w = s_w[0:tm]
        c_w = jnp.minimum(pos + (win - left), seq_len) - jnp.maximum(pos - left, 0)
        if wsum is None:
            wsum, cnt = s_w, c_w
        else:
            sel = lane_p >= gi * POOL_GDIM
            wsum = jnp.where(sel, s_w, wsum)
            cnt = jnp.where(sel, c_w, cnt)
    pooled = wsum / cnt.astype(F32) - u[:, C_POOL:C_DT]
    yp_ref[0] = (_bdot(pooled, pw_ref[...]) * ps_ref[...]).astype(BF16)


def _inproj_call(x, sh, sc, w_cat, conv_w, conv_b, dt_bias, a_log, fmix, pw, ps, *, tm):
    bsz, seq, d = x.shape
    nt = seq // tm
    nb8 = seq // 8
    tb = tm // 8
    mod_map = (lambda b, i: (b, 0, 0)) if sh.shape[0] > 1 else (lambda b, i: (0, 0, 0))

    def full(a):
        return pl.BlockSpec(a.shape, lambda b, i: (0,) * a.ndim)

    in_specs = [
        pl.BlockSpec((1, tm, d), lambda b, i: (b, i, 0)),
        pl.BlockSpec((1, 8, d), lambda b, i: (b, jnp.maximum(i * tb - 1, 0), 0)),
        pl.BlockSpec((1, 8, d), lambda b, i: (b, jnp.minimum((i + 1) * tb, nb8 - 1), 0)),
        pl.BlockSpec((1, 1, d), mod_map),
        pl.BlockSpec((1, 1, d), mod_map),
    ] + [full(a) for a in (w_cat, conv_w, conv_b, dt_bias, a_log, fmix, pw, ps)]
    out_shape = [
        jax.ShapeDtypeStruct((bsz, seq, SSD_WIDTH), BF16),
        jax.ShapeDtypeStruct((bsz, seq, SSD_WIDTH), BF16),
        jax.ShapeDtypeStruct((bsz, seq, BC_WIDTH), BF16),
        jax.ShapeDtypeStruct((bsz, 2, seq, LANE), F32),
        jax.ShapeDtypeStruct((bsz, 2, seq, LANE), F32),
        jax.ShapeDtypeStruct((bsz, 2, seq, FNET_WIDTH), F32),
        jax.ShapeDtypeStruct((bsz, seq, POOL_WIDTH), BF16),
    ]
    out_specs = [
        pl.BlockSpec((1, tm, SSD_WIDTH), lambda b, i: (b, i, 0)),
        pl.BlockSpec((1, tm, SSD_WIDTH), lambda b, i: (b, i, 0)),
        pl.BlockSpec((1, tm, BC_WIDTH), lambda b, i: (b, i, 0)),
        pl.BlockSpec((1, 2, tm, LANE), lambda b, i: (b, 0, i, 0)),
        pl.BlockSpec((1, 2, tm, LANE), lambda b, i: (b, 0, i, 0)),
        pl.BlockSpec((1, 2, tm, FNET_WIDTH), lambda b, i: (b, 0, i, 0)),
        pl.BlockSpec((1, tm, POOL_WIDTH), lambda b, i: (b, i, 0)),
    ]
    return pl.pallas_call(
        functools.partial(_inproj_kernel, tm=tm, seq_len=seq),
        grid=(bsz, nt),
        in_specs=in_specs,
        out_specs=out_specs,
        out_shape=out_shape,
        compiler_params=_params("parallel", "parallel"),
        name="inproj",
    )(x, x, x, sh, sc, w_cat, conv_w, conv_b, dt_bias, a_log, fmix, pw, ps)


def _bf16_pieces(x, terms):
    pieces = []
    rem = x
    for t in range(terms):
        piece = rem.astype(BF16)
        pieces.append(piece)
        if t + 1 < terms:
            rem = rem - piece.astype(F32)
    return pieces


def _ssd_chunk(d, k, xs_ref, bc_ref, dt_ref, da_ref, tri_ref, ex_ref, y_ref, st_ref):
    rows = slice(k * SSD_CHUNK, (k + 1) * SSD_CHUNK)
    tri = tri_ref[d]
    a = da_ref[0, 0, rows, :]
    a_parts = jnp.concatenate(_bf16_pieces(a, 3), axis=1)
    cs3 = jnp.dot(tri.astype(BF16), a_parts, preferred_element_type=F32)
    cs = cs3[:, 0:LANE] + cs3[:, LANE:2 * LANE] + cs3[:, 2 * LANE:3 * LANE]
    cs_t = cs.T
    mask = tri > 0.5
    stack = jnp.concatenate(_bf16_pieces(cs, 2) + _bf16_pieces(dt_ref[0, 0, rows, :], 2), axis=0)
    ex4 = jnp.dot(stack, ex_ref[...], preferred_element_type=F32)
    c = SSD_CHUNK
    cs_x = ex4[0:c] + ex4[c:2 * c]
    dt_x = ex4[2 * c:3 * c] + ex4[3 * c:4 * c]
    last = SSD_CHUNK - 1 if d == 0 else 0
    tot_x = cs_x[last:last + 1, :]
    e_cs = jnp.exp(cs_x)
    e_tot = jnp.exp(tot_x)
    xd = xs_ref[0, rows, :].astype(F32) * dt_x
    xd_dec = (xd * jnp.exp(tot_x - cs_x)).astype(BF16)
    xd = xd.astype(BF16)
    bc = bc_ref[0, rows, :]
    gn = SSD_GROUPS * SSD_STATE
    pw = 2 * SSD_HEADDIM
    gw = HEADS_PER_GROUP * SSD_HEADDIM
    low_half = lax.broadcasted_iota(jnp.int32, (1, pw), 1) < SSD_HEADDIM
    zero = jnp.zeros((SSD_CHUNK, pw), BF16)
    for g in range(SSD_GROUPS):
        bg_t = bc[:, g * SSD_STATE:(g + 1) * SSD_STATE].astype(F32).T.astype(BF16)
        cg = bc[:, gn + g * SSD_STATE:gn + (g + 1) * SSD_STATE]
        cb = jnp.dot(cg, bg_t, preferred_element_type=F32)
        glanes = slice(g * gw, (g + 1) * gw)
        st = st_ref[d, g]
        y_off = jnp.dot(cg, st.astype(BF16), preferred_element_type=F32) * e_cs[:, glanes]
        y_diag = []
        for pp in range(HEADS_PER_GROUP // 2):
            p = g * (HEADS_PER_GROUP // 2) + pp
            decay = []
            for h in (2 * p, 2 * p + 1):
                seg = cs[:, h:h + 1] - cs_t[h:h + 1, :]
                decay.append(cb * jnp.where(mask, jnp.exp(jnp.minimum(seg, 0.0)), 0.0))
            m = jnp.concatenate(decay, axis=1).astype(BF16)
            xp = xd[:, p * pw:(p + 1) * pw]
            x_bd = jnp.concatenate([jnp.where(low_half, xp, zero), jnp.where(low_half, zero, xp)], axis=0)
            y_diag.append(jnp.dot(m, x_bd, preferred_element_type=F32))
        y_ref[0, rows, glanes] = (jnp.concatenate(y_diag, axis=1) + y_off).astype(BF16)
        st_ref[d, g] = st * e_tot[:, glanes] + jnp.dot(bg_t, xd_dec[:, glanes], preferred_element_type=F32)


def _ssd_kernel(xsf_ref, bcf_ref, dtf_ref, daf_ref, xsb_ref, bcb_ref, dtb_ref, dab_ref, tri_ref, ex_ref, init_ref,
                yf_ref, yb_ref, fin_ref, st_ref, *, kc):
    s = pl.program_id(1)
    ns = pl.num_programs(1)

    @pl.when(s == 0)
    def _():
        st_ref[...] = init_ref[0]

    for k in range(kc):
        _ssd_chunk(0, k, xsf_ref, bcf_ref, dtf_ref, daf_ref, tri_ref, ex_ref, yf_ref, st_ref)
        _ssd_chunk(1, kc - 1 - k, xsb_ref, bcb_ref, dtb_ref, dab_ref, tri_ref, ex_ref, yb_ref, st_ref)

    @pl.when(s == ns - 1)
    def _():
        fin_ref[0] = st_ref[...]


def _ssd_call(xs, bc, dt, da, tri, ex, init):
    bsz, seq, _ = xs.shape
    kc = min(SSD_CHUNKS_PER_STEP, seq // SSD_CHUNK)
    rows = kc * SSD_CHUNK
    ns = seq // rows
    st_block = (1,) + SSD_STATE_SHAPE

    def tok(w, rev):
        return pl.BlockSpec((1, rows, w), lambda b, s: (b, ns - 1 - s if rev else s, 0))

    def per_dir(w, d):
        return pl.BlockSpec((1, 1, rows, w), lambda b, s: (b, d, ns - 1 - s if d else s, 0))

    return pl.pallas_call(
        functools.partial(_ssd_kernel, kc=kc),
        grid=(bsz, ns),
        in_specs=[
            tok(SSD_WIDTH, False), tok(BC_WIDTH, False), per_dir(LANE, 0), per_dir(LANE, 0),
            tok(SSD_WIDTH, True), tok(BC_WIDTH, True), per_dir(LANE, 1), per_dir(LANE, 1),
            pl.BlockSpec(tri.shape, lambda b, s: (0, 0, 0)),
            pl.BlockSpec(ex.shape, lambda b, s: (0, 0)),
            pl.BlockSpec(st_block, lambda b, s: (b, 0, 0, 0, 0)),
        ],
        out_specs=[tok(SSD_WIDTH, False), tok(SSD_WIDTH, True), pl.BlockSpec(st_block, lambda b, s: (b, 0, 0, 0, 0))],
        out_shape=[
            jax.ShapeDtypeStruct((bsz, seq, SSD_WIDTH), BF16),
            jax.ShapeDtypeStruct((bsz, seq, SSD_WIDTH), BF16),
            jax.ShapeDtypeStruct((bsz,) + SSD_STATE_SHAPE, F32),
        ],
        scratch_shapes=[pltpu.VMEM(SSD_STATE_SHAPE, F32)],
        compiler_params=_params("parallel", "arbitrary"),
        name="ssd",
    )(xs, bc, dt, da, xs, bc, dt, da, tri, ex, init)


def _dft1_kernel(p_ref, q_ref, m_ref, t_ref, *, n1):
    for j in range(DFT_NB):
        m = m_ref[j]
        rp = _bdot(m, p_ref[0, 0, :, j, :])
        rq = _bdot(m, q_ref[0, 0, :, j, :])
        t_ref[0, 0, j] = (rp[:n1] + rq[n1:]).astype(BF16)
        t_ref[0, 1, j] = (rp[n1:] - rq[:n1]).astype(BF16)


def _dft1_call(pq, mtab):
    bsz, _, seq, w = pq.shape
    n1 = seq // DFT_N2
    pqv = pq.reshape(bsz, 2, n1, DFT_N2, w)
    return pl.pallas_call(
        functools.partial(_dft1_kernel, n1=n1),
        grid=(bsz, DFT_N2 // DFT_NB),
        in_specs=[
            pl.BlockSpec((1, 1, n1, DFT_NB, w), lambda b, j: (b, 0, 0, j, 0)),
            pl.BlockSpec((1, 1, n1, DFT_NB, w), lambda b, j: (b, 1, 0, j, 0)),
            pl.BlockSpec((DFT_NB, 2 * n1, n1), lambda b, j: (j, 0, 0)),
        ],
        out_specs=pl.BlockSpec((1, 2, DFT_NB, n1, w), lambda b, j: (b, 0, j, 0, 0)),
        out_shape=jax.ShapeDtypeStruct((bsz, 2, DFT_N2, n1, w), BF16),
        compiler_params=_params("parallel", "parallel"),
        name="dft1",
    )(pqv, pqv, mtab)


def _dft2_kernel(d_ref, t_ref, o_ref):
    o_ref[0] = _bdot(d_ref[...], t_ref[0]).astype(BF16)


def _dft2_call(dmat, t, *, tn):
    bsz, k, n = t.shape
    mo = dmat.shape[0]
    return pl.pallas_call(
        _dft2_kernel,
        grid=(bsz, n // tn),
        in_specs=[
            pl.BlockSpec((mo, k), lambda b, j: (0, 0)),
            pl.BlockSpec((1, k, tn), lambda b, j: (b, 0, j)),
        ],
        out_specs=pl.BlockSpec((1, mo, tn), lambda b, j: (b, 0, j)),
        out_shape=jax.ShapeDtypeStruct((bsz, mo, n), BF16),
        compiler_params=_params("parallel", "parallel"),
        name="dft2",
    )(dmat, t)


def _mix_kernel(yf_ref, yb_ref, xs_ref, z_ref, yfn_ref, ypl_ref, x_ref, dsk_ref, nw_ref, wo_ref, g1_ref,
                lg_ref, lb_ref, sh2_ref, sc2_ref, x1_ref, h2_ref):
    y = yf_ref[0].astype(F32) + yb_ref[0].astype(F32) + xs_ref[0].astype(F32) * dsk_ref[...]
    z = z_ref[0].astype(F32)
    y = y * (z * jax.nn.sigmoid(z))
    gw = SSD_WIDTH // SSD_GROUPS
    parts = []
    for g in range(SSD_GROUPS):
        yg = y[:, g * gw:(g + 1) * gw]
        ms = jnp.mean(yg * yg, axis=-1, keepdims=True)
        parts.append(yg * lax.rsqrt(ms + EPS))
    yn = jnp.concatenate(parts, axis=-1) * nw_ref[...]
    o_f = SSD_WIDTH
    o_p = SSD_WIDTH + FNET_WIDTH
    mix = (_bdot(yn, wo_ref[0:o_f, :]) + _bdot(yfn_ref[0], wo_ref[o_f:o_p, :])
           + _bdot(ypl_ref[0], wo_ref[o_p:, :]))
    x1 = _ln(ALPHA * x_ref[0] + g1_ref[0] * mix) * lg_ref[...] + lb_ref[...]
    x1_ref[0] = x1
    h2_ref[0] = (_ln(x1) * (1.0 + sc2_ref[0]) + sh2_ref[0]).astype(BF16)


def _mix_call(yf, yb, xs, z, yfn, ypl, x, dsk, nw, wo, g1, lg, lb, sh2, sc2, *, tm):
    bsz, seq, d = x.shape
    mod_map = (lambda b, i: (b, 0, 0)) if g1.shape[0] > 1 else (lambda b, i: (0, 0, 0))

    def full(a):
        return pl.BlockSpec(a.shape, lambda b, i: (0,) * a.ndim)

    def tok(w):
        return pl.BlockSpec((1, tm, w), lambda b, i: (b, i, 0))

    mod = pl.BlockSpec((1, 1, d), mod_map)
    return pl.pallas_call(
        _mix_kernel,
        grid=(bsz, seq // tm),
        in_specs=[
            tok(SSD_WIDTH), tok(SSD_WIDTH), tok(SSD_WIDTH), tok(SSD_WIDTH), tok(FNET_WIDTH), tok(POOL_WIDTH), tok(d),
            full(dsk), full(nw), full(wo), mod, full(lg), full(lb), mod, mod,
        ],
        out_specs=[tok(d), tok(d)],
        out_shape=[jax.ShapeDtypeStruct((bsz, seq, d), F32), jax.ShapeDtypeStruct((bsz, seq, d), BF16)],
        compiler_params=_params("parallel", "parallel"),
        name="mix",
    )(yf, yb, xs, z, yfn, ypl, x, dsk, nw, wo, g1, lg, lb, sh2, sc2)


def _ffn_kernel(h_ref, hp_ref, hn_ref, wu_ref, cw_ref, cb_ref, wd_ref, x1_ref, g2_ref, lg_ref, lb_ref, o_ref,
                hbuf, a0, a1, hid, acc, *, tm, on_grid):
    i = pl.program_id(1)
    nt = pl.num_programs(1)
    nj = D_FF // FFN_CB
    ext = tm + 2 * FFN_HALO
    ra = ext // FFN_UP_CHUNKS
    rb = tm // FFN_ROW_CHUNKS
    slots = (a0, a1)
    base = FFN_PAD + FFN_HALO
    rows = (-1, 0, 1) if on_grid else (0,)

    def up_project(a_ref, j):
        for r in range(FFN_UP_CHUNKS):
            a = jnp.dot(hbuf[r * ra:(r + 1) * ra, :], wu_ref[j], preferred_element_type=F32)
            a_ref[FFN_PAD + r * ra:FFN_PAD + (r + 1) * ra, :] = a.astype(BF16)

    def conv(a_ref, j, r):
        cw = cw_ref[j].astype(BF16)
        m = FFN_PAD
        n = rb + 2 * m
        start = base + r * rb - m
        col = (lax.broadcasted_iota(jnp.int32, (n, 1), 0) + (r * rb - m)) & (GRID_W - 1)
        taps = {dr: a_ref[pl.ds(start + dr * GRID_W, n), :] for dr in rows}

        def column(dc, lo, hi):
            acc_dc = None
            for dr in rows:
                k = (dr + 1) * 3 + dc + 1
                term = taps[dr][lo:hi] * cw[k:k + 1]
                acc_dc = term if acc_dc is None else acc_dc + term
            return acc_dc

        left = column(-1, 0, n)
        right = column(1, 0, n)
        if on_grid:
            zero = jnp.zeros_like(left)
            left = jnp.where(col == GRID_W - 1, zero, left)
            right = jnp.where(col == 0, zero, right)
        left2 = jnp.concatenate([left[n - 2:], left[:n - 2]], axis=0)
        u = (left2 + right)[m + 1:m + 1 + rb]
        return column(0, m, m + rb) + u + cb_ref[j].astype(BF16)

    hbuf[0:FFN_HALO] = jnp.where(i > 0, hp_ref[0], jnp.zeros_like(hp_ref[0]))
    hbuf[FFN_HALO:FFN_HALO + tm] = h_ref[0]
    hbuf[FFN_HALO + tm:] = jnp.where(i < nt - 1, hn_ref[0], jnp.zeros_like(hn_ref[0]))
    for a in slots:
        a[0:FFN_PAD] = jnp.zeros((FFN_PAD, 2 * FFN_CB), BF16)
        a[FFN_PAD + ext:] = jnp.zeros((FFN_PAD, 2 * FFN_CB), BF16)

    up_project(slots[0], 0)
    for j in range(nj):
        if j + 1 < nj:
            up_project(slots[(j + 1) % 2], j + 1)
        half = j % 2
        for r in range(FFN_ROW_CHUNKS):
            c = conv(slots[half], j, r)
            hidden = jax.nn.gelu(c[:, FFN_CB:], approximate=True) * c[:, :FFN_CB]
            hid[r * rb:(r + 1) * rb, half * FFN_CB:(half + 1) * FFN_CB] = hidden
        if half == 1 or j == nj - 1:
            first = j - half
            k = (half + 1) * FFN_CB
            part = jnp.dot(hid[:, 0:k], wd_ref[first * FFN_CB:first * FFN_CB + k, :], preferred_element_type=F32)
            if first == 0:
                acc[...] = part
            else:
                acc[...] += part

    o_ref[0] = _ln(ALPHA * x1_ref[0] + g2_ref[0] * acc[...]) * lg_ref[...] + lb_ref[...]


def _ffn_call(h, w_up, conv_w9, conv_b, w_down, x1, g2, lg, lb, *, tm, on_grid):
    bsz, seq, d = h.shape
    nt = seq // tm
    th = tm // FFN_HALO
    nh = seq // FFN_HALO
    ext = tm + 2 * FFN_HALO
    mod_map = (lambda b, i: (b, 0, 0)) if g2.shape[0] > 1 else (lambda b, i: (0, 0, 0))

    def resident(a):
        return pl.BlockSpec(a.shape, lambda b, i: (0,) * a.ndim, pipeline_mode=pl.Buffered(1))

    a_scratch = pltpu.VMEM((ext + 2 * FFN_PAD, 2 * FFN_CB), BF16)
    return pl.pallas_call(
        functools.partial(_ffn_kernel, tm=tm, on_grid=on_grid),
        grid=(bsz, nt),
        in_specs=[
            pl.BlockSpec((1, tm, d), lambda b, i: (b, i, 0)),
            pl.BlockSpec((1, FFN_HALO, d), lambda b, i: (b, jnp.maximum(i * th - 1, 0), 0)),
            pl.BlockSpec((1, FFN_HALO, d), lambda b, i: (b, jnp.minimum((i + 1) * th, nh - 1), 0)),
            resident(w_up), resident(conv_w9), resident(conv_b), resident(w_down),
            pl.BlockSpec((1, tm, d), lambda b, i: (b, i, 0)),
            pl.BlockSpec((1, 1, d), mod_map),
            resident(lg), resident(lb),
        ],
        out_specs=pl.BlockSpec((1, tm, d), lambda b, i: (b, i, 0)),
        out_shape=jax.ShapeDtypeStruct((bsz, seq, d), F32),
        scratch_shapes=[pltpu.VMEM((ext, d), BF16), a_scratch, a_scratch, pltpu.VMEM((tm, 2 * FFN_CB), BF16),
                        pltpu.VMEM((tm, d), F32)],
        compiler_params=_params("parallel", "arbitrary"),
        name="ffn",
    )(h, h, h, w_up, conv_w9, conv_b, w_down, x1, g2, lg, lb)


def _block_diag(blocks):
    g, n, m = blocks.shape
    out = jnp.zeros((g * n, g * m), blocks.dtype)
    for k in range(g):
        out = out.at[k * n:(k + 1) * n, k * m:(k + 1) * m].set(blocks[k])
    return out


def _pair_blocks(w):
    rows = w.shape[0]
    halves = w.reshape(rows, 2, D_FF // FFN_CB, FFN_CB)
    return jnp.transpose(halves, (2, 0, 1, 3)).reshape(D_FF // FFN_CB, rows, 2 * FFN_CB)


def _dft_tables(seq):
    n1 = seq // DFT_N2
    k1 = np.arange(n1, dtype=np.int64)[:, None]
    nn1 = np.arange(n1, dtype=np.int64)[None, :]
    tabs = []
    for n2 in range(DFT_N2):
        th = 2.0 * np.pi * ((k1 * (DFT_N2 * nn1 + n2)) % seq) / seq
        tabs.append(np.concatenate([np.cos(th), -np.sin(th)], axis=0))
    mtab = np.stack(tabs)
    k2 = np.arange(DFT_N2, dtype=np.int64)[:, None]
    nn2 = np.arange(DFT_N2, dtype=np.int64)[None, :]
    th2 = 2.0 * np.pi * ((k2 * nn2) % DFT_N2) / DFT_N2
    scale = 1.0 / math.sqrt(seq * FNET_GDIM)
    dmat = scale * np.concatenate([np.cos(th2), np.sin(th2)], axis=1)
    return jnp.asarray(mtab, F32), jnp.asarray(dmat, F32)


def _dense_dft_table(seq):
    k = np.arange(seq, dtype=np.int64)
    th = 2.0 * np.pi * ((k[:, None] * k[None, :]) % seq) / seq
    scale = 1.0 / math.sqrt(seq * FNET_GDIM)
    return jnp.asarray(scale * np.concatenate([np.cos(th), -np.sin(th)], axis=1), F32)


def _channel_dft_tables():
    k = np.arange(FNET_GDIM, dtype=np.int64)
    th = 2.0 * np.pi * ((k[:, None] * k[None, :]) % FNET_GDIM) / FNET_GDIM
    eye = np.eye(FNET_GROUPS)
    return jnp.asarray(np.kron(eye, np.cos(th)), F32), jnp.asarray(np.kron(eye, np.sin(th)), F32)


def _scan_matrices():
    r = np.arange(SSD_CHUNK)
    lower = (r[:, None] >= r[None, :]).astype(np.float32)
    return jnp.asarray(np.stack([lower, lower.T]), F32)


def _head_expansion():
    ex = np.zeros((LANE, SSD_WIDTH), np.float32)
    for h in range(SSD_HEADS):
        ex[h, h * SSD_HEADDIM:(h + 1) * SSD_HEADDIM] = 1.0
    return jnp.asarray(ex, BF16)


def _token_mixer(x, sh1, sc1, lw, init, tri, ex, *, tm):
    bsz, seq, _ = x.shape
    z, xs, bc, dt, da, pq, ypl = _inproj_call(
        x, sh1, sc1, lw["w_cat"], lw["ssd_conv_w"], lw["ssd_conv_b"], lw["dt_bias"], lw["a_log"], lw["fmix"],
        lw["pool_bd"], lw["pool_scale"], tm=tm)
    yf, yb, fin = _ssd_call(xs, bc, dt, da, tri, ex, init)
    return z, xs, yf, yb, pq, ypl, fin


def _fourier_positions(pq):
    bsz, _, seq, w = pq.shape
    if seq % (DFT_N2 * 8) == 0:
        n1 = seq // DFT_N2
        mtab, dmat = _dft_tables(seq)
        t = _dft1_call(pq, mtab).reshape(bsz, 2 * DFT_N2, n1 * w)
        out = _dft2_call(dmat, t, tn=min(4096, n1 * w))
        return out.reshape(bsz, seq, w)
    return _dft2_call(_dense_dft_table(seq), pq.reshape(bsz, 2 * seq, w), tn=w)


def kernel(x, c, ctx, c_ctx, w_ada, b_ada, w_in, ssd_conv_w, ssd_conv_b, ssd_dt_bias, ssd_a_log, ssd_d,
           ssd_norm_w, fnet_w, pool_w, pool_scale, w_out, ln1_g, ln1_b, ffn_w_up, ffn_conv_w, ffn_conv_b,
           ffn_w_down, ln2_g, ln2_b):
    bsz, seq, d = x.shape
    n_ctx = ctx.shape[1]
    assert d == D_MODEL and bsz <= 7 and seq % (GRID_W * 8) == 0 and n_ctx % SSD_CHUNK == 0
    tm_lat = min(512, seq)
    tm_ffn = min(1024, seq)

    cv = jnp.zeros((8, d), F32).at[0:bsz].set(c).at[bsz].set(c_ctx)
    mods = _ada_call(cv, w_ada, b_ada)
    tri = _scan_matrices()
    ex = _head_expansion()
    fmix = _fmix_call(*_channel_dft_tables(), jnp.stack([_block_diag(fnet_w[l]) for l in range(DEPTH)]))
    zero_state = jnp.zeros((bsz,) + SSD_STATE_SHAPE, F32)
    lane_pad = jnp.zeros((d, LANE - SSD_HEADS), F32)

    for l in range(DEPTH):
        last = l == DEPTH - 1
        m = mods[l].reshape(8, 6, d)
        lat = [m[0:bsz, k][:, None, :] for k in range(6)]
        cx = [m[bsz:bsz + 1, k][:, None, :] for k in range(6)]
        w = w_in[l]
        w_cat = jnp.concatenate(
            [w[:, 0:OFF_DT], w[:, OFF_FNET:N_IN], w[:, OFF_DT:OFF_DT + SSD_HEADS], lane_pad,
             w[:, OFF_DT + SSD_HEADS:OFF_FNET], lane_pad], axis=1).astype(BF16)
        pad16 = ((0, 0), (0, LANE - SSD_HEADS))
        lw = {
            "w_cat": w_cat,
            "ssd_conv_w": ssd_conv_w[l],
            "ssd_conv_b": ssd_conv_b[l][None, :],
            "dt_bias": jnp.pad(ssd_dt_bias[l], pad16),
            "a_log": jnp.pad(ssd_a_log[l], pad16),
            "fmix": fmix[l],
            "pool_bd": _block_diag(pool_w[l]).astype(BF16),
            "pool_scale": pool_scale[l][None, :],
        }
        dsk = jnp.repeat(ssd_d[l], SSD_HEADDIM)[None, :]
        nw = ssd_norm_w[l][None, :]
        wo = w_out[l].astype(BF16)
        lg1, lb1 = ln1_g[l][None, :], ln1_b[l][None, :]
        lg2, lb2 = ln2_g[l][None, :], ln2_b[l][None, :]
        w_up = _pair_blocks(ffn_w_up[l]).astype(BF16)
        w_down = ffn_w_down[l].astype(BF16)
        conv9 = _pair_blocks(ffn_conv_w[l].reshape(9, 2 * D_FF))
        conv_b = _pair_blocks(ffn_conv_b[l][None, :])

        cz, cxs, cyf, cyb, cpq, cypl, c_fin = _token_mixer(ctx, cx[0], cx[1], lw, zero_state, tri, ex, tm=n_ctx)
        z, xs, yf, yb, pq, ypl, _ = _token_mixer(x, lat[0], lat[1], lw, c_fin, tri, ex, tm=tm_lat)

        yfn = _fourier_positions(pq)
        x1, h2 = _mix_call(yf, yb, xs, z, yfn, ypl, x, dsk, nw, wo, lat[2], lg1, lb1, lat[3], lat[4], tm=tm_lat)
        x = _ffn_call(h2, w_up, conv9, conv_b, w_down, x1, lat[5], lg2, lb2, tm=tm_ffn, on_grid=True)

        if not last:
            cyfn = _fourier_positions(cpq)
            c1, ch2 = _mix_call(cyf, cyb, cxs, cz, cyfn, cypl, ctx, dsk, nw, wo, cx[2], lg1, lb1, cx[3], cx[4],
                                tm=n_ctx)
            ctx = _ffn_call(ch2, w_up, conv9, conv_b, w_down, c1, cx[5], lg2, lb2, tm=n_ctx, on_grid=False)
    return x
```

```python
import functools
import math

import numpy as np
import jax
import jax.numpy as jnp
from jax import lax
from jax.experimental import pallas as pl
from jax.experimental.pallas import tpu as pltpu

F32 = jnp.float32
BF16 = jnp.bfloat16
HIGHEST = lax.Precision.HIGHEST

D_MODEL = 1024
DEPTH = 2
GRID_W = 64
SSD_WIDTH = 512
SSD_HEADDIM = 64
SSD_HEADS = 8
SSD_GROUPS = 2
HEADS_PER_GROUP = SSD_HEADS // SSD_GROUPS
SSD_STATE = 128
SSD_CHUNK = 128
SSD_CHUNKS_PER_STEP = 4
XBC_WIDTH = SSD_WIDTH + 2 * SSD_GROUPS * SSD_STATE
BC_WIDTH = 2 * SSD_GROUPS * SSD_STATE
SSD_STATE_SHAPE = (2, SSD_GROUPS, SSD_STATE, HEADS_PER_GROUP * SSD_HEADDIM)
FNET_WIDTH = 256
FNET_GROUPS = 4
FNET_GDIM = 64
POOL_WINDOWS = (2, 4, 8, 16)
POOL_WIDTH = 256
POOL_GDIM = 64
POOL_HALO = 8
D_FF = 2816
ALPHA = (2.0 * DEPTH) ** 0.25
EPS = 1e-6

OFF_XBC = SSD_WIDTH
OFF_DT = OFF_XBC + XBC_WIDTH
OFF_FNET = OFF_DT + 2 * SSD_HEADS
OFF_POOL = OFF_FNET + FNET_WIDTH
N_IN = OFF_POOL + POOL_WIDTH

LANE = 128
C_Z = 0
C_XBC = C_Z + SSD_WIDTH
C_FNET = C_XBC + XBC_WIDTH
C_POOL = C_FNET + FNET_WIDTH
C_DT = C_POOL + POOL_WIDTH
N_CAT = C_DT + 2 * LANE

DFT_N2 = 64
DFT_NB = 8
FFN_CB = 256
FFN_HALO = GRID_W
FFN_PAD = 16
FFN_ROW_CHUNKS = 4
FFN_UP_CHUNKS = 2
VMEM_LIMIT = 56 * 1024 * 1024


def _ln(x):
    mu = jnp.mean(x, axis=-1, keepdims=True)
    xc = x - mu
    var = jnp.mean(xc * xc, axis=-1, keepdims=True)
    return xc * lax.rsqrt(var + EPS)


def _bdot(a, b):
    return jnp.dot(a.astype(BF16), b.astype(BF16), preferred_element_type=F32)


def _params(*sem):
    return pltpu.CompilerParams(dimension_semantics=sem, vmem_limit_bytes=VMEM_LIMIT)


def _ada_kernel(c_ref, w_ref, b_ref, o_ref):
    cv = c_ref[...]
    s = cv * jax.nn.sigmoid(cv)
    o_ref[0] = jnp.dot(s, w_ref[0], preferred_element_type=F32, precision=HIGHEST) + b_ref[0]


def _ada_call(cv, w_ada, b_ada):
    depth, d, n = w_ada.shape
    tn = 1536
    return pl.pallas_call(
        _ada_kernel,
        grid=(depth, n // tn),
        in_specs=[
            pl.BlockSpec((8, d), lambda l, j: (0, 0)),
            pl.BlockSpec((1, d, tn), lambda l, j: (l, 0, j)),
            pl.BlockSpec((1, 1, tn), lambda l, j: (l, 0, j)),
        ],
        out_specs=pl.BlockSpec((1, 8, tn), lambda l, j: (l, 0, j)),
        out_shape=jax.ShapeDtypeStruct((depth, 8, n), F32),
        compiler_params=_params("parallel", "parallel"),
        name="ada",
    )(cv, w_ada, b_ada.reshape(depth, 1, n))


def _fmix_kernel(cc_ref, cs_ref, fw_ref, o_ref):
    fw = fw_ref[0]
    o_ref[0, 0] = jnp.dot(cc_ref[...], fw, preferred_element_type=F32, precision=HIGHEST).astype(BF16)
    o_ref[0, 1] = jnp.dot(cs_ref[...], fw, preferred_element_type=F32, precision=HIGHEST).astype(BF16)


def _fmix_call(cc, cs, fw_bd):
    depth, w, _ = fw_bd.shape
    return pl.pallas_call(
        _fmix_kernel,
        grid=(depth,),
        in_specs=[
            pl.BlockSpec((w, w), lambda l: (0, 0)),
            pl.BlockSpec((w, w), lambda l: (0, 0)),
            pl.BlockSpec((1, w, w), lambda l: (l, 0, 0)),
        ],
        out_specs=pl.BlockSpec((1, 2, w, w), lambda l: (l, 0, 0, 0)),
        out_shape=jax.ShapeDtypeStruct((depth, 2, w, w), BF16),
        compiler_params=_params("parallel"),
        name="fmix",
    )(cc, cs, fw_bd)


def _inproj_kernel(x_ref, xp_ref, xn_ref, sh_ref, sc_ref, w_ref, cw_ref, cb_ref, dtb_ref, alog_ref,
                   fm_ref, pw_ref, ps_ref,
                   z_ref, xs_ref, bc_ref, dt_ref, da_ref, pq_ref, yp_ref, *, tm, seq_len):
    i = pl.program_id(1)
    nt = pl.num_programs(1)
    sh = sh_ref[0]
    sc = sc_ref[0]

    def modulated(xv):
        return (_ln(xv) * (1.0 + sc) + sh).astype(BF16)

    x_ext = jnp.concatenate([xp_ref[0], x_ref[0], xn_ref[0]], axis=0)
    u_ext = jnp.dot(modulated(x_ext), w_ref[...], preferred_element_type=F32)
    u = u_ext[POOL_HALO:POOL_HALO + tm]
    up = jnp.where(i > 0, u_ext[0:POOL_HALO, C_XBC:C_DT], 0.0)
    un = jnp.where(i < nt - 1, u_ext[POOL_HALO + tm:, C_XBC:C_DT], 0.0)

    z_ref[0] = u[:, C_Z:C_XBC].astype(BF16)

    xbc = u[:, C_XBC:C_FNET]
    row = lax.broadcasted_iota(jnp.int32, (tm, 1), 0)
    prev_row = up[POOL_HALO - 1:POOL_HALO, 0:XBC_WIDTH]
    next_row = un[0:1, 0:XBC_WIDTH]
    xm1 = jnp.where(row == 0, prev_row, pltpu.roll(xbc, 1, 0))
    xp1 = jnp.where(row == tm - 1, next_row, pltpu.roll(xbc, tm - 1, 0))
    cw = cw_ref[...]
    conv = xm1 * cw[0:1] + xbc * cw[1:2] + xp1 * cw[2:3] + cb_ref[...]
    act = conv * jax.nn.sigmoid(conv)
    xs_ref[0] = act[:, 0:SSD_WIDTH].astype(BF16)
    bc_ref[0] = act[:, SSD_WIDTH:XBC_WIDTH].astype(BF16)

    lane = lax.broadcasted_iota(jnp.int32, (1, LANE), 1)
    for d in range(2):
        raw = u[:, C_DT + d * LANE:C_DT + (d + 1) * LANE] + dtb_ref[d:d + 1, :]
        sp = jnp.maximum(raw, 0.0) + jnp.log1p(jnp.exp(-jnp.abs(raw)))
        dt = jnp.where(lane < SSD_HEADS, sp, 0.0)
        dt_ref[0, d] = dt
        da_ref[0, d] = dt * (-jnp.exp(alog_ref[d:d + 1, :]))

    uf = u[:, C_FNET:C_POOL].astype(BF16)
    pq_ref[0, 0] = jnp.dot(uf, fm_ref[0], preferred_element_type=F32)
    pq_ref[0, 1] = jnp.dot(uf, fm_ref[1], preferred_element_type=F32)

    off_pool = C_POOL - C_XBC
    ext = jnp.concatenate([up[:, off_pool:off_pool + POOL_WIDTH], u[:, C_POOL:C_DT],
                           un[:, off_pool:off_pool + POOL_WIDTH]], axis=0)
    n_ext = tm + 2 * POOL_HALO
    sums = {}
    cur = ext
    width = 1
    while width < POOL_WINDOWS[-1]:
        cur = cur + pltpu.roll(cur, n_ext - width, 0)
        width *= 2
        sums[width] = cur
    lane_p = lax.broadcasted_iota(jnp.int32, (1, POOL_WIDTH), 1)
    pos = i * tm + lax.broadcasted_iota(jnp.int32, (tm, 1), 0)
    wsum = None
    cnt = None
    for gi, win in enumerate(POOL_WINDOWS):
        left = win // 2
        start = POOL_HALO - left
        s_w = sums[win]
        if start:
            s_w = pltpu.roll(s_w, n_ext - start, 0)
        s_w = s_w[0:tm]
        c_w = jnp.minimum(pos + (win - left), seq_len) - jnp.maximum(pos - left, 0)
        if wsum is None:
            wsum, cnt = s_w, c_w
        else:
            sel = lane_p >= gi * POOL_GDIM
            wsum = jnp.where(sel, s_w, wsum)
            cnt = jnp.where(sel, c_w, cnt)
    pooled = wsum / cnt.astype(F32) - u[:, C_POOL:C_DT]
    yp_ref[0] = (_bdot(pooled, pw_ref[...]) * ps_ref[...]).astype(BF16)


def _inproj_call(x, sh, sc, w_cat, conv_w, conv_b, dt_bias, a_log, fmix, pw, ps, *, tm):
    bsz, seq, d = x.shape
    nt = seq // tm
    nb8 = seq // 8
    tb = tm // 8
    mod_map = (lambda b, i: (b, 0, 0)) if sh.shape[0] > 1 else (lambda b, i: (0, 0, 0))

    def full(a):
        return pl.BlockSpec(a.shape, lambda b, i: (0,) * a.ndim)

    in_specs = [
        pl.BlockSpec((1, tm, d), lambda b, i: (b, i, 0)),
        pl.BlockSpec((1, 8, d), lambda b, i: (b, jnp.maximum(i * tb - 1, 0), 0)),
        pl.BlockSpec((1, 8, d), lambda b, i: (b, jnp.minimum((i + 1) * tb, nb8 - 1), 0)),
        pl.BlockSpec((1, 1, d), mod_map),
        pl.BlockSpec((1, 1, d), mod_map),
    ] + [full(a) for a in (w_cat, conv_w, conv_b, dt_bias, a_log, fmix, pw, ps)]
    out_shape = [
        jax.ShapeDtypeStruct((bsz, seq, SSD_WIDTH), BF16),
        jax.ShapeDtypeStruct((bsz, seq, SSD_WIDTH), BF16),
        jax.ShapeDtypeStruct((bsz, seq, BC_WIDTH), BF16),
        jax.ShapeDtypeStruct((bsz, 2, seq, LANE), F32),
        jax.ShapeDtypeStruct((bsz, 2, seq, LANE), F32),
        jax.ShapeDtypeStruct((bsz, 2, seq, FNET_WIDTH), F32),
        jax.ShapeDtypeStruct((bsz, seq, POOL_WIDTH), BF16),
    ]
    out_specs = [
        pl.BlockSpec((1, tm, SSD_WIDTH), lambda b, i: (b, i, 0)),
        pl.BlockSpec((1, tm, SSD_WIDTH), lambda b, i: (b, i, 0)),
        pl.BlockSpec((1, tm, BC_WIDTH), lambda b, i: (b, i, 0)),
        pl.BlockSpec((1, 2, tm, LANE), lambda b, i: (b, 0, i, 0)),
        pl.BlockSpec((1, 2, tm, LANE), lambda b, i: (b, 0, i, 0)),
        pl.BlockSpec((1, 2, tm, FNET_WIDTH), lambda b, i: (b, 0, i, 0)),
        pl.BlockSpec((1, tm, POOL_WIDTH), lambda b, i: (b, i, 0)),
    ]
    return pl.pallas_call(
        functools.partial(_inproj_kernel, tm=tm, seq_len=seq),
        grid=(bsz, nt),
        in_specs=in_specs,
        out_specs=out_specs,
        out_shape=out_shape,
        compiler_params=_params("parallel", "parallel"),
        name="inproj",
    )(x, x, x, sh, sc, w_cat, conv_w, conv_b, dt_bias, a_log, fmix, pw, ps)


def _bf16_pieces(x, terms):
    pieces = []
    rem = x
    for t in range(terms):
        piece = rem.astype(BF16)
        pieces.append(piece)
        if t + 1 < terms:
            rem = rem - piece.astype(F32)
    return pieces


def _ssd_kernel(xsf_ref, bcf_ref, dtf_ref, daf_ref, xsb_ref, bcb_ref, dtb_ref, dab_ref, tri_ref, ex_ref, init_ref,
                yf_ref, yb_ref, fin_ref, st_ref, *, kc):
    s = pl.program_id(1)
    ns = pl.num_programs(1)

    @pl.when(s == 0)
    def _():
        st_ref[...] = init_ref[0]

    fwd = (xsf_ref, bcf_ref, dtf_ref, daf_ref, yf_ref)
    bwd = (xsb_ref, bcb_ref, dtb_ref, dab_ref, yb_ref)
    jobs = [(0, k, fwd) for k in range(kc)] + [(1, kc - 1 - k, bwd) for k in range(kc)]
    c = SSD_CHUNK
    gn = SSD_GROUPS * SSD_STATE
    pw = 2 * SSD_HEADDIM
    gw = HEADS_PER_GROUP * SSD_HEADDIM
    low_half = lax.broadcasted_iota(jnp.int32, (1, pw), 1) < SSD_HEADDIM
    zero = jnp.zeros((SSD_CHUNK, pw), BF16)
    tri_b = [tri_ref[d].astype(BF16) for d in range(2)]
    mask = [tri_ref[d] > 0.5 for d in range(2)]

    def rows(k):
        return slice(k * c, (k + 1) * c)

    cs3 = [jnp.dot(tri_b[d], jnp.concatenate(_bf16_pieces(r[3][0, 0, rows(k), :], 3), axis=1),
                   preferred_element_type=F32) for d, k, r in jobs]
    cs = [v[:, 0:LANE] + v[:, LANE:2 * LANE] + v[:, 2 * LANE:3 * LANE] for v in cs3]
    cs_t = [v.T for v in cs]
    ex4 = [jnp.dot(jnp.concatenate(_bf16_pieces(cs[n], 2) + _bf16_pieces(r[2][0, 0, rows(k), :], 2), axis=0),
                   ex_ref[...], preferred_element_type=F32) for n, (d, k, r) in enumerate(jobs)]
    cs_x = [v[0:c] + v[c:2 * c] for v in ex4]
    dt_x = [v[2 * c:3 * c] + v[3 * c:4 * c] for v in ex4]
    tot_x = [cs_x[n][(c - 1 if d == 0 else 0):(c if d == 0 else 1), :] for n, (d, k, r) in enumerate(jobs)]
    e_cs = [jnp.exp(v) for v in cs_x]
    e_tot = [jnp.exp(v) for v in tot_x]
    xd32 = [r[0][0, rows(k), :].astype(F32) * dt_x[n] for n, (d, k, r) in enumerate(jobs)]
    xd_dec = [(xd32[n] * jnp.exp(tot_x[n] - cs_x[n])).astype(BF16) for n in range(len(jobs))]
    xd = [v.astype(BF16) for v in xd32]
    bc = [r[1][0, rows(k), :] for d, k, r in jobs]
    bg_t = [[v[:, g * SSD_STATE:(g + 1) * SSD_STATE].astype(F32).T.astype(BF16) for g in range(SSD_GROUPS)]
            for v in bc]
    cg = [[v[:, gn + g * SSD_STATE:gn + (g + 1) * SSD_STATE] for g in range(SSD_GROUPS)] for v in bc]
    cb = [[jnp.dot(cg[n][g], bg_t[n][g], preferred_element_type=F32) for g in range(SSD_GROUPS)]
          for n in range(len(jobs))]

    def y_diagonal(n, d, g):
        parts = []
        for pp in range(HEADS_PER_GROUP // 2):
            p = g * (HEADS_PER_GROUP // 2) + pp
            decay = []
            for h in (2 * p, 2 * p + 1):
                seg = cs[n][:, h:h + 1] - cs_t[n][h:h + 1, :]
                decay.append(cb[n][g] * jnp.where(mask[d], jnp.exp(jnp.minimum(seg, 0.0)), 0.0))
            m = jnp.concatenate(decay, axis=1).astype(BF16)
            xp = xd[n][:, p * pw:(p + 1) * pw]
            x_bd = jnp.concatenate([jnp.where(low_half, xp, zero), jnp.where(low_half, zero, xp)], axis=0)
            parts.append(jnp.dot(m, x_bd, preferred_element_type=F32))
        return jnp.concatenate(parts, axis=1)

    y_diag = [[y_diagonal(n, d, g) for g in range(SSD_GROUPS)] for n, (d, k, r) in enumerate(jobs)]

    for n, (d, k, r) in enumerate(jobs):
        for g in range(SSD_GROUPS):
            glanes = slice(g * gw, (g + 1) * gw)
            st = st_ref[d, g]
            y_off = jnp.dot(cg[n][g], st.astype(BF16), preferred_element_type=F32) * e_cs[n][:, glanes]
            r[4][0, rows(k), glanes] = (y_diag[n][g] + y_off).astype(BF16)
            st_ref[d, g] = st * e_tot[n][:, glanes] + jnp.dot(bg_t[n][g], xd_dec[n][:, glanes],
                                                              preferred_element_type=F32)

    @pl.when(s == ns - 1)
    def _():
        fin_ref[0] = st_ref[...]


def _ssd_call(xs, bc, dt, da, tri, ex, init):
    bsz, seq, _ = xs.shape
    kc = min(SSD_CHUNKS_PER_STEP, seq // SSD_CHUNK)
    rows = kc * SSD_CHUNK
    ns = seq // rows
    st_block = (1,) + SSD_STATE_SHAPE

    def tok(w, rev):
        return pl.BlockSpec((1, rows, w), lambda b, s: (b, ns - 1 - s if rev else s, 0))

    def per_dir(w, d):
        return pl.BlockSpec((1, 1, rows, w), lambda b, s: (b, d, ns - 1 - s if d else s, 0))

    return pl.pallas_call(
        functools.partial(_ssd_kernel, kc=kc),
        grid=(bsz, ns),
        in_specs=[
            tok(SSD_WIDTH, False), tok(BC_WIDTH, False), per_dir(LANE, 0), per_dir(LANE, 0),
            tok(SSD_WIDTH, True), tok(BC_WIDTH, True), per_dir(LANE, 1), per_dir(LANE, 1),
            pl.BlockSpec(tri.shape, lambda b, s: (0, 0, 0)),
            pl.BlockSpec(ex.shape, lambda b, s: (0, 0)),
            pl.BlockSpec(st_block, lambda b, s: (b, 0, 0, 0, 0)),
        ],
        out_specs=[tok(SSD_WIDTH, False), tok(SSD_WIDTH, True), pl.BlockSpec(st_block, lambda b, s: (b, 0, 0, 0, 0))],
        out_shape=[
            jax.ShapeDtypeStruct((bsz, seq, SSD_WIDTH), BF16),
            jax.ShapeDtypeStruct((bsz, seq, SSD_WIDTH), BF16),
            jax.ShapeDtypeStruct((bsz,) + SSD_STATE_SHAPE, F32),
        ],
        scratch_shapes=[pltpu.VMEM(SSD_STATE_SHAPE, F32)],
        compiler_params=_params("parallel", "arbitrary"),
        name="ssd",
    )(xs, bc, dt, da, xs, bc, dt, da, tri, ex, init)


def _dft1_kernel(p_ref, q_ref, m_ref, t_ref, *, n1):
    for j in range(DFT_NB):
        m = m_ref[j]
        rp = _bdot(m, p_ref[0, 0, :, j, :])
        rq = _bdot(m, q_ref[0, 0, :, j, :])
        t_ref[0, 0, j] = (rp[:n1] + rq[n1:]).astype(BF16)
        t_ref[0, 1, j] = (rp[n1:] - rq[:n1]).astype(BF16)


def _dft1_call(pq, mtab):
    bsz, _, seq, w = pq.shape
    n1 = seq // DFT_N2
    pqv = pq.reshape(bsz, 2, n1, DFT_N2, w)
    return pl.pallas_call(
        functools.partial(_dft1_kernel, n1=n1),
        grid=(bsz, DFT_N2 // DFT_NB),
        in_specs=[
            pl.BlockSpec((1, 1, n1, DFT_NB, w), lambda b, j: (b, 0, 0, j, 0)),
            pl.BlockSpec((1, 1, n1, DFT_NB, w), lambda b, j: (b, 1, 0, j, 0)),
            pl.BlockSpec((DFT_NB, 2 * n1, n1), lambda b, j: (j, 0, 0)),
        ],
        out_specs=pl.BlockSpec((1, 2, DFT_NB, n1, w), lambda b, j: (b, 0, j, 0, 0)),
        out_shape=jax.ShapeDtypeStruct((bsz, 2, DFT_N2, n1, w), BF16),
        compiler_params=_params("parallel", "parallel"),
        name="dft1",
    )(pqv, pqv, mtab)


def _dft2_kernel(d_ref, t_ref, o_ref):
    o_ref[0] = _bdot(d_ref[...], t_ref[0]).astype(BF16)


def _dft2_call(dmat, t, *, tn):
    bsz, k, n = t.shape
    mo = dmat.shape[0]
    return pl.pallas_call(
        _dft2_kernel,
        grid=(bsz, n // tn),
        in_specs=[
            pl.BlockSpec((mo, k), lambda b, j: (0, 0)),
            pl.BlockSpec((1, k, tn), lambda b, j: (b, 0, j)),
        ],
        out_specs=pl.BlockSpec((1, mo, tn), lambda b, j: (b, 0, j)),
        out_shape=jax.ShapeDtypeStruct((bsz, mo, n), BF16),
        compiler_params=_params("parallel", "parallel"),
        name="dft2",
    )(dmat, t)


def _mix_kernel(yf_ref, yb_ref, xs_ref, z_ref, yfn_ref, ypl_ref, x_ref, dsk_ref, nw_ref, wo_ref, g1_ref,
                lg_ref, lb_ref, sh2_ref, sc2_ref, x1_ref, h2_ref):
    y = yf_ref[0].astype(F32) + yb_ref[0].astype(F32) + xs_ref[0].astype(F32) * dsk_ref[...]
    z = z_ref[0].astype(F32)
    y = y * (z * jax.nn.sigmoid(z))
    gw = SSD_WIDTH // SSD_GROUPS
    parts = []
    for g in range(SSD_GROUPS):
        yg = y[:, g * gw:(g + 1) * gw]
        ms = jnp.mean(yg * yg, axis=-1, keepdims=True)
        parts.append(yg * lax.rsqrt(ms + EPS))
    yn = jnp.concatenate(parts, axis=-1) * nw_ref[...]
    o_f = SSD_WIDTH
    o_p = SSD_WIDTH + FNET_WIDTH
    mix = (_bdot(yn, wo_ref[0:o_f, :]) + _bdot(yfn_ref[0], wo_ref[o_f:o_p, :])
           + _bdot(ypl_ref[0], wo_ref[o_p:, :]))
    x1 = _ln(ALPHA * x_ref[0] + g1_ref[0] * mix) * lg_ref[...] + lb_ref[...]
    x1_ref[0] = x1
    h2_ref[0] = (_ln(x1) * (1.0 + sc2_ref[0]) + sh2_ref[0]).astype(BF16)


def _mix_call(yf, yb, xs, z, yfn, ypl, x, dsk, nw, wo, g1, lg, lb, sh2, sc2, *, tm):
    bsz, seq, d = x.shape
    mod_map = (lambda b, i: (b, 0, 0)) if g1.shape[0] > 1 else (lambda b, i: (0, 0, 0))

    def full(a):
        return pl.BlockSpec(a.shape, lambda b, i: (0,) * a.ndim)

    def tok(w):
        return pl.BlockSpec((1, tm, w), lambda b, i: (b, i, 0))

    mod = pl.BlockSpec((1, 1, d), mod_map)
    return pl.pallas_call(
        _mix_kernel,
        grid=(bsz, seq // tm),
        in_specs=[
            tok(SSD_WIDTH), tok(SSD_WIDTH), tok(SSD_WIDTH), tok(SSD_WIDTH), tok(FNET_WIDTH), tok(POOL_WIDTH), tok(d),
            full(dsk), full(nw), full(wo), mod, full(lg), full(lb), mod, mod,
        ],
        out_specs=[tok(d), tok(d)],
        out_shape=[jax.ShapeDtypeStruct((bsz, seq, d), F32), jax.ShapeDtypeStruct((bsz, seq, d), BF16)],
        compiler_params=_params("parallel", "parallel"),
        name="mix",
    )(yf, yb, xs, z, yfn, ypl, x, dsk, nw, wo, g1, lg, lb, sh2, sc2)


def _ffn_kernel(h_ref, hp_ref, hn_ref, wu_ref, cw_ref, cb_ref, wd_ref, x1_ref, g2_ref, lg_ref, lb_ref, o_ref,
                hbuf, a0, a1, hid, acc, *, tm, on_grid):
    i = pl.program_id(1)
    nt = pl.num_programs(1)
    nj = D_FF // FFN_CB
    ext = tm + 2 * FFN_HALO
    ra = ext // FFN_UP_CHUNKS
    rb = tm // FFN_ROW_CHUNKS
    slots = (a0, a1)
    base = FFN_PAD + FFN_HALO
    rows = (-1, 0, 1) if on_grid else (0,)

    def up_project(a_ref, j):
        for r in range(FFN_UP_CHUNKS):
            a = jnp.dot(hbuf[r * ra:(r + 1) * ra, :], wu_ref[j], preferred_element_type=F32)
            a_ref[FFN_PAD + r * ra:FFN_PAD + (r + 1) * ra, :] = a.astype(BF16)

    def conv(a_ref, j, r):
        cw = cw_ref[j].astype(BF16)
        m = FFN_PAD
        n = rb + 2 * m
        start = base + r * rb - m
        col = (lax.broadcasted_iota(jnp.int32, (n, 1), 0) + (r * rb - m)) & (GRID_W - 1)
        taps = {dr: a_ref[pl.ds(start + dr * GRID_W, n), :] for dr in rows}

        def column(dc, lo, hi):
            acc_dc = None
            for dr in rows:
                k = (dr + 1) * 3 + dc + 1
                term = taps[dr][lo:hi] * cw[k:k + 1]
                acc_dc = term if acc_dc is None else acc_dc + term
            return acc_dc

        left = column(-1, 0, n)
        right = column(1, 0, n)
        if on_grid:
            zero = jnp.zeros_like(left)
            left = jnp.where(col == GRID_W - 1, zero, left)
            right = jnp.where(col == 0, zero, right)
        left2 = jnp.concatenate([left[n - 2:], left[:n - 2]], axis=0)
        u = (left2 + right)[m + 1:m + 1 + rb]
        return column(0, m, m + rb) + u + cb_ref[j].astype(BF16)

    hbuf[0:FFN_HALO] = jnp.where(i > 0, hp_ref[0], jnp.zeros_like(hp_ref[0]))
    hbuf[FFN_HALO:FFN_HALO + tm] = h_ref[0]
    hbuf[FFN_HALO + tm:] = jnp.where(i < nt - 1, hn_ref[0], jnp.zeros_like(hn_ref[0]))
    for a in slots:
        a[0:FFN_PAD] = jnp.zeros((FFN_PAD, 2 * FFN_CB), BF16)
        a[FFN_PAD + ext:] = jnp.zeros((FFN_PAD, 2 * FFN_CB), BF16)

    up_project(slots[0], 0)
    for j in range(nj):
        if j + 1 < nj:
            up_project(slots[(j + 1) % 2], j + 1)
        half = j % 2
        for r in range(FFN_ROW_CHUNKS):
            c = conv(slots[half], j, r)
            hidden = jax.nn.gelu(c[:, FFN_CB:], approximate=True) * c[:, :FFN_CB]
            hid[r * rb:(r + 1) * rb, half * FFN_CB:(half + 1) * FFN_CB] = hidden
        if half == 1 or j == nj - 1:
            first = j - half
            k = (half + 1) * FFN_CB
            part = jnp.dot(hid[:, 0:k], wd_ref[first * FFN_CB:first * FFN_CB + k, :], preferred_element_type=F32)
            if first == 0:
                acc[...] = part
            else:
                acc[...] += part

    o_ref[0] = _ln(ALPHA * x1_ref[0] + g2_ref[0] * acc[...]) * lg_ref[...] + lb_ref[...]


def _ffn_call(h, w_up, conv_w9, conv_b, w_down, x1, g2, lg, lb, *, tm, on_grid):
    bsz, seq, d = h.shape
    nt = seq // tm
    th = tm // FFN_HALO
    nh = seq // FFN_HALO
    ext = tm + 2 * FFN_HALO
    mod_map = (lambda b, i: (b, 0, 0)) if g2.shape[0] > 1 else (lambda b, i: (0, 0, 0))

    def resident(a):
        return pl.BlockSpec(a.shape, lambda b, i: (0,) * a.ndim, pipeline_mode=pl.Buffered(1))

    a_scratch = pltpu.VMEM((ext + 2 * FFN_PAD, 2 * FFN_CB), BF16)
    return pl.pallas_call(
        functools.partial(_ffn_kernel, tm=tm, on_grid=on_grid),
        grid=(bsz, nt),
        in_specs=[
            pl.BlockSpec((1, tm, d), lambda b, i: (b, i, 0)),
            pl.BlockSpec((1, FFN_HALO, d), lambda b, i: (b, jnp.maximum(i * th - 1, 0), 0)),
            pl.BlockSpec((1, FFN_HALO, d), lambda b, i: (b, jnp.minimum((i + 1) * th, nh - 1), 0)),
            resident(w_up), resident(conv_w9), resident(conv_b), resident(w_down),
            pl.BlockSpec((1, tm, d), lambda b, i: (b, i, 0)),
            pl.BlockSpec((1, 1, d), mod_map),
            resident(lg), resident(lb),
        ],
        out_specs=pl.BlockSpec((1, tm, d), lambda b, i: (b, i, 0)),
        out_shape=jax.ShapeDtypeStruct((bsz, seq, d), F32),
        scratch_shapes=[pltpu.VMEM((ext, d), BF16), a_scratch, a_scratch, pltpu.VMEM((tm, 2 * FFN_CB), BF16),
                        pltpu.VMEM((tm, d), F32)],
        compiler_params=_params("parallel", "arbitrary"),
        name="ffn",
    )(h, h, h, w_up, conv_w9, conv_b, w_down, x1, g2, lg, lb)


def _block_diag(blocks):
    g, n, m = blocks.shape
    out = jnp.zeros((g * n, g * m), blocks.dtype)
    for k in range(g):
        out = out.at[k * n:(k + 1) * n, k * m:(k + 1) * m].set(blocks[k])
    return out


def _pair_blocks(w):
    rows = w.shape[0]
    halves = w.reshape(rows, 2, D_FF // FFN_CB, FFN_CB)
    return jnp.transpose(halves, (2, 0, 1, 3)).reshape(D_FF // FFN_CB, rows, 2 * FFN_CB)


def _dft_tables(seq):
    n1 = seq // DFT_N2
    k1 = np.arange(n1, dtype=np.int64)[:, None]
    nn1 = np.arange(n1, dtype=np.int64)[None, :]
    tabs = []
    for n2 in range(DFT_N2):
        th = 2.0 * np.pi * ((k1 * (DFT_N2 * nn1 + n2)) % seq) / seq
        tabs.append(np.concatenate([np.cos(th), -np.sin(th)], axis=0))
    mtab = np.stack(tabs)
    k2 = np.arange(DFT_N2, dtype=np.int64)[:, None]
    nn2 = np.arange(DFT_N2, dtype=np.int64)[None, :]
    th2 = 2.0 * np.pi * ((k2 * nn2) % DFT_N2) / DFT_N2
    scale = 1.0 / math.sqrt(seq * FNET_GDIM)
    dmat = scale * np.concatenate([np.cos(th2), np.sin(th2)], axis=1)
    return jnp.asarray(mtab, F32), jnp.asarray(dmat, F32)


def _dense_dft_table(seq):
    k = np.arange(seq, dtype=np.int64)
    th = 2.0 * np.pi * ((k[:, None] * k[None, :]) % seq) / seq
    scale = 1.0 / math.sqrt(seq * FNET_GDIM)
    return jnp.asarray(scale * np.concatenate([np.cos(th), -np.sin(th)], axis=1), F32)


def _channel_dft_tables():
    k = np.arange(FNET_GDIM, dtype=np.int64)
    th = 2.0 * np.pi * ((k[:, None] * k[None, :]) % FNET_GDIM) / FNET_GDIM
    eye = np.eye(FNET_GROUPS)
    return jnp.asarray(np.kron(eye, np.cos(th)), F32), jnp.asarray(np.kron(eye, np.sin(th)), F32)


def _scan_matrices():
    r = np.arange(SSD_CHUNK)
    lower = (r[:, None] >= r[None, :]).astype(np.float32)
    return jnp.asarray(np.stack([lower, lower.T]), F32)


def _head_expansion():
    ex = np.zeros((LANE, SSD_WIDTH), np.float32)
    for h in range(SSD_HEADS):
        ex[h, h * SSD_HEADDIM:(h + 1) * SSD_HEADDIM] = 1.0
    return jnp.asarray(ex, BF16)


def _token_mixer(x, sh1, sc1, lw, init, tri, ex, *, tm):
    bsz, seq, _ = x.shape
    z, xs, bc, dt, da, pq, ypl = _inproj_call(
        x, sh1, sc1, lw["w_cat"], lw["ssd_conv_w"], lw["ssd_conv_b"], lw["dt_bias"], lw["a_log"], lw["fmix"],
        lw["pool_bd"], lw["pool_scale"], tm=tm)
    yf, yb, fin = _ssd_call(xs, bc, dt, da, tri, ex, init)
    return z, xs, yf, yb, pq, ypl, fin


def _fourier_positions(pq):
    bsz, _, seq, w = pq.shape
    if seq % (DFT_N2 * 8) == 0:
        n1 = seq // DFT_N2
        mtab, dmat = _dft_tables(seq)
        t = _dft1_call(pq, mtab).reshape(bsz, 2 * DFT_N2, n1 * w)
        out = _dft2_call(dmat, t, tn=min(4096, n1 * w))
        return out.reshape(bsz, seq, w)
    return _dft2_call(_dense_dft_table(seq), pq.reshape(bsz, 2 * seq, w), tn=w)


def kernel(x, c, ctx, c_ctx, w_ada, b_ada, w_in, ssd_conv_w, ssd_conv_b, ssd_dt_bias, ssd_a_log, ssd_d,
           ssd_norm_w, fnet_w, pool_w, pool_scale, w_out, ln1_g, ln1_b, ffn_w_up, ffn_conv_w, ffn_conv_b,
           ffn_w_down, ln2_g, ln2_b):
    bsz, seq, d = x.shape
    n_ctx = ctx.shape[1]
    assert d == D_MODEL and bsz <= 7 and seq % (GRID_W * 8) == 0 and n_ctx % SSD_CHUNK == 0
    tm_lat = min(512, seq)
    tm_ffn = min(1024, seq)

    cv = jnp.zeros((8, d), F32).at[0:bsz].set(c).at[bsz].set(c_ctx)
    mods = _ada_call(cv, w_ada, b_ada)
    tri = _scan_matrices()
    ex = _head_expansion()
    fmix = _fmix_call(*_channel_dft_tables(), jnp.stack([_block_diag(fnet_w[l]) for l in range(DEPTH)]))
    zero_state = jnp.zeros((bsz,) + SSD_STATE_SHAPE, F32)
    lane_pad = jnp.zeros((d, LANE - SSD_HEADS), F32)

    for l in range(DEPTH):
        last = l == DEPTH - 1
        m = mods[l].reshape(8, 6, d)
        lat = [m[0:bsz, k][:, None, :] for k in range(6)]
        cx = [m[bsz:bsz + 1, k][:, None, :] for k in range(6)]
        w = w_in[l]
        w_cat = jnp.concatenate(
            [w[:, 0:OFF_DT], w[:, OFF_FNET:N_IN], w[:, OFF_DT:OFF_DT + SSD_HEADS], lane_pad,
             w[:, OFF_DT + SSD_HEADS:OFF_FNET], lane_pad], axis=1).astype(BF16)
        pad16 = ((0, 0), (0, LANE - SSD_HEADS))
        lw = {
            "w_cat": w_cat,
            "ssd_conv_w": ssd_conv_w[l],
            "ssd_conv_b": ssd_conv_b[l][None, :],
            "dt_bias": jnp.pad(ssd_dt_bias[l], pad16),
            "a_log": jnp.pad(ssd_a_log[l], pad16),
            "fmix": fmix[l],
            "pool_bd": _block_diag(pool_w[l]).astype(BF16),
            "pool_scale": pool_scale[l][None, :],
        }
        dsk = jnp.repeat(ssd_d[l], SSD_HEADDIM)[None, :]
        nw = ssd_norm_w[l][None, :]
        wo = w_out[l].astype(BF16)
        lg1, lb1 = ln1_g[l][None, :], ln1_b[l][None, :]
        lg2, lb2 = ln2_g[l][None, :], ln2_b[l][None, :]
        w_up = _pair_blocks(ffn_w_up[l]).astype(BF16)
        w_down = ffn_w_down[l].astype(BF16)
        conv9 = _pair_blocks(ffn_conv_w[l].reshape(9, 2 * D_FF))
        conv_b = _pair_blocks(ffn_conv_b[l][None, :])

        cz, cxs, cyf, cyb, cpq, cypl, c_fin = _token_mixer(ctx, cx[0], cx[1], lw, zero_state, tri, ex, tm=n_ctx)
        z, xs, yf, yb, pq, ypl, _ = _token_mixer(x, lat[0], lat[1], lw, c_fin, tri, ex, tm=tm_lat)

        yfn = _fourier_positions(pq)
        x1, h2 = _mix_call(yf, yb, xs, z, yfn, ypl, x, dsk, nw, wo, lat[2], lg1, lb1, lat[3], lat[4], tm=tm_lat)
        x = _ffn_call(h2, w_up, conv9, conv_b, w_down, x1, lat[5], lg2, lb2, tm=tm_ffn, on_grid=True)

        if not last:
            cyfn = _fourier_positions(cpq)
            c1, ch2 = _mix_call(cyf, cyb, cxs, cz, cyfn, cypl, ctx, dsk, nw, wo, cx[2], lg1, lb1, cx[3], cx[4],
                                tm=n_ctx)
            ctx = _ffn_call(ch2, w_up, conv9, conv_b, w_down, c1, cx[5], lg2, lb2, tm=n_ctx, on_grid=False)
    return x
```

```python
import functools
import math

import numpy as np
import jax
import jax.numpy as jnp
from jax import lax
from jax.experimental import pallas as pl
from jax.experimental.pallas import tpu as pltpu

F32 = jnp.float32
BF16 = jnp.bfloat16
HIGHEST = lax.Precision.HIGHEST

D_MODEL = 1024
DEPTH = 2
GRID_W = 64
SSD_WIDTH = 512
SSD_HEADDIM = 64
SSD_HEADS = 8
SSD_GROUPS = 2
HEADS_PER_GROUP = SSD_HEADS // SSD_GROUPS
SSD_STATE = 128
SSD_CHUNK = 128
SSD_CHUNKS_PER_STEP = 8
XBC_WIDTH = SSD_WIDTH + 2 * SSD_GROUPS * SSD_STATE
BC_WIDTH = 2 * SSD_GROUPS * SSD_STATE
SSD_STATE_SHAPE = (2, SSD_GROUPS, SSD_STATE, HEADS_PER_GROUP * SSD_HEADDIM)
FNET_WIDTH = 256
FNET_GROUPS = 4
FNET_GDIM = 64
POOL_WINDOWS = (2, 4, 8, 16)
POOL_WIDTH = 256
POOL_GDIM = 64
POOL_HALO = 8
D_FF = 2816
ALPHA = (2.0 * DEPTH) ** 0.25
EPS = 1e-6

OFF_XBC = SSD_WIDTH
OFF_DT = OFF_XBC + XBC_WIDTH
OFF_FNET = OFF_DT + 2 * SSD_HEADS
OFF_POOL = OFF_FNET + FNET_WIDTH
N_IN = OFF_POOL + POOL_WIDTH

LANE = 128
C_Z = 0
C_XBC = C_Z + SSD_WIDTH
C_FNET = C_XBC + XBC_WIDTH
C_POOL = C_FNET + FNET_WIDTH
C_DT = C_POOL + POOL_WIDTH
N_CAT = C_DT + 2 * LANE

DFT_N2 = 64
DFT_NB = 8
FFN_CB = 256
FFN_HALO = GRID_W
FFN_PAD = 16
FFN_ROW_CHUNKS = 2
FFN_UP_CHUNKS = 1
VMEM_LIMIT = 56 * 1024 * 1024


def _ln(x):
    mu = jnp.mean(x, axis=-1, keepdims=True)
    xc = x - mu
    var = jnp.mean(xc * xc, axis=-1, keepdims=True)
    return xc * lax.rsqrt(var + EPS)


def _bdot(a, b):
    return jnp.dot(a.astype(BF16), b.astype(BF16), preferred_element_type=F32)


def _params(*sem):
    return pltpu.CompilerParams(dimension_semantics=sem, vmem_limit_bytes=VMEM_LIMIT)


def _ada_kernel(c_ref, w_ref, b_ref, o_ref):
    cv = c_ref[...]
    s = cv * jax.nn.sigmoid(cv)
    o_ref[0] = jnp.dot(s, w_ref[0], preferred_element_type=F32, precision=HIGHEST) + b_ref[0]


def _ada_call(cv, w_ada, b_ada):
    depth, d, n = w_ada.shape
    tn = 1536
    return pl.pallas_call(
        _ada_kernel,
        grid=(depth, n // tn),
        in_specs=[
            pl.BlockSpec((8, d), lambda l, j: (0, 0)),
            pl.BlockSpec((1, d, tn), lambda l, j: (l, 0, j)),
            pl.BlockSpec((1, 1, tn), lambda l, j: (l, 0, j)),
        ],
        out_specs=pl.BlockSpec((1, 8, tn), lambda l, j: (l, 0, j)),
        out_shape=jax.ShapeDtypeStruct((depth, 8, n), F32),
        compiler_params=_params("parallel", "parallel"),
        name="ada",
    )(cv, w_ada, b_ada.reshape(depth, 1, n))


def _fmix_kernel(cc_ref, cs_ref, fw_ref, o_ref):
    fw = fw_ref[0]
    o_ref[0, 0] = jnp.dot(cc_ref[...], fw, preferred_element_type=F32, precision=HIGHEST).astype(BF16)
    o_ref[0, 1] = jnp.dot(cs_ref[...], fw, preferred_element_type=F32, precision=HIGHEST).astype(BF16)


def _fmix_call(cc, cs, fw_bd):
    depth, w, _ = fw_bd.shape
    return pl.pallas_call(
        _fmix_kernel,
        grid=(depth,),
        in_specs=[
            pl.BlockSpec((w, w), lambda l: (0, 0)),
            pl.BlockSpec((w, w), lambda l: (0, 0)),
            pl.BlockSpec((1, w, w), lambda l: (l, 0, 0)),
        ],
        out_specs=pl.BlockSpec((1, 2, w, w), lambda l: (l, 0, 0, 0)),
        out_shape=jax.ShapeDtypeStruct((depth, 2, w, w), BF16),
        compiler_params=_params("parallel"),
        name="fmix",
    )(cc, cs, fw_bd)


def _inproj_kernel(x_ref, xp_ref, xn_ref, sh_ref, sc_ref, w_ref, cw_ref, cb_ref, dtb_ref, alog_ref,
                   fm_ref, pw_ref, ps_ref, inv_ref,
                   z_ref, xs_ref, bc_ref, dt_ref, da_ref, pq_ref, yp_ref, *, tm):
    i = pl.program_id(1)
    nt = pl.num_programs(1)
    sh = sh_ref[0]
    sc = sc_ref[0]

    def modulated(xv):
        return (_ln(xv) * (1.0 + sc) + sh).astype(BF16)

    x_ext = jnp.concatenate([xp_ref[0], x_ref[0], xn_ref[0]], axis=0)
    u_ext = jnp.dot(modulated(x_ext), w_ref[...], preferred_element_type=F32)
    u = u_ext[POOL_HALO:POOL_HALO + tm]
    up = jnp.where(i > 0, u_ext[0:POOL_HALO, C_XBC:C_DT], 0.0)
    un = jnp.where(i < nt - 1, u_ext[POOL_HALO + tm:, C_XBC:C_DT], 0.0)

    z_ref[0] = u[:, C_Z:C_XBC].astype(BF16)

    xbc = u[:, C_XBC:C_FNET]
    row = lax.broadcasted_iota(jnp.int32, (tm, 1), 0)
    prev_row = up[POOL_HALO - 1:POOL_HALO, 0:XBC_WIDTH]
    next_row = un[0:1, 0:XBC_WIDTH]
    xm1 = jnp.where(row == 0, prev_row, pltpu.roll(xbc, 1, 0))
    xp1 = jnp.where(row == tm - 1, next_row, pltpu.roll(xbc, tm - 1, 0))
    cw = cw_ref[...]
    conv = xm1 * cw[0:1] + xbc * cw[1:2] + xp1 * cw[2:3] + cb_ref[...]
    act = conv * jax.nn.sigmoid(conv)
    xs_ref[0] = act[:, 0:SSD_WIDTH].astype(BF16)
    bc_ref[0] = act[:, SSD_WIDTH:XBC_WIDTH].astype(BF16)

    lane = lax.broadcasted_iota(jnp.int32, (1, LANE), 1)
    for d in range(2):
        raw = u[:, C_DT + d * LANE:C_DT + (d + 1) * LANE] + dtb_ref[d:d + 1, :]
        sp = jnp.maximum(raw, 0.0) + jnp.log1p(jnp.exp(-jnp.abs(raw)))
        dt = jnp.where(lane < SSD_HEADS, sp, 0.0)
        dt_ref[0, d] = dt
        da_ref[0, d] = dt * (-jnp.exp(alog_ref[d:d + 1, :]))

    uf = u[:, C_FNET:C_POOL].astype(BF16)
    pq_ref[0, 0] = jnp.dot(uf, fm_ref[0], preferred_element_type=F32)
    pq_ref[0, 1] = jnp.dot(uf, fm_ref[1], preferred_element_type=F32)

    off_pool = C_POOL - C_XBC
    ext = jnp.concatenate([up[:, off_pool:off_pool + POOL_WIDTH], u[:, C_POOL:C_DT],
                           un[:, off_pool:off_pool + POOL_WIDTH]], axis=0)
    n_ext = tm + 2 * POOL_HALO
    sums = {}
    cur = ext
    width = 1
    while width < POOL_WINDOWS[-1]:
        cur = cur + pltpu.roll(cur, n_ext - width, 0)
        width *= 2
        sums[width] = cur
    lane_p = lax.broadcasted_iota(jnp.int32, (1, POOL_WIDTH), 1)
    wsum = None
    for gi, win in enumerate(POOL_WINDOWS):
        start = POOL_HALO - win // 2
        s_w = sums[win]
        if start:
            s_w = pltpu.roll(s_w, n_ext - start, 0)
        s_w = s_w[0:tm]
        wsum = s_w if wsum is None else jnp.where(lane_p >= gi * POOL_GDIM, s_w, wsum)
    pooled = wsum * inv_ref[...] - u[:, C_POOL:C_DT]
    yp_ref[0] = (_bdot(pooled, pw_ref[...]) * ps_ref[...]).astype(BF16)


def _inproj_call(x, sh, sc, w_cat, conv_w, conv_b, dt_bias, a_log, fmix, pw, ps, inv, *, tm):
    bsz, seq, d = x.shape
    nt = seq // tm
    nb8 = seq // 8
    tb = tm // 8
    mod_map = (lambda b, i: (b, 0, 0)) if sh.shape[0] > 1 else (lambda b, i: (0, 0, 0))

    def full(a):
        return pl.BlockSpec(a.shape, lambda b, i: (0,) * a.ndim)

    in_specs = [
        pl.BlockSpec((1, tm, d), lambda b, i: (b, i, 0)),
        pl.BlockSpec((1, 8, d), lambda b, i: (b, jnp.maximum(i * tb - 1, 0), 0)),
        pl.BlockSpec((1, 8, d), lambda b, i: (b, jnp.minimum((i + 1) * tb, nb8 - 1), 0)),
        pl.BlockSpec((1, 1, d), mod_map),
        pl.BlockSpec((1, 1, d), mod_map),
    ] + [full(a) for a in (w_cat, conv_w, conv_b, dt_bias, a_log, fmix, pw, ps)] + [
        pl.BlockSpec((tm, POOL_WIDTH), lambda b, i: (i, 0))]
    out_shape = [
        jax.ShapeDtypeStruct((bsz, seq, SSD_WIDTH), BF16),
        jax.ShapeDtypeStruct((bsz, seq, SSD_WIDTH), BF16),
        jax.ShapeDtypeStruct((bsz, seq, BC_WIDTH), BF16),
        jax.ShapeDtypeStruct((bsz, 2, seq, LANE), F32),
        jax.ShapeDtypeStruct((bsz, 2, seq, LANE), F32),
        jax.ShapeDtypeStruct((bsz, 2, seq, FNET_WIDTH), F32),
        jax.ShapeDtypeStruct((bsz, seq, POOL_WIDTH), BF16),
    ]
    out_specs = [
        pl.BlockSpec((1, tm, SSD_WIDTH), lambda b, i: (b, i, 0)),
        pl.BlockSpec((1, tm, SSD_WIDTH), lambda b, i: (b, i, 0)),
        pl.BlockSpec((1, tm, BC_WIDTH), lambda b, i: (b, i, 0)),
        pl.BlockSpec((1, 2, tm, LANE), lambda b, i: (b, 0, i, 0)),
        pl.BlockSpec((1, 2, tm, LANE), lambda b, i: (b, 0, i, 0)),
        pl.BlockSpec((1, 2, tm, FNET_WIDTH), lambda b, i: (b, 0, i, 0)),
        pl.BlockSpec((1, tm, POOL_WIDTH), lambda b, i: (b, i, 0)),
    ]
    return pl.pallas_call(
        functools.partial(_inproj_kernel, tm=tm),
        grid=(bsz, nt),
        in_specs=in_specs,
        out_specs=out_specs,
        out_shape=out_shape,
        compiler_params=_params("parallel", "parallel"),
        name="inproj",
    )(x, x, x, sh, sc, w_cat, conv_w, conv_b, dt_bias, a_log, fmix, pw, ps, inv)


def _bf16_pieces(x, terms):
    pieces = []
    rem = x
    for t in range(terms):
        piece = rem.astype(BF16)
        pieces.append(piece)
        if t + 1 < terms:
            rem = rem - piece.astype(F32)
    return pieces


def _ssd_kernel(xsf_ref, bcf_ref, dtf_ref, daf_ref, xsb_ref, bcb_ref, dtb_ref, dab_ref, tri_ref, ex_ref, init_ref,
                yf_ref, yb_ref, fin_ref, st_ref, *, kc):
    s = pl.program_id(1)
    ns = pl.num_programs(1)

    @pl.when(s == 0)
    def _():
        st_ref[...] = init_ref[0]

    fwd = (xsf_ref, bcf_ref, dtf_ref, daf_ref, yf_ref)
    bwd = (xsb_ref, bcb_ref, dtb_ref, dab_ref, yb_ref)
    jobs = [(0, k, fwd) for k in range(kc)] + [(1, kc - 1 - k, bwd) for k in range(kc)]
    c = SSD_CHUNK
    gn = SSD_GROUPS * SSD_STATE
    pw = 2 * SSD_HEADDIM
    gw = HEADS_PER_GROUP * SSD_HEADDIM
    low_half = lax.broadcasted_iota(jnp.int32, (1, pw), 1) < SSD_HEADDIM
    zero = jnp.zeros((SSD_CHUNK, pw), BF16)
    tri_b = [tri_ref[d].astype(BF16) for d in range(2)]
    mask = [tri_ref[d] > 0.5 for d in range(2)]

    def rows(k):
        return slice(k * c, (k + 1) * c)

    cs3 = [jnp.dot(tri_b[d], jnp.concatenate(_bf16_pieces(r[3][0, 0, rows(k), :], 3), axis=1),
                   preferred_element_type=F32) for d, k, r in jobs]
    cs = [v[:, 0:LANE] + v[:, LANE:2 * LANE] + v[:, 2 * LANE:3 * LANE] for v in cs3]
    cs_t = [v.T for v in cs]
    ex4 = [jnp.dot(jnp.concatenate(_bf16_pieces(cs[n], 2) + _bf16_pieces(r[2][0, 0, rows(k), :], 2), axis=0),
                   ex_ref[...], preferred_element_type=F32) for n, (d, k, r) in enumerate(jobs)]
    cs_x = [v[0:c] + v[c:2 * c] for v in ex4]
    dt_x = [v[2 * c:3 * c] + v[3 * c:4 * c] for v in ex4]
    tot_x = [cs_x[n][(c - 1 if d == 0 else 0):(c if d == 0 else 1), :] for n, (d, k, r) in enumerate(jobs)]
    e_cs = [jnp.exp(v) for v in cs_x]
    e_tot = [jnp.exp(v) for v in tot_x]
    xd32 = [r[0][0, rows(k), :].astype(F32) * dt_x[n] for n, (d, k, r) in enumerate(jobs)]
    xd_dec = [(xd32[n] * jnp.exp(tot_x[n] - cs_x[n])).astype(BF16) for n in range(len(jobs))]
    xd = [v.astype(BF16) for v in xd32]
    bc = [r[1][0, rows(k), :] for d, k, r in jobs]
    bg_t = [[v[:, g * SSD_STATE:(g + 1) * SSD_STATE].astype(F32).T.astype(BF16) for g in range(SSD_GROUPS)]
            for v in bc]
    cg = [[v[:, gn + g * SSD_STATE:gn + (g + 1) * SSD_STATE] for g in range(SSD_GROUPS)] for v in bc]
    cb = [[jnp.dot(cg[n][g], bg_t[n][g], preferred_element_type=F32) for g in range(SSD_GROUPS)]
          for n in range(len(jobs))]

    def y_diagonal(n, d, g):
        parts = []
        for pp in range(HEADS_PER_GROUP // 2):
            p = g * (HEADS_PER_GROUP // 2) + pp
            decay = []
            for h in (2 * p, 2 * p + 1):
                seg = cs[n][:, h:h + 1] - cs_t[n][h:h + 1, :]
                decay.append(cb[n][g] * jnp.where(mask[d], jnp.exp(jnp.minimum(seg, 0.0)), 0.0))
            m = jnp.concatenate(decay, axis=1).astype(BF16)
            xp = xd[n][:, p * pw:(p + 1) * pw]
            x_bd = jnp.concatenate([jnp.where(low_half, xp, zero), jnp.where(low_half, zero, xp)], axis=0)
            parts.append(jnp.dot(m, x_bd, preferred_element_type=F32))
        return jnp.concatenate(parts, axis=1)

    y_diag = [[y_diagonal(n, d, g) for g in range(SSD_GROUPS)] for n, (d, k, r) in enumerate(jobs)]

    for n, (d, k, r) in enumerate(jobs):
        for g in range(SSD_GROUPS):
            glanes = slice(g * gw, (g + 1) * gw)
            st = st_ref[d, g]
            y_off = jnp.dot(cg[n][g], st.astype(BF16), preferred_element_type=F32) * e_cs[n][:, glanes]
            r[4][0, rows(k), glanes] = (y_diag[n][g] + y_off).astype(BF16)
            st_ref[d, g] = st * e_tot[n][:, glanes] + jnp.dot(bg_t[n][g], xd_dec[n][:, glanes],
                                                              preferred_element_type=F32)

    @pl.when(s == ns - 1)
    def _():
        fin_ref[0] = st_ref[...]


def _ssd_call(xs, bc, dt, da, tri, ex, init):
    bsz, seq, _ = xs.shape
    kc = min(SSD_CHUNKS_PER_STEP, seq // SSD_CHUNK)
    rows = kc * SSD_CHUNK
    ns = seq // rows
    st_block = (1,) + SSD_STATE_SHAPE

    def tok(w, rev):
        return pl.BlockSpec((1, rows, w), lambda b, s: (b, ns - 1 - s if rev else s, 0))

    def per_dir(w, d):
        return pl.BlockSpec((1, 1, rows, w), lambda b, s: (b, d, ns - 1 - s if d else s, 0))

    return pl.pallas_call(
        functools.partial(_ssd_kernel, kc=kc),
        grid=(bsz, ns),
        in_specs=[
            tok(SSD_WIDTH, False), tok(BC_WIDTH, False), per_dir(LANE, 0), per_dir(LANE, 0),
            tok(SSD_WIDTH, True), tok(BC_WIDTH, True), per_dir(LANE, 1), per_dir(LANE, 1),
            pl.BlockSpec(tri.shape, lambda b, s: (0, 0, 0)),
            pl.BlockSpec(ex.shape, lambda b, s: (0, 0)),
            pl.BlockSpec(st_block, lambda b, s: (b, 0, 0, 0, 0)),
        ],
        out_specs=[tok(SSD_WIDTH, False), tok(SSD_WIDTH, True), pl.BlockSpec(st_block, lambda b, s: (b, 0, 0, 0, 0))],
        out_shape=[
            jax.ShapeDtypeStruct((bsz, seq, SSD_WIDTH), BF16),
            jax.ShapeDtypeStruct((bsz, seq, SSD_WIDTH), BF16),
            jax.ShapeDtypeStruct((bsz,) + SSD_STATE_SHAPE, F32),
        ],
        scratch_shapes=[pltpu.VMEM(SSD_STATE_SHAPE, F32)],
        compiler_params=_params("parallel", "arbitrary"),
        name="ssd",
    )(xs, bc, dt, da, xs, bc, dt, da, tri, ex, init)


def _dft1_kernel(p_ref, q_ref, m_ref, t_ref, *, n1):
    for j in range(DFT_NB):
        m = m_ref[j]
        rp = _bdot(m, p_ref[0, 0, :, j, :])
        rq = _bdot(m, q_ref[0, 0, :, j, :])
        t_ref[0, 0, j] = (rp[:n1] + rq[n1:]).astype(BF16)
        t_ref[0, 1, j] = (rp[n1:] - rq[:n1]).astype(BF16)


def _dft1_call(pq, mtab):
    bsz, _, seq, w = pq.shape
    n1 = seq // DFT_N2
    pqv = pq.reshape(bsz, 2, n1, DFT_N2, w)
    return pl.pallas_call(
        functools.partial(_dft1_kernel, n1=n1),
        grid=(bsz, DFT_N2 // DFT_NB),
        in_specs=[
            pl.BlockSpec((1, 1, n1, DFT_NB, w), lambda b, j: (b, 0, 0, j, 0)),
            pl.BlockSpec((1, 1, n1, DFT_NB, w), lambda b, j: (b, 1, 0, j, 0)),
            pl.BlockSpec((DFT_NB, 2 * n1, n1), lambda b, j: (j, 0, 0)),
        ],
        out_specs=pl.BlockSpec((1, 2, DFT_NB, n1, w), lambda b, j: (b, 0, j, 0, 0)),
        out_shape=jax.ShapeDtypeStruct((bsz, 2, DFT_N2, n1, w), BF16),
        compiler_params=_params("parallel", "parallel"),
        name="dft1",
    )(pqv, pqv, mtab)


def _dft2_kernel(d_ref, t_ref, o_ref):
    o_ref[0] = _bdot(d_ref[...], t_ref[0]).astype(BF16)


def _dft2_call(dmat, t, *, tn):
    bsz, k, n = t.shape
    mo = dmat.shape[0]
    return pl.pallas_call(
        _dft2_kernel,
        grid=(bsz, n // tn),
        in_specs=[
            pl.BlockSpec((mo, k), lambda b, j: (0, 0)),
            pl.BlockSpec((1, k, tn), lambda b, j: (b, 0, j)),
        ],
        out_specs=pl.BlockSpec((1, mo, tn), lambda b, j: (b, 0, j)),
        out_shape=jax.ShapeDtypeStruct((bsz, mo, n), BF16),
        compiler_params=_params("parallel", "parallel"),
        name="dft2",
    )(dmat, t)


def _mix_kernel(yf_ref, yb_ref, xs_ref, z_ref, yfn_ref, ypl_ref, x_ref, dsk_ref, nw_ref, wo_ref, g1_ref,
                lg_ref, lb_ref, sh2_ref, sc2_ref, x1_ref, h2_ref):
    y = yf_ref[0].astype(F32) + yb_ref[0].astype(F32) + xs_ref[0].astype(F32) * dsk_ref[...]
    z = z_ref[0].astype(F32)
    y = y * (z * jax.nn.sigmoid(z))
    gw = SSD_WIDTH // SSD_GROUPS
    parts = []
    for g in range(SSD_GROUPS):
        yg = y[:, g * gw:(g + 1) * gw]
        ms = jnp.mean(yg * yg, axis=-1, keepdims=True)
        parts.append(yg * lax.rsqrt(ms + EPS))
    yn = jnp.concatenate(parts, axis=-1) * nw_ref[...]
    o_f = SSD_WIDTH
    o_p = SSD_WIDTH + FNET_WIDTH
    mix = (_bdot(yn, wo_ref[0:o_f, :]) + _bdot(yfn_ref[0], wo_ref[o_f:o_p, :])
           + _bdot(ypl_ref[0], wo_ref[o_p:, :]))
    x1 = _ln(ALPHA * x_ref[0] + g1_ref[0] * mix) * lg_ref[...] + lb_ref[...]
    x1_ref[0] = x1
    h2_ref[0] = (_ln(x1) * (1.0 + sc2_ref[0]) + sh2_ref[0]).astype(BF16)


def _mix_call(yf, yb, xs, z, yfn, ypl, x, dsk, nw, wo, g1, lg, lb, sh2, sc2, *, tm):
    bsz, seq, d = x.shape
    mod_map = (lambda b, i: (b, 0, 0)) if g1.shape[0] > 1 else (lambda b, i: (0, 0, 0))

    def full(a):
        return pl.BlockSpec(a.shape, lambda b, i: (0,) * a.ndim)

    def tok(w):
        return pl.BlockSpec((1, tm, w), lambda b, i: (b, i, 0))

    mod = pl.BlockSpec((1, 1, d), mod_map)
    return pl.pallas_call(
        _mix_kernel,
        grid=(bsz, seq // tm),
        in_specs=[
            tok(SSD_WIDTH), tok(SSD_WIDTH), tok(SSD_WIDTH), tok(SSD_WIDTH), tok(FNET_WIDTH), tok(POOL_WIDTH), tok(d),
            full(dsk), full(nw), full(wo), mod, full(lg), full(lb), mod, mod,
        ],
        out_specs=[tok(d), tok(d)],
        out_shape=[jax.ShapeDtypeStruct((bsz, seq, d), F32), jax.ShapeDtypeStruct((bsz, seq, d), BF16)],
        compiler_params=_params("parallel", "parallel"),
        name="mix",
    )(yf, yb, xs, z, yfn, ypl, x, dsk, nw, wo, g1, lg, lb, sh2, sc2)


def _ffn_kernel(h_ref, hp_ref, hn_ref, wu_ref, cw_ref, cb_ref, wd_ref, x1_ref, g2_ref, lg_ref, lb_ref, o_ref,
                hbuf, a0, a1, hid, acc, *, tm, on_grid):
    i = pl.program_id(1)
    nt = pl.num_programs(1)
    nj = D_FF // FFN_CB
    ext = tm + 2 * FFN_HALO
    ra = ext // FFN_UP_CHUNKS
    rb = tm // FFN_ROW_CHUNKS
    slots = (a0, a1)
    base = FFN_PAD + FFN_HALO
    rows = (-1, 0, 1) if on_grid else (0,)

    def up_project(a_ref, j):
        for r in range(FFN_UP_CHUNKS):
            a = jnp.dot(hbuf[r * ra:(r + 1) * ra, :], wu_ref[j], preferred_element_type=F32)
            a_ref[FFN_PAD + r * ra:FFN_PAD + (r + 1) * ra, :] = a.astype(BF16)

    def conv(a_ref, j, r):
        cw = cw_ref[j].astype(BF16)
        m = FFN_PAD
        n = rb + 2 * m
        start = base + r * rb - m
        col = (lax.broadcasted_iota(jnp.int32, (n, 1), 0) + (r * rb - m)) & (GRID_W - 1)
        taps = {dr: a_ref[pl.ds(start + dr * GRID_W, n), :] for dr in rows}

        def column(dc, lo, hi):
            acc_dc = None
            for dr in rows:
                k = (dr + 1) * 3 + dc + 1
                term = taps[dr][lo:hi] * cw[k:k + 1]
                acc_dc = term if acc_dc is None else acc_dc + term
            return acc_dc

        left = column(-1, 0, n)
        right = column(1, 0, n)
        if on_grid:
            zero = jnp.zeros_like(left)
            left = jnp.where(col == GRID_W - 1, zero, left)
            right = jnp.where(col == 0, zero, right)
        left2 = jnp.concatenate([left[n - 2:], left[:n - 2]], axis=0)
        u = (left2 + right)[m + 1:m + 1 + rb]
        return column(0, m, m + rb) + u + cb_ref[j].astype(BF16)

    hbuf[0:FFN_HALO] = jnp.where(i > 0, hp_ref[0], jnp.zeros_like(hp_ref[0]))
    hbuf[FFN_HALO:FFN_HALO + tm] = h_ref[0]
    hbuf[FFN_HALO + tm:] = jnp.where(i < nt - 1, hn_ref[0], jnp.zeros_like(hn_ref[0]))
    for a in slots:
        a[0:FFN_PAD] = jnp.zeros((FFN_PAD, 2 * FFN_CB), BF16)
        a[FFN_PAD + ext:] = jnp.zeros((FFN_PAD, 2 * FFN_CB), BF16)

    up_project(slots[0], 0)
    for j in range(nj):
        if j + 1 < nj:
            up_project(slots[(j + 1) % 2], j + 1)
        half = j % 2
        for r in range(FFN_ROW_CHUNKS):
            c = conv(slots[half], j, r)
            hidden = jax.nn.gelu(c[:, FFN_CB:], approximate=True) * c[:, :FFN_CB]
            hid[r * rb:(r + 1) * rb, half * FFN_CB:(half + 1) * FFN_CB] = hidden
        if half == 1 or j == nj - 1:
            first = j - half
            k = (half + 1) * FFN_CB
            part = jnp.dot(hid[:, 0:k], wd_ref[first * FFN_CB:first * FFN_CB + k, :], preferred_element_type=F32)
            if first == 0:
                acc[...] = part
            else:
                acc[...] += part

    o_ref[0] = _ln(ALPHA * x1_ref[0] + g2_ref[0] * acc[...]) * lg_ref[...] + lb_ref[...]


def _ffn_call(h, w_up, conv_w9, conv_b, w_down, x1, g2, lg, lb, *, tm, on_grid):
    bsz, seq, d = h.shape
    nt = seq // tm
    th = tm // FFN_HALO
    nh = seq // FFN_HALO
    ext = tm + 2 * FFN_HALO
    mod_map = (lambda b, i: (b, 0, 0)) if g2.shape[0] > 1 else (lambda b, i: (0, 0, 0))

    def resident(a):
        return pl.BlockSpec(a.shape, lambda b, i: (0,) * a.ndim, pipeline_mode=pl.Buffered(1))

    a_scratch = pltpu.VMEM((ext + 2 * FFN_PAD, 2 * FFN_CB), BF16)
    return pl.pallas_call(
        functools.partial(_ffn_kernel, tm=tm, on_grid=on_grid),
        grid=(bsz, nt),
        in_specs=[
            pl.BlockSpec((1, tm, d), lambda b, i: (b, i, 0)),
            pl.BlockSpec((1, FFN_HALO, d), lambda b, i: (b, jnp.maximum(i * th - 1, 0), 0)),
            pl.BlockSpec((1, FFN_HALO, d), lambda b, i: (b, jnp.minimum((i + 1) * th, nh - 1), 0)),
            resident(w_up), resident(conv_w9), resident(conv_b), resident(w_down),
            pl.BlockSpec((1, tm, d), lambda b, i: (b, i, 0)),
            pl.BlockSpec((1, 1, d), mod_map),
            resident(lg), resident(lb),
        ],
        out_specs=pl.BlockSpec((1, tm, d), lambda b, i: (b, i, 0)),
        out_shape=jax.ShapeDtypeStruct((bsz, seq, d), F32),
        scratch_shapes=[pltpu.VMEM((ext, d), BF16), a_scratch, a_scratch, pltpu.VMEM((tm, 2 * FFN_CB), BF16),
                        pltpu.VMEM((tm, d), F32)],
        compiler_params=_params("parallel", "arbitrary"),
        name="ffn",
    )(h, h, h, w_up, conv_w9, conv_b, w_down, x1, g2, lg, lb)


def _block_diag(blocks):
    g, n, m = blocks.shape
    out = jnp.zeros((g * n, g * m), blocks.dtype)
    for k in range(g):
        out = out.at[k * n:(k + 1) * n, k * m:(k + 1) * m].set(blocks[k])
    return out


def _pair_blocks(w):
    rows = w.shape[0]
    halves = w.reshape(rows, 2, D_FF // FFN_CB, FFN_CB)
    return jnp.transpose(halves, (2, 0, 1, 3)).reshape(D_FF // FFN_CB, rows, 2 * FFN_CB)


def _dft_tables(seq):
    n1 = seq // DFT_N2
    k1 = np.arange(n1, dtype=np.int64)[:, None]
    nn1 = np.arange(n1, dtype=np.int64)[None, :]
    tabs = []
    for n2 in range(DFT_N2):
        th = 2.0 * np.pi * ((k1 * (DFT_N2 * nn1 + n2)) % seq) / seq
        tabs.append(np.concatenate([np.cos(th), -np.sin(th)], axis=0))
    mtab = np.stack(tabs)
    k2 = np.arange(DFT_N2, dtype=np.int64)[:, None]
    nn2 = np.arange(DFT_N2, dtype=np.int64)[None, :]
    th2 = 2.0 * np.pi * ((k2 * nn2) % DFT_N2) / DFT_N2
    scale = 1.0 / math.sqrt(seq * FNET_GDIM)
    dmat = scale * np.concatenate([np.cos(th2), np.sin(th2)], axis=1)
    return jnp.asarray(mtab, F32), jnp.asarray(dmat, F32)


def _dense_dft_table(seq):
    k = np.arange(seq, dtype=np.int64)
    th = 2.0 * np.pi * ((k[:, None] * k[None, :]) % seq) / seq
    scale = 1.0 / math.sqrt(seq * FNET_GDIM)
    return jnp.asarray(scale * np.concatenate([np.cos(th), -np.sin(th)], axis=1), F32)


def _channel_dft_tables():
    k = np.arange(FNET_GDIM, dtype=np.int64)
    th = 2.0 * np.pi * ((k[:, None] * k[None, :]) % FNET_GDIM) / FNET_GDIM
    eye = np.eye(FNET_GROUPS)
    return jnp.asarray(np.kron(eye, np.cos(th)), F32), jnp.asarray(np.kron(eye, np.sin(th)), F32)


def _pool_inverse_counts(seq):
    pos = np.arange(seq, dtype=np.int64)
    cols = []
    for win in POOL_WINDOWS:
        left = win // 2
        cnt = np.minimum(pos + (win - left), seq) - np.maximum(pos - left, 0)
        cols.append(np.repeat((1.0 / cnt)[:, None], POOL_GDIM, axis=1))
    return jnp.asarray(np.concatenate(cols, axis=1), F32)


def _scan_matrices():
    r = np.arange(SSD_CHUNK)
    lower = (r[:, None] >= r[None, :]).astype(np.float32)
    return jnp.asarray(np.stack([lower, lower.T]), F32)


def _head_expansion():
    ex = np.zeros((LANE, SSD_WIDTH), np.float32)
    for h in range(SSD_HEADS):
        ex[h, h * SSD_HEADDIM:(h + 1) * SSD_HEADDIM] = 1.0
    return jnp.asarray(ex, BF16)


def _token_mixer(x, sh1, sc1, lw, init, tri, ex, *, tm):
    bsz, seq, _ = x.shape
    z, xs, bc, dt, da, pq, ypl = _inproj_call(
        x, sh1, sc1, lw["w_cat"], lw["ssd_conv_w"], lw["ssd_conv_b"], lw["dt_bias"], lw["a_log"], lw["fmix"],
        lw["pool_bd"], lw["pool_scale"], _pool_inverse_counts(seq), tm=tm)
    yf, yb, fin = _ssd_call(xs, bc, dt, da, tri, ex, init)
    return z, xs, yf, yb, pq, ypl, fin


def _fourier_positions(pq):
    bsz, _, seq, w = pq.shape
    if seq % (DFT_N2 * 8) == 0:
        n1 = seq // DFT_N2
        mtab, dmat = _dft_tables(seq)
        t = _dft1_call(pq, mtab).reshape(bsz, 2 * DFT_N2, n1 * w)
        out = _dft2_call(dmat, t, tn=min(4096, n1 * w))
        return out.reshape(bsz, seq, w)
    return _dft2_call(_dense_dft_table(seq), pq.reshape(bsz, 2 * seq, w), tn=w)


def kernel(x, c, ctx, c_ctx, w_ada, b_ada, w_in, ssd_conv_w, ssd_conv_b, ssd_dt_bias, ssd_a_log, ssd_d,
           ssd_norm_w, fnet_w, pool_w, pool_scale, w_out, ln1_g, ln1_b, ffn_w_up, ffn_conv_w, ffn_conv_b,
           ffn_w_down, ln2_g, ln2_b):
    bsz, seq, d = x.shape
    n_ctx = ctx.shape[1]
    assert d == D_MODEL and bsz <= 7 and seq % (GRID_W * 8) == 0 and n_ctx % SSD_CHUNK == 0
    tm_lat = min(512, seq)
    tm_ffn = min(1024, seq)

    cv = jnp.zeros((8, d), F32).at[0:bsz].set(c).at[bsz].set(c_ctx)
    mods = _ada_call(cv, w_ada, b_ada)
    tri = _scan_matrices()
    ex = _head_expansion()
    fmix = _fmix_call(*_channel_dft_tables(), jnp.stack([_block_diag(fnet_w[l]) for l in range(DEPTH)]))
    zero_state = jnp.zeros((bsz,) + SSD_STATE_SHAPE, F32)
    lane_pad = jnp.zeros((d, LANE - SSD_HEADS), F32)

    for l in range(DEPTH):
        last = l == DEPTH - 1
        m = mods[l].reshape(8, 6, d)
        lat = [m[0:bsz, k][:, None, :] for k in range(6)]
        cx = [m[bsz:bsz + 1, k][:, None, :] for k in range(6)]
        w = w_in[l]
        w_cat = jnp.concatenate(
            [w[:, 0:OFF_DT], w[:, OFF_FNET:N_IN], w[:, OFF_DT:OFF_DT + SSD_HEADS], lane_pad,
             w[:, OFF_DT + SSD_HEADS:OFF_FNET], lane_pad], axis=1).astype(BF16)
        pad16 = ((0, 0), (0, LANE - SSD_HEADS))
        lw = {
            "w_cat": w_cat,
            "ssd_conv_w": ssd_conv_w[l],
            "ssd_conv_b": ssd_conv_b[l][None, :],
            "dt_bias": jnp.pad(ssd_dt_bias[l], pad16),
            "a_log": jnp.pad(ssd_a_log[l], pad16),
            "fmix": fmix[l],
            "pool_bd": _block_diag(pool_w[l]).astype(BF16),
            "pool_scale": pool_scale[l][None, :],
        }
        dsk = jnp.repeat(ssd_d[l], SSD_HEADDIM)[None, :]
        nw = ssd_norm_w[l][None, :]
        wo = w_out[l].astype(BF16)
        lg1, lb1 = ln1_g[l][None, :], ln1_b[l][None, :]
        lg2, lb2 = ln2_g[l][None, :], ln2_b[l][None, :]
        w_up = _pair_blocks(ffn_w_up[l]).astype(BF16)
        w_down = ffn_w_down[l].astype(BF16)
        conv9 = _pair_blocks(ffn_conv_w[l].reshape(9, 2 * D_FF))
        conv_b = _pair_blocks(ffn_conv_b[l][None, :])

        cz, cxs, cyf, cyb, cpq, cypl, c_fin = _token_mixer(ctx, cx[0], cx[1], lw, zero_state, tri, ex, tm=n_ctx)
        z, xs, yf, yb, pq, ypl, _ = _token_mixer(x, lat[0], lat[1], lw, c_fin, tri, ex, tm=tm_lat)

        yfn = _fourier_positions(pq)
        x1, h2 = _mix_call(yf, yb, xs, z, yfn, ypl, x, dsk, nw, wo, lat[2], lg1, lb1, lat[3], lat[4], tm=tm_lat)
        x = _ffn_call(h2, w_up, conv9, conv_b, w_down, x1, lat[5], lg2, lb2, tm=tm_ffn, on_grid=True)

        if not last:
            cyfn = _fourier_positions(cpq)
            c1, ch2 = _mix_call(cyf, cyb, cxs, cz, cyfn, cypl, ctx, dsk, nw, wo, cx[2], lg1, lb1, cx[3], cx[4],
                                tm=n_ctx)
            ctx = _ffn_call(ch2, w_up, conv9, conv_b, w_down, c1, cx[5], lg2, lb2, tm=n_ctx, on_grid=False)
    return x
```

```python
import functools
import math

import numpy as np
import jax
import jax.numpy as jnp
from jax import lax
from jax.experimental import pallas as pl
from jax.experimental.pallas import tpu as pltpu

F32 = jnp.float32
BF16 = jnp.bfloat16
HIGHEST = lax.Precision.HIGHEST

D_MODEL = 1024
DEPTH = 2
GRID_W = 64
SSD_WIDTH = 512
SSD_HEADDIM = 64
SSD_HEADS = 8
SSD_GROUPS = 2
HEADS_PER_GROUP = SSD_HEADS // SSD_GROUPS
SSD_STATE = 128
SSD_CHUNK = 128
SSD_CHUNKS_PER_STEP = 8
XBC_WIDTH = SSD_WIDTH + 2 * SSD_GROUPS * SSD_STATE
BC_WIDTH = 2 * SSD_GROUPS * SSD_STATE
SSD_STATE_SHAPE = (2, SSD_GROUPS, SSD_STATE, HEADS_PER_GROUP * SSD_HEADDIM)
FNET_WIDTH = 256
FNET_GROUPS = 4
FNET_GDIM = 64
POOL_WINDOWS = (2, 4, 8, 16)
POOL_WIDTH = 256
POOL_GDIM = 64
POOL_HALO = 8
D_FF = 2816
ALPHA = (2.0 * DEPTH) ** 0.25
EPS = 1e-6

OFF_XBC = SSD_WIDTH
OFF_DT = OFF_XBC + XBC_WIDTH
OFF_FNET = OFF_DT + 2 * SSD_HEADS
OFF_POOL = OFF_FNET + FNET_WIDTH
N_IN = OFF_POOL + POOL_WIDTH

LANE = 128
C_Z = 0
C_XBC = C_Z + SSD_WIDTH
C_FNET = C_XBC + XBC_WIDTH
C_POOL = C_FNET + FNET_WIDTH
C_DT = C_POOL + POOL_WIDTH
N_CAT = C_DT + 2 * LANE

DFT_N2 = 64
DFT_NB = 8
FFN_CB = 256
FFN_HALO = GRID_W
FFN_PAD = 16
FFN_ROW_CHUNKS = 2
FFN_UP_CHUNKS = 1
VMEM_LIMIT = 56 * 1024 * 1024


def _ln(x):
    mu = jnp.mean(x, axis=-1, keepdims=True)
    xc = x - mu
    var = jnp.mean(xc * xc, axis=-1, keepdims=True)
    return xc * lax.rsqrt(var + EPS)


def _bdot(a, b):
    return jnp.dot(a.astype(BF16), b.astype(BF16), preferred_element_type=F32)


def _params(*sem):
    return pltpu.CompilerParams(dimension_semantics=sem, vmem_limit_bytes=VMEM_LIMIT)


def _ada_kernel(c_ref, w_ref, b_ref, o_ref):
    cv = c_ref[...]
    s = cv * jax.nn.sigmoid(cv)
    o_ref[0] = jnp.dot(s, w_ref[0], preferred_element_type=F32, precision=HIGHEST) + b_ref[0]


def _ada_call(cv, w_ada, b_ada):
    depth, d, n = w_ada.shape
    tn = 1536
    return pl.pallas_call(
        _ada_kernel,
        grid=(depth, n // tn),
        in_specs=[
            pl.BlockSpec((8, d), lambda l, j: (0, 0)),
            pl.BlockSpec((1, d, tn), lambda l, j: (l, 0, j)),
            pl.BlockSpec((1, 1, tn), lambda l, j: (l, 0, j)),
        ],
        out_specs=pl.BlockSpec((1, 8, tn), lambda l, j: (l, 0, j)),
        out_shape=jax.ShapeDtypeStruct((depth, 8, n), F32),
        compiler_params=_params("parallel", "parallel"),
        name="ada",
    )(cv, w_ada, b_ada.reshape(depth, 1, n))


def _fmix_kernel(cc_ref, cs_ref, fw_ref, o_ref):
    fw = fw_ref[0]
    o_ref[0, 0] = jnp.dot(cc_ref[...], fw, preferred_element_type=F32, precision=HIGHEST).astype(BF16)
    o_ref[0, 1] = jnp.dot(cs_ref[...], fw, preferred_element_type=F32, precision=HIGHEST).astype(BF16)


def _fmix_call(cc, cs, fw_bd):
    depth, w, _ = fw_bd.shape
    return pl.pallas_call(
        _fmix_kernel,
        grid=(depth,),
        in_specs=[
            pl.BlockSpec((w, w), lambda l: (0, 0)),
            pl.BlockSpec((w, w), lambda l: (0, 0)),
            pl.BlockSpec((1, w, w), lambda l: (l, 0, 0)),
        ],
        out_specs=pl.BlockSpec((1, 2, w, w), lambda l: (l, 0, 0, 0)),
        out_shape=jax.ShapeDtypeStruct((depth, 2, w, w), BF16),
        compiler_params=_params("parallel"),
        name="fmix",
    )(cc, cs, fw_bd)


def _inproj_kernel(x_ref, xp_ref, xn_ref, sh_ref, sc_ref, w_ref, cw_ref, cb_ref, dtb_ref, alog_ref,
                   fm_ref, pw_ref, ps_ref, inv_ref,
                   z_ref, xs_ref, bc_ref, dt_ref, da_ref, pq_ref, yp_ref, *, tm):
    i = pl.program_id(1)
    nt = pl.num_programs(1)
    sh = sh_ref[0]
    sc = sc_ref[0]

    def modulated(xv):
        return (_ln(xv) * (1.0 + sc) + sh).astype(BF16)

    x_ext = jnp.concatenate([xp_ref[0], x_ref[0], xn_ref[0]], axis=0)
    u_ext = jnp.dot(modulated(x_ext), w_ref[...], preferred_element_type=F32)
    u = u_ext[POOL_HALO:POOL_HALO + tm]
    up = jnp.where(i > 0, u_ext[0:POOL_HALO, C_XBC:C_DT], 0.0)
    un = jnp.where(i < nt - 1, u_ext[POOL_HALO + tm:, C_XBC:C_DT], 0.0)

    z_ref[0] = u[:, C_Z:C_XBC].astype(BF16)

    xbc = u[:, C_XBC:C_FNET]
    row = lax.broadcasted_iota(jnp.int32, (tm, 1), 0)
    prev_row = up[POOL_HALO - 1:POOL_HALO, 0:XBC_WIDTH]
    next_row = un[0:1, 0:XBC_WIDTH]
    xm1 = jnp.where(row == 0, prev_row, pltpu.roll(xbc, 1, 0))
    xp1 = jnp.where(row == tm - 1, next_row, pltpu.roll(xbc, tm - 1, 0))
    cw = cw_ref[...]
    conv = xm1 * cw[0:1] + xbc * cw[1:2] + xp1 * cw[2:3] + cb_ref[...]
    act = conv * jax.nn.sigmoid(conv)
    xs_ref[0] = act[:, 0:SSD_WIDTH].astype(BF16)
    bc_ref[0] = act[:, SSD_WIDTH:XBC_WIDTH].astype(BF16)

    lane = lax.broadcasted_iota(jnp.int32, (1, LANE), 1)
    for d in range(2):
        raw = u[:, C_DT + d * LANE:C_DT + (d + 1) * LANE] + dtb_ref[d:d + 1, :]
        sp = jnp.maximum(raw, 0.0) + jnp.log1p(jnp.exp(-jnp.abs(raw)))
        dt = jnp.where(lane < SSD_HEADS, sp, 0.0)
        dt_ref[0, d] = dt
        da_ref[0, d] = dt * (-jnp.exp(alog_ref[d:d + 1, :]))

    uf = u[:, C_FNET:C_POOL].astype(BF16)
    pq_ref[0, 0] = jnp.dot(uf, fm_ref[0], preferred_element_type=F32)
    pq_ref[0, 1] = jnp.dot(uf, fm_ref[1], preferred_element_type=F32)

    off_pool = C_POOL - C_XBC
    ext = jnp.concatenate([up[:, off_pool:off_pool + POOL_WIDTH], u[:, C_POOL:C_DT],
                           un[:, off_pool:off_pool + POOL_WIDTH]], axis=0)
    n_ext = tm + 2 * POOL_HALO
    sums = {}
    cur = ext
    width = 1
    while width < POOL_WINDOWS[-1]:
        cur = cur + pltpu.roll(cur, n_ext - width, 0)
        width *= 2
        sums[width] = cur
    lane_p = lax.broadcasted_iota(jnp.int32, (1, POOL_WIDTH), 1)
    wsum = None
    for gi, win in enumerate(POOL_WINDOWS):
        start = POOL_HALO - win // 2
        s_w = sums[win]
        if start:
            s_w = pltpu.roll(s_w, n_ext - start, 0)
        s_w = s_w[0:tm]
        wsum = s_w if wsum is None else jnp.where(lane_p >= gi * POOL_GDIM, s_w, wsum)
    pooled = wsum * inv_ref[...] - u[:, C_POOL:C_DT]
    yp_ref[0] = (_bdot(pooled, pw_ref[...]) * ps_ref[...]).astype(BF16)


def _inproj_call(x, sh, sc, w_cat, conv_w, conv_b, dt_bias, a_log, fmix, pw, ps, inv, *, tm):
    bsz, seq, d = x.shape
    nt = seq // tm
    nb8 = seq // 8
    tb = tm // 8
    mod_map = (lambda b, i: (b, 0, 0)) if sh.shape[0] > 1 else (lambda b, i: (0, 0, 0))

    def full(a):
        return pl.BlockSpec(a.shape, lambda b, i: (0,) * a.ndim)

    in_specs = [
        pl.BlockSpec((1, tm, d), lambda b, i: (b, i, 0)),
        pl.BlockSpec((1, 8, d), lambda b, i: (b, jnp.maximum(i * tb - 1, 0), 0)),
        pl.BlockSpec((1, 8, d), lambda b, i: (b, jnp.minimum((i + 1) * tb, nb8 - 1), 0)),
        pl.BlockSpec((1, 1, d), mod_map),
        pl.BlockSpec((1, 1, d), mod_map),
    ] + [full(a) for a in (w_cat, conv_w, conv_b, dt_bias, a_log, fmix, pw, ps)] + [
        pl.BlockSpec((tm, POOL_WIDTH), lambda b, i: (i, 0))]
    out_shape = [
        jax.ShapeDtypeStruct((bsz, seq, SSD_WIDTH), BF16),
        jax.ShapeDtypeStruct((bsz, seq, SSD_WIDTH), BF16),
        jax.ShapeDtypeStruct((bsz, seq, BC_WIDTH), BF16),
        jax.ShapeDtypeStruct((bsz, 2, seq, LANE), F32),
        jax.ShapeDtypeStruct((bsz, 2, seq, LANE), F32),
        jax.ShapeDtypeStruct((bsz, 2, seq, FNET_WIDTH), F32),
        jax.ShapeDtypeStruct((bsz, seq, POOL_WIDTH), BF16),
    ]
    out_specs = [
        pl.BlockSpec((1, tm, SSD_WIDTH), lambda b, i: (b, i, 0)),
        pl.BlockSpec((1, tm, SSD_WIDTH), lambda b, i: (b, i, 0)),
        pl.BlockSpec((1, tm, BC_WIDTH), lambda b, i: (b, i, 0)),
        pl.BlockSpec((1, 2, tm, LANE), lambda b, i: (b, 0, i, 0)),
        pl.BlockSpec((1, 2, tm, LANE), lambda b, i: (b, 0, i, 0)),
        pl.BlockSpec((1, 2, tm, FNET_WIDTH), lambda b, i: (b, 0, i, 0)),
        pl.BlockSpec((1, tm, POOL_WIDTH), lambda b, i: (b, i, 0)),
    ]
    return pl.pallas_call(
        functools.partial(_inproj_kernel, tm=tm),
        grid=(bsz, nt),
        in_specs=in_specs,
        out_specs=out_specs,
        out_shape=out_shape,
        compiler_params=_params("parallel", "parallel"),
        name="inproj",
    )(x, x, x, sh, sc, w_cat, conv_w, conv_b, dt_bias, a_log, fmix, pw, ps, inv)


def _bf16_pieces(x, terms):
    pieces = []
    rem = x
    for t in range(terms):
        piece = rem.astype(BF16)
        pieces.append(piece)
        if t + 1 < terms:
            rem = rem - piece.astype(F32)
    return pieces


def _ssd_kernel(xsf_ref, bcf_ref, dtf_ref, daf_ref, xsb_ref, bcb_ref, dtb_ref, dab_ref, tri_ref, ex_ref, init_ref,
                yf_ref, yb_ref, fin_ref, st_ref, *, kc):
    s = pl.program_id(1)
    ns = pl.num_programs(1)

    @pl.when(s == 0)
    def _():
        st_ref[...] = init_ref[0]

    fwd = (xsf_ref, bcf_ref, dtf_ref, daf_ref, yf_ref)
    bwd = (xsb_ref, bcb_ref, dtb_ref, dab_ref, yb_ref)
    jobs = [(0, k, fwd) for k in range(kc)] + [(1, kc - 1 - k, bwd) for k in range(kc)]
    c = SSD_CHUNK
    gn = SSD_GROUPS * SSD_STATE
    pw = 2 * SSD_HEADDIM
    gw = HEADS_PER_GROUP * SSD_HEADDIM
    low_half = lax.broadcasted_iota(jnp.int32, (1, pw), 1) < SSD_HEADDIM
    zero = jnp.zeros((SSD_CHUNK, pw), BF16)
    tri_b = [tri_ref[d].astype(BF16) for d in range(2)]
    mask = [tri_ref[d] > 0.5 for d in range(2)]

    def rows(k):
        return slice(k * c, (k + 1) * c)

    cs3 = [jnp.dot(tri_b[d], jnp.concatenate(_bf16_pieces(r[3][0, 0, rows(k), :], 3), axis=1),
                   preferred_element_type=F32) for d, k, r in jobs]
    cs = [v[:, 0:LANE] + v[:, LANE:2 * LANE] + v[:, 2 * LANE:3 * LANE] for v in cs3]
    cs_t = [v.T for v in cs]
    ex4 = [jnp.dot(jnp.concatenate(_bf16_pieces(cs[n], 2) + _bf16_pieces(r[2][0, 0, rows(k), :], 2), axis=0),
                   ex_ref[...], preferred_element_type=F32) for n, (d, k, r) in enumerate(jobs)]
    cs_x = [v[0:c] + v[c:2 * c] for v in ex4]
    dt_x = [v[2 * c:3 * c] + v[3 * c:4 * c] for v in ex4]
    tot_x = [cs_x[n][(c - 1 if d == 0 else 0):(c if d == 0 else 1), :] for n, (d, k, r) in enumerate(jobs)]
    e_cs = [jnp.exp(v) for v in cs_x]
    e_tot = [jnp.exp(v) for v in tot_x]
    xd32 = [r[0][0, rows(k), :].astype(F32) * dt_x[n] for n, (d, k, r) in enumerate(jobs)]
    xd_dec = [(xd32[n] * jnp.exp(tot_x[n] - cs_x[n])).astype(BF16) for n in range(len(jobs))]
    xd = [v.astype(BF16) for v in xd32]
    bc = [r[1][0, rows(k), :] for d, k, r in jobs]
    bg_t = [[v[:, g * SSD_STATE:(g + 1) * SSD_STATE].astype(F32).T.astype(BF16) for g in range(SSD_GROUPS)]
            for v in bc]
    cg = [[v[:, gn + g * SSD_STATE:gn + (g + 1) * SSD_STATE] for g in range(SSD_GROUPS)] for v in bc]
    cb = [[jnp.dot(cg[n][g], bg_t[n][g], preferred_element_type=F32) for g in range(SSD_GROUPS)]
          for n in range(len(jobs))]

    def y_diagonal(n, d, g):
        parts = []
        for pp in range(HEADS_PER_GROUP // 2):
            p = g * (HEADS_PER_GROUP // 2) + pp
            decay = []
            for h in (2 * p, 2 * p + 1):
                seg = cs[n][:, h:h + 1] - cs_t[n][h:h + 1, :]
                decay.append(cb[n][g] * jnp.where(mask[d], jnp.exp(jnp.minimum(seg, 0.0)), 0.0))
            m = jnp.concatenate(decay, axis=1).astype(BF16)
            xp = xd[n][:, p * pw:(p + 1) * pw]
            x_bd = jnp.concatenate([jnp.where(low_half, xp, zero), jnp.where(low_half, zero, xp)], axis=0)
            parts.append(jnp.dot(m, x_bd, preferred_element_type=F32))
        return jnp.concatenate(parts, axis=1)

    y_diag = [[y_diagonal(n, d, g) for g in range(SSD_GROUPS)] for n, (d, k, r) in enumerate(jobs)]

    for n, (d, k, r) in enumerate(jobs):
        for g in range(SSD_GROUPS):
            glanes = slice(g * gw, (g + 1) * gw)
            st = st_ref[d, g]
            y_off = jnp.dot(cg[n][g], st.astype(BF16), preferred_element_type=F32) * e_cs[n][:, glanes]
            r[4][0, rows(k), glanes] = (y_diag[n][g] + y_off).astype(BF16)
            st_ref[d, g] = st * e_tot[n][:, glanes] + jnp.dot(bg_t[n][g], xd_dec[n][:, glanes],
                                                              preferred_element_type=F32)

    @pl.when(s == ns - 1)
    def _():
        fin_ref[0] = st_ref[...]


def _ssd_call(xs, bc, dt, da, tri, ex, init):
    bsz, seq, _ = xs.shape
    kc = min(SSD_CHUNKS_PER_STEP, seq // SSD_CHUNK)
    rows = kc * SSD_CHUNK
    ns = seq // rows
    st_block = (1,) + SSD_STATE_SHAPE

    def tok(w, rev):
        return pl.BlockSpec((1, rows, w), lambda b, s: (b, ns - 1 - s if rev else s, 0))

    def per_dir(w, d):
        return pl.BlockSpec((1, 1, rows, w), lambda b, s: (b, d, ns - 1 - s if d else s, 0))

    return pl.pallas_call(
        functools.partial(_ssd_kernel, kc=kc),
        grid=(bsz, ns),
        in_specs=[
            tok(SSD_WIDTH, False), tok(BC_WIDTH, False), per_dir(LANE, 0), per_dir(LANE, 0),
            tok(SSD_WIDTH, True), tok(BC_WIDTH, True), per_dir(LANE, 1), per_dir(LANE, 1),
            pl.BlockSpec(tri.shape, lambda b, s: (0, 0, 0)),
            pl.BlockSpec(ex.shape, lambda b, s: (0, 0)),
            pl.BlockSpec(st_block, lambda b, s: (b, 0, 0, 0, 0)),
        ],
        out_specs=[tok(SSD_WIDTH, False), tok(SSD_WIDTH, True), pl.BlockSpec(st_block, lambda b, s: (b, 0, 0, 0, 0))],
        out_shape=[
            jax.ShapeDtypeStruct((bsz, seq, SSD_WIDTH), BF16),
            jax.ShapeDtypeStruct((bsz, seq, SSD_WIDTH), BF16),
            jax.ShapeDtypeStruct((bsz,) + SSD_STATE_SHAPE, F32),
        ],
        scratch_shapes=[pltpu.VMEM(SSD_STATE_SHAPE, F32)],
        compiler_params=_params("parallel", "arbitrary"),
        name="ssd",
    )(xs, bc, dt, da, xs, bc, dt, da, tri, ex, init)


def _dft1_kernel(p_ref, q_ref, m_ref, t_ref, *, n1):
    for j in range(DFT_NB):
        m = m_ref[j]
        rp = _bdot(m, p_ref[0, 0, :, j, :])
        rq = _bdot(m, q_ref[0, 0, :, j, :])
        t_ref[0, 0, j] = (rp[:n1] + rq[n1:]).astype(BF16)
        t_ref[0, 1, j] = (rp[n1:] - rq[:n1]).astype(BF16)


def _dft1_call(pq, mtab):
    bsz, _, seq, w = pq.shape
    n1 = seq // DFT_N2
    pqv = pq.reshape(bsz, 2, n1, DFT_N2, w)
    return pl.pallas_call(
        functools.partial(_dft1_kernel, n1=n1),
        grid=(bsz, DFT_N2 // DFT_NB),
        in_specs=[
            pl.BlockSpec((1, 1, n1, DFT_NB, w), lambda b, j: (b, 0, 0, j, 0)),
            pl.BlockSpec((1, 1, n1, DFT_NB, w), lambda b, j: (b, 1, 0, j, 0)),
            pl.BlockSpec((DFT_NB, 2 * n1, n1), lambda b, j: (j, 0, 0)),
        ],
        out_specs=pl.BlockSpec((1, 2, DFT_NB, n1, w), lambda b, j: (b, 0, j, 0, 0)),
        out_shape=jax.ShapeDtypeStruct((bsz, 2, DFT_N2, n1, w), BF16),
        compiler_params=_params("parallel", "parallel"),
        name="dft1",
    )(pqv, pqv, mtab)


def _dft2_kernel(d_ref, t_ref, o_ref):
    o_ref[0] = _bdot(d_ref[...], t_ref[0]).astype(BF16)


def _dft2_call(dmat, t, *, tn):
    bsz, k, n = t.shape
    mo = dmat.shape[0]
    return pl.pallas_call(
        _dft2_kernel,
        grid=(bsz, n // tn),
        in_specs=[
            pl.BlockSpec((mo, k), lambda b, j: (0, 0)),
            pl.BlockSpec((1, k, tn), lambda b, j: (b, 0, j)),
        ],
        out_specs=pl.BlockSpec((1, mo, tn), lambda b, j: (b, 0, j)),
        out_shape=jax.ShapeDtypeStruct((bsz, mo, n), BF16),
        compiler_params=_params("parallel", "parallel"),
        name="dft2",
    )(dmat, t)


def _mix_kernel(yf_ref, yb_ref, xs_ref, z_ref, yfn_ref, ypl_ref, x_ref, dsk_ref, nw_ref, wo_ref, g1_ref,
                lg_ref, lb_ref, sh2_ref, sc2_ref, x1_ref, h2_ref):
    y = yf_ref[0].astype(F32) + yb_ref[0].astype(F32) + xs_ref[0].astype(F32) * dsk_ref[...]
    z = z_ref[0].astype(F32)
    y = y * (z * jax.nn.sigmoid(z))
    gw = SSD_WIDTH // SSD_GROUPS
    parts = []
    for g in range(SSD_GROUPS):
        yg = y[:, g * gw:(g + 1) * gw]
        ms = jnp.mean(yg * yg, axis=-1, keepdims=True)
        parts.append(yg * lax.rsqrt(ms + EPS))
    yn = jnp.concatenate(parts, axis=-1) * nw_ref[...]
    o_f = SSD_WIDTH
    o_p = SSD_WIDTH + FNET_WIDTH
    mix = (_bdot(yn, wo_ref[0:o_f, :]) + _bdot(yfn_ref[0], wo_ref[o_f:o_p, :])
           + _bdot(ypl_ref[0], wo_ref[o_p:, :]))
    x1 = _ln(ALPHA * x_ref[0] + g1_ref[0] * mix) * lg_ref[...] + lb_ref[...]
    x1_ref[0] = x1
    h2_ref[0] = (_ln(x1) * (1.0 + sc2_ref[0]) + sh2_ref[0]).astype(BF16)


def _mix_call(yf, yb, xs, z, yfn, ypl, x, dsk, nw, wo, g1, lg, lb, sh2, sc2, *, tm):
    bsz, seq, d = x.shape
    mod_map = (lambda b, i: (b, 0, 0)) if g1.shape[0] > 1 else (lambda b, i: (0, 0, 0))

    def full(a):
        return pl.BlockSpec(a.shape, lambda b, i: (0,) * a.ndim)

    def tok(w):
        return pl.BlockSpec((1, tm, w), lambda b, i: (b, i, 0))

    mod = pl.BlockSpec((1, 1, d), mod_map)
    return pl.pallas_call(
        _mix_kernel,
        grid=(bsz, seq // tm),
        in_specs=[
            tok(SSD_WIDTH), tok(SSD_WIDTH), tok(SSD_WIDTH), tok(SSD_WIDTH), tok(FNET_WIDTH), tok(POOL_WIDTH), tok(d),
            full(dsk), full(nw), full(wo), mod, full(lg), full(lb), mod, mod,
        ],
        out_specs=[tok(d), tok(d)],
        out_shape=[jax.ShapeDtypeStruct((bsz, seq, d), F32), jax.ShapeDtypeStruct((bsz, seq, d), BF16)],
        compiler_params=_params("parallel", "parallel"),
        name="mix",
    )(yf, yb, xs, z, yfn, ypl, x, dsk, nw, wo, g1, lg, lb, sh2, sc2)


def _ffn_kernel(h_ref, hp_ref, hn_ref, wu_ref, cw_ref, cb_ref, wd_ref, x1_ref, g2_ref, lg_ref, lb_ref, o_ref,
                hbuf, a0, a1, hid, acc, *, tm, on_grid):
    i = pl.program_id(1)
    nt = pl.num_programs(1)
    nj = D_FF // FFN_CB
    ext = tm + 2 * FFN_HALO
    ra = ext // FFN_UP_CHUNKS
    rb = tm // FFN_ROW_CHUNKS
    slots = (a0, a1)
    base = FFN_PAD + FFN_HALO
    rows = (-1, 0, 1) if on_grid else (0,)

    def block_cols(j):
        return slice(j * FFN_CB, (j + 1) * FFN_CB), slice(D_FF + j * FFN_CB, D_FF + (j + 1) * FFN_CB)

    def paired(ref, j):
        return jnp.concatenate([ref[:, cols] for cols in block_cols(j)], axis=1)

    def up_project(a_ref, j):
        for r in range(FFN_UP_CHUNKS):
            hb = hbuf[r * ra:(r + 1) * ra, :]
            for half, cols in enumerate(block_cols(j)):
                a = jnp.dot(hb, wu_ref[:, cols], preferred_element_type=F32)
                a_ref[FFN_PAD + r * ra:FFN_PAD + (r + 1) * ra, half * FFN_CB:(half + 1) * FFN_CB] = a.astype(BF16)

    def conv(a_ref, j, r):
        cw = paired(cw_ref, j).astype(BF16)
        m = FFN_PAD
        n = rb + 2 * m
        start = base + r * rb - m
        col = (lax.broadcasted_iota(jnp.int32, (n, 1), 0) + (r * rb - m)) & (GRID_W - 1)
        taps = {dr: a_ref[pl.ds(start + dr * GRID_W, n), :] for dr in rows}

        def column(dc, lo, hi):
            acc_dc = None
            for dr in rows:
                k = (dr + 1) * 3 + dc + 1
                term = taps[dr][lo:hi] * cw[k:k + 1]
                acc_dc = term if acc_dc is None else acc_dc + term
            return acc_dc

        left = column(-1, 0, n)
        right = column(1, 0, n)
        if on_grid:
            zero = jnp.zeros_like(left)
            left = jnp.where(col == GRID_W - 1, zero, left)
            right = jnp.where(col == 0, zero, right)
        left2 = jnp.concatenate([left[n - 2:], left[:n - 2]], axis=0)
        u = (left2 + right)[m + 1:m + 1 + rb]
        return column(0, m, m + rb) + u + paired(cb_ref, j).astype(BF16)

    hbuf[0:FFN_HALO] = jnp.where(i > 0, hp_ref[0], jnp.zeros_like(hp_ref[0]))
    hbuf[FFN_HALO:FFN_HALO + tm] = h_ref[0]
    hbuf[FFN_HALO + tm:] = jnp.where(i < nt - 1, hn_ref[0], jnp.zeros_like(hn_ref[0]))
    for a in slots:
        a[0:FFN_PAD] = jnp.zeros((FFN_PAD, 2 * FFN_CB), BF16)
        a[FFN_PAD + ext:] = jnp.zeros((FFN_PAD, 2 * FFN_CB), BF16)

    up_project(slots[0], 0)
    for j in range(nj):
        if j + 1 < nj:
            up_project(slots[(j + 1) % 2], j + 1)
        half = j % 2
        for r in range(FFN_ROW_CHUNKS):
            c = conv(slots[half], j, r)
            hidden = jax.nn.gelu(c[:, FFN_CB:], approximate=True) * c[:, :FFN_CB]
            hid[r * rb:(r + 1) * rb, half * FFN_CB:(half + 1) * FFN_CB] = hidden
        if half == 1 or j == nj - 1:
            first = j - half
            k = (half + 1) * FFN_CB
            part = jnp.dot(hid[:, 0:k], wd_ref[first * FFN_CB:first * FFN_CB + k, :], preferred_element_type=F32)
            if first == 0:
                acc[...] = part
            else:
                acc[...] += part

    o_ref[0] = _ln(ALPHA * x1_ref[0] + g2_ref[0] * acc[...]) * lg_ref[...] + lb_ref[...]


def _ffn_call(h, w_up, conv_w9, conv_b, w_down, x1, g2, lg, lb, *, tm, on_grid):
    bsz, seq, d = h.shape
    nt = seq // tm
    th = tm // FFN_HALO
    nh = seq // FFN_HALO
    ext = tm + 2 * FFN_HALO
    mod_map = (lambda b, i: (b, 0, 0)) if g2.shape[0] > 1 else (lambda b, i: (0, 0, 0))

    def resident(a):
        return pl.BlockSpec(a.shape, lambda b, i: (0,) * a.ndim, pipeline_mode=pl.Buffered(1))

    a_scratch = pltpu.VMEM((ext + 2 * FFN_PAD, 2 * FFN_CB), BF16)
    return pl.pallas_call(
        functools.partial(_ffn_kernel, tm=tm, on_grid=on_grid),
        grid=(bsz, nt),
        in_specs=[
            pl.BlockSpec((1, tm, d), lambda b, i: (b, i, 0)),
            pl.BlockSpec((1, FFN_HALO, d), lambda b, i: (b, jnp.maximum(i * th - 1, 0), 0)),
            pl.BlockSpec((1, FFN_HALO, d), lambda b, i: (b, jnp.minimum((i + 1) * th, nh - 1), 0)),
            resident(w_up), resident(conv_w9), resident(conv_b), resident(w_down),
            pl.BlockSpec((1, tm, d), lambda b, i: (b, i, 0)),
            pl.BlockSpec((1, 1, d), mod_map),
            resident(lg), resident(lb),
        ],
        out_specs=pl.BlockSpec((1, tm, d), lambda b, i: (b, i, 0)),
        out_shape=jax.ShapeDtypeStruct((bsz, seq, d), F32),
        scratch_shapes=[pltpu.VMEM((ext, d), BF16), a_scratch, a_scratch, pltpu.VMEM((tm, 2 * FFN_CB), BF16),
                        pltpu.VMEM((tm, d), F32)],
        compiler_params=_params("parallel", "arbitrary"),
        name="ffn",
    )(h, h, h, w_up, conv_w9, conv_b, w_down, x1, g2, lg, lb)


def _block_diag(blocks):
    g, n, m = blocks.shape
    out = jnp.zeros((g * n, g * m), blocks.dtype)
    for k in range(g):
        out = out.at[k * n:(k + 1) * n, k * m:(k + 1) * m].set(blocks[k])
    return out


def _dft_tables(seq):
    n1 = seq // DFT_N2
    k1 = np.arange(n1, dtype=np.int64)[:, None]
    nn1 = np.arange(n1, dtype=np.int64)[None, :]
    tabs = []
    for n2 in range(DFT_N2):
        th = 2.0 * np.pi * ((k1 * (DFT_N2 * nn1 + n2)) % seq) / seq
        tabs.append(np.concatenate([np.cos(th), -np.sin(th)], axis=0))
    mtab = np.stack(tabs)
    k2 = np.arange(DFT_N2, dtype=np.int64)[:, None]
    nn2 = np.arange(DFT_N2, dtype=np.int64)[None, :]
    th2 = 2.0 * np.pi * ((k2 * nn2) % DFT_N2) / DFT_N2
    scale = 1.0 / math.sqrt(seq * FNET_GDIM)
    dmat = scale * np.concatenate([np.cos(th2), np.sin(th2)], axis=1)
    return jnp.asarray(mtab, F32), jnp.asarray(dmat, F32)


def _dense_dft_table(seq):
    k = np.arange(seq, dtype=np.int64)
    th = 2.0 * np.pi * ((k[:, None] * k[None, :]) % seq) / seq
    scale = 1.0 / math.sqrt(seq * FNET_GDIM)
    return jnp.asarray(scale * np.concatenate([np.cos(th), -np.sin(th)], axis=1), F32)


def _channel_dft_tables():
    k = np.arange(FNET_GDIM, dtype=np.int64)
    th = 2.0 * np.pi * ((k[:, None] * k[None, :]) % FNET_GDIM) / FNET_GDIM
    eye = np.eye(FNET_GROUPS)
    return jnp.asarray(np.kron(eye, np.cos(th)), F32), jnp.asarray(np.kron(eye, np.sin(th)), F32)


def _pool_inverse_counts(seq):
    pos = np.arange(seq, dtype=np.int64)
    cols = []
    for win in POOL_WINDOWS:
        left = win // 2
        cnt = np.minimum(pos + (win - left), seq) - np.maximum(pos - left, 0)
        cols.append(np.repeat((1.0 / cnt)[:, None], POOL_GDIM, axis=1))
    return jnp.asarray(np.concatenate(cols, axis=1), F32)


def _scan_matrices():
    r = np.arange(SSD_CHUNK)
    lower = (r[:, None] >= r[None, :]).astype(np.float32)
    return jnp.asarray(np.stack([lower, lower.T]), F32)


def _head_expansion():
    ex = np.zeros((LANE, SSD_WIDTH), np.float32)
    for h in range(SSD_HEADS):
        ex[h, h * SSD_HEADDIM:(h + 1) * SSD_HEADDIM] = 1.0
    return jnp.asarray(ex, BF16)


def _token_mixer(x, sh1, sc1, lw, init, tri, ex, *, tm):
    bsz, seq, _ = x.shape
    z, xs, bc, dt, da, pq, ypl = _inproj_call(
        x, sh1, sc1, lw["w_cat"], lw["ssd_conv_w"], lw["ssd_conv_b"], lw["dt_bias"], lw["a_log"], lw["fmix"],
        lw["pool_bd"], lw["pool_scale"], _pool_inverse_counts(seq), tm=tm)
    yf, yb, fin = _ssd_call(xs, bc, dt, da, tri, ex, init)
    return z, xs, yf, yb, pq, ypl, fin


def _fourier_positions(pq):
    bsz, _, seq, w = pq.shape
    if seq % (DFT_N2 * 8) == 0:
        n1 = seq // DFT_N2
        mtab, dmat = _dft_tables(seq)
        t = _dft1_call(pq, mtab).reshape(bsz, 2 * DFT_N2, n1 * w)
        out = _dft2_call(dmat, t, tn=min(4096, n1 * w))
        return out.reshape(bsz, seq, w)
    return _dft2_call(_dense_dft_table(seq), pq.reshape(bsz, 2 * seq, w), tn=w)


def kernel(x, c, ctx, c_ctx, w_ada, b_ada, w_in, ssd_conv_w, ssd_conv_b, ssd_dt_bias, ssd_a_log, ssd_d,
           ssd_norm_w, fnet_w, pool_w, pool_scale, w_out, ln1_g, ln1_b, ffn_w_up, ffn_conv_w, ffn_conv_b,
           ffn_w_down, ln2_g, ln2_b):
    bsz, seq, d = x.shape
    n_ctx = ctx.shape[1]
    assert d == D_MODEL and bsz <= 7 and seq % (GRID_W * 8) == 0 and n_ctx % SSD_CHUNK == 0
    tm_lat = min(512, seq)
    tm_ffn = min(1024, seq)

    cv = jnp.zeros((8, d), F32).at[0:bsz].set(c).at[bsz].set(c_ctx)
    mods = _ada_call(cv, w_ada, b_ada)
    tri = _scan_matrices()
    ex = _head_expansion()
    fmix = _fmix_call(*_channel_dft_tables(), jnp.stack([_block_diag(fnet_w[l]) for l in range(DEPTH)]))
    zero_state = jnp.zeros((bsz,) + SSD_STATE_SHAPE, F32)
    lane_pad = jnp.zeros((d, LANE - SSD_HEADS), F32)

    for l in range(DEPTH):
        last = l == DEPTH - 1
        m = mods[l].reshape(8, 6, d)
        lat = [m[0:bsz, k][:, None, :] for k in range(6)]
        cx = [m[bsz:bsz + 1, k][:, None, :] for k in range(6)]
        w = w_in[l]
        w_cat = jnp.concatenate(
            [w[:, 0:OFF_DT], w[:, OFF_FNET:N_IN], w[:, OFF_DT:OFF_DT + SSD_HEADS], lane_pad,
             w[:, OFF_DT + SSD_HEADS:OFF_FNET], lane_pad], axis=1).astype(BF16)
        pad16 = ((0, 0), (0, LANE - SSD_HEADS))
        lw = {
            "w_cat": w_cat,
            "ssd_conv_w": ssd_conv_w[l],
            "ssd_conv_b": ssd_conv_b[l][None, :],
            "dt_bias": jnp.pad(ssd_dt_bias[l], pad16),
            "a_log": jnp.pad(ssd_a_log[l], pad16),
            "fmix": fmix[l],
            "pool_bd": _block_diag(pool_w[l]).astype(BF16),
            "pool_scale": pool_scale[l][None, :],
        }
        dsk = jnp.repeat(ssd_d[l], SSD_HEADDIM)[None, :]
        nw = ssd_norm_w[l][None, :]
        wo = w_out[l].astype(BF16)
        lg1, lb1 = ln1_g[l][None, :], ln1_b[l][None, :]
        lg2, lb2 = ln2_g[l][None, :], ln2_b[l][None, :]
        w_up = ffn_w_up[l].astype(BF16)
        w_down = ffn_w_down[l].astype(BF16)
        conv9 = ffn_conv_w[l].reshape(9, 2 * D_FF)
        conv_b = ffn_conv_b[l][None, :]

        cz, cxs, cyf, cyb, cpq, cypl, c_fin = _token_mixer(ctx, cx[0], cx[1], lw, zero_state, tri, ex, tm=n_ctx)
        z, xs, yf, yb, pq, ypl, _ = _token_mixer(x, lat[0], lat[1], lw, c_fin, tri, ex, tm=tm_lat)

        yfn = _fourier_positions(pq)
        x1, h2 = _mix_call(yf, yb, xs, z, yfn, ypl, x, dsk, nw, wo, lat[2], lg1, lb1, lat[3], lat[4], tm=tm_lat)
        x = _ffn_call(h2, w_up, conv9, conv_b, w_down, x1, lat[5], lg2, lb2, tm=tm_ffn, on_grid=True)

        if not last:
            cyfn = _fourier_positions(cpq)
            c1, ch2 = _mix_call(cyf, cyb, cxs, cz, cyfn, cypl, ctx, dsk, nw, wo, cx[2], lg1, lb1, cx[3], cx[4],
                                tm=n_ctx)
            ctx = _ffn_call(ch2, w_up, conv9, conv_b, w_down, c1, cx[5], lg2, lb2, tm=n_ctx, on_grid=False)
    return x
```

```python
import functools
import math

import numpy as np
import jax
import jax.numpy as jnp
from jax import lax
from jax.experimental import pallas as pl
from jax.experimental.pallas import tpu as pltpu

F32 = jnp.float32
BF16 = jnp.bfloat16
HIGHEST = lax.Precision.HIGHEST

D_MODEL = 1024
DEPTH = 2
GRID_W = 64
SSD_WIDTH = 512
SSD_HEADDIM = 64
SSD_HEADS = 8
SSD_GROUPS = 2
HEADS_PER_GROUP = SSD_HEADS // SSD_GROUPS
SSD_STATE = 128
SSD_CHUNK = 128
SSD_CHUNKS_PER_STEP = 8
XBC_WIDTH = SSD_WIDTH + 2 * SSD_GROUPS * SSD_STATE
BC_WIDTH = 2 * SSD_GROUPS * SSD_STATE
SSD_STATE_SHAPE = (2, SSD_GROUPS, SSD_STATE, HEADS_PER_GROUP * SSD_HEADDIM)
FNET_WIDTH = 256
FNET_GROUPS = 4
FNET_GDIM = 64
POOL_WINDOWS = (2, 4, 8, 16)
POOL_WIDTH = 256
POOL_GDIM = 64
POOL_HALO = 8
D_FF = 2816
ALPHA = (2.0 * DEPTH) ** 0.25
EPS = 1e-6

OFF_XBC = SSD_WIDTH
OFF_DT = OFF_XBC + XBC_WIDTH
OFF_FNET = OFF_DT + 2 * SSD_HEADS
OFF_POOL = OFF_FNET + FNET_WIDTH
N_IN = OFF_POOL + POOL_WIDTH

LANE = 128
C_Z = 0
C_XBC = C_Z + SSD_WIDTH
C_FNET = C_XBC + XBC_WIDTH
C_POOL = C_FNET + FNET_WIDTH
C_DT = C_POOL + POOL_WIDTH
N_CAT = C_DT + 2 * LANE

DFT_N2 = 64
DFT_NB = 8
FFN_CB = 256
FFN_HALO = GRID_W
FFN_PAD = 16
FFN_ROW_CHUNKS = 2
FFN_UP_CHUNKS = 1
VMEM_LIMIT = 56 * 1024 * 1024


def _ln(x):
    mu = jnp.mean(x, axis=-1, keepdims=True)
    xc = x - mu
    var = jnp.mean(xc * xc, axis=-1, keepdims=True)
    return xc * lax.rsqrt(var + EPS)


def _bdot(a, b):
    return jnp.dot(a.astype(BF16), b.astype(BF16), preferred_element_type=F32)


def _params(*sem):
    return pltpu.CompilerParams(dimension_semantics=sem, vmem_limit_bytes=VMEM_LIMIT)


def _ada_kernel(c_ref, w_ref, b_ref, o_ref):
    cv = c_ref[...]
    s = cv * jax.nn.sigmoid(cv)
    o_ref[0] = jnp.dot(s, w_ref[0], preferred_element_type=F32, precision=HIGHEST) + b_ref[0]


def _ada_call(cv, w_ada, b_ada):
    depth, d, n = w_ada.shape
    tn = 1536
    return pl.pallas_call(
        _ada_kernel,
        grid=(depth, n // tn),
        in_specs=[
            pl.BlockSpec((8, d), lambda l, j: (0, 0)),
            pl.BlockSpec((1, d, tn), lambda l, j: (l, 0, j)),
            pl.BlockSpec((1, 1, tn), lambda l, j: (l, 0, j)),
        ],
        out_specs=pl.BlockSpec((1, 8, tn), lambda l, j: (l, 0, j)),
        out_shape=jax.ShapeDtypeStruct((depth, 8, n), F32),
        compiler_params=_params("parallel", "parallel"),
        name="ada",
    )(cv, w_ada, b_ada.reshape(depth, 1, n))


def _fmix_kernel(cc_ref, cs_ref, fw_ref, o_ref):
    fw = fw_ref[0]
    o_ref[0, 0] = jnp.dot(cc_ref[...], fw, preferred_element_type=F32, precision=HIGHEST).astype(BF16)
    o_ref[0, 1] = jnp.dot(cs_ref[...], fw, preferred_element_type=F32, precision=HIGHEST).astype(BF16)


def _fmix_call(cc, cs, fw_bd):
    depth, w, _ = fw_bd.shape
    return pl.pallas_call(
        _fmix_kernel,
        grid=(depth,),
        in_specs=[
            pl.BlockSpec((w, w), lambda l: (0, 0)),
            pl.BlockSpec((w, w), lambda l: (0, 0)),
            pl.BlockSpec((1, w, w), lambda l: (l, 0, 0)),
        ],
        out_specs=pl.BlockSpec((1, 2, w, w), lambda l: (l, 0, 0, 0)),
        out_shape=jax.ShapeDtypeStruct((depth, 2, w, w), BF16),
        compiler_params=_params("parallel"),
        name="fmix",
    )(cc, cs, fw_bd)


def _inproj_kernel(x_ref, xp_ref, xn_ref, sh_ref, sc_ref, w_ref, cw_ref, cb_ref, dtb_ref, alog_ref,
                   fm_ref, pw_ref, ps_ref, inv_ref,
                   z_ref, xs_ref, bc_ref, dt_ref, da_ref, pq_ref, yp_ref, *, tm, pq_blocked):
    i = pl.program_id(1)
    nt = pl.num_programs(1)
    sh = sh_ref[0]
    sc = sc_ref[0]

    def modulated(xv):
        return (_ln(xv) * (1.0 + sc) + sh).astype(BF16)

    x_ext = jnp.concatenate([xp_ref[0], x_ref[0], xn_ref[0]], axis=0)
    u_ext = jnp.dot(modulated(x_ext), w_ref[...], preferred_element_type=F32)
    u = u_ext[POOL_HALO:POOL_HALO + tm]
    up = jnp.where(i > 0, u_ext[0:POOL_HALO, C_XBC:C_DT], 0.0)
    un = jnp.where(i < nt - 1, u_ext[POOL_HALO + tm:, C_XBC:C_DT], 0.0)

    z_ref[0] = u[:, C_Z:C_XBC].astype(BF16)

    xbc = u[:, C_XBC:C_FNET]
    row = lax.broadcasted_iota(jnp.int32, (tm, 1), 0)
    prev_row = up[POOL_HALO - 1:POOL_HALO, 0:XBC_WIDTH]
    next_row = un[0:1, 0:XBC_WIDTH]
    xm1 = jnp.where(row == 0, prev_row, pltpu.roll(xbc, 1, 0))
    xp1 = jnp.where(row == tm - 1, next_row, pltpu.roll(xbc, tm - 1, 0))
    cw = cw_ref[...]
    conv = xm1 * cw[0:1] + xbc * cw[1:2] + xp1 * cw[2:3] + cb_ref[...]
    act = conv * jax.nn.sigmoid(conv)
    xs_ref[0] = act[:, 0:SSD_WIDTH].astype(BF16)
    bc_ref[0] = act[:, SSD_WIDTH:XBC_WIDTH].astype(BF16)

    lane = lax.broadcasted_iota(jnp.int32, (1, LANE), 1)
    for d in range(2):
        raw = u[:, C_DT + d * LANE:C_DT + (d + 1) * LANE] + dtb_ref[d:d + 1, :]
        sp = jnp.maximum(raw, 0.0) + jnp.log1p(jnp.exp(-jnp.abs(raw)))
        dt = jnp.where(lane < SSD_HEADS, sp, 0.0)
        dt_ref[0, d] = dt
        da_ref[0, d] = dt * (-jnp.exp(alog_ref[d:d + 1, :]))

    uf = u[:, C_FNET:C_POOL].astype(BF16)
    for part in range(2):
        pq = jnp.dot(uf, fm_ref[part], preferred_element_type=F32)
        if pq_blocked:
            pq = pq.reshape(tm // DFT_N2, DFT_N2, FNET_WIDTH).reshape(tm // DFT_N2, DFT_N2 * FNET_WIDTH)
        pq_ref[0, part] = pq

    off_pool = C_POOL - C_XBC
    ext = jnp.concatenate([up[:, off_pool:off_pool + POOL_WIDTH], u[:, C_POOL:C_DT],
                           un[:, off_pool:off_pool + POOL_WIDTH]], axis=0)
    n_ext = tm + 2 * POOL_HALO
    sums = {}
    cur = ext
    width = 1
    while width < POOL_WINDOWS[-1]:
        cur = cur + pltpu.roll(cur, n_ext - width, 0)
        width *= 2
        sums[width] = cur
    lane_p = lax.broadcasted_iota(jnp.int32, (1, POOL_WIDTH), 1)
    wsum = None
    for gi, win in enumerate(POOL_WINDOWS):
        start = POOL_HALO - win // 2
        s_w = sums[win]
        if start:
            s_w = pltpu.roll(s_w, n_ext - start, 0)
        s_w = s_w[0:tm]
        wsum = s_w if wsum is None else jnp.where(lane_p >= gi * POOL_GDIM, s_w, wsum)
    pooled = wsum * inv_ref[...] - u[:, C_POOL:C_DT]
    yp_ref[0] = (_bdot(pooled, pw_ref[...]) * ps_ref[...]).astype(BF16)


def _inproj_call(x, sh, sc, w_cat, conv_w, conv_b, dt_bias, a_log, fmix, pw, ps, inv, *, tm, pq_blocked):
    bsz, seq, d = x.shape
    nt = seq // tm
    nb8 = seq // 8
    tb = tm // 8
    mod_map = (lambda b, i: (b, 0, 0)) if sh.shape[0] > 1 else (lambda b, i: (0, 0, 0))

    def full(a):
        return pl.BlockSpec(a.shape, lambda b, i: (0,) * a.ndim)

    in_specs = [
        pl.BlockSpec((1, tm, d), lambda b, i: (b, i, 0)),
        pl.BlockSpec((1, 8, d), lambda b, i: (b, jnp.maximum(i * tb - 1, 0), 0)),
        pl.BlockSpec((1, 8, d), lambda b, i: (b, jnp.minimum((i + 1) * tb, nb8 - 1), 0)),
        pl.BlockSpec((1, 1, d), mod_map),
        pl.BlockSpec((1, 1, d), mod_map),
    ] + [full(a) for a in (w_cat, conv_w, conv_b, dt_bias, a_log, fmix, pw, ps)] + [
        pl.BlockSpec((tm, POOL_WIDTH), lambda b, i: (i, 0))]
    if pq_blocked:
        pq_shape = jax.ShapeDtypeStruct((bsz, 2, seq // DFT_N2, DFT_N2 * FNET_WIDTH), F32)
        pq_spec = pl.BlockSpec((1, 2, tm // DFT_N2, DFT_N2 * FNET_WIDTH), lambda b, i: (b, 0, i, 0))
    else:
        pq_shape = jax.ShapeDtypeStruct((bsz, 2, seq, FNET_WIDTH), F32)
        pq_spec = pl.BlockSpec((1, 2, tm, FNET_WIDTH), lambda b, i: (b, 0, i, 0))
    out_shape = [
        jax.ShapeDtypeStruct((bsz, seq, SSD_WIDTH), BF16),
        jax.ShapeDtypeStruct((bsz, seq, SSD_WIDTH), BF16),
        jax.ShapeDtypeStruct((bsz, seq, BC_WIDTH), BF16),
        jax.ShapeDtypeStruct((bsz, 2, seq, LANE), F32),
        jax.ShapeDtypeStruct((bsz, 2, seq, LANE), F32),
        pq_shape,
        jax.ShapeDtypeStruct((bsz, seq, POOL_WIDTH), BF16),
    ]
    out_specs = [
        pl.BlockSpec((1, tm, SSD_WIDTH), lambda b, i: (b, i, 0)),
        pl.BlockSpec((1, tm, SSD_WIDTH), lambda b, i: (b, i, 0)),
        pl.BlockSpec((1, tm, BC_WIDTH), lambda b, i: (b, i, 0)),
        pl.BlockSpec((1, 2, tm, LANE), lambda b, i: (b, 0, i, 0)),
        pl.BlockSpec((1, 2, tm, LANE), lambda b, i: (b, 0, i, 0)),
        pq_spec,
        pl.BlockSpec((1, tm, POOL_WIDTH), lambda b, i: (b, i, 0)),
    ]
    return pl.pallas_call(
        functools.partial(_inproj_kernel, tm=tm, pq_blocked=pq_blocked),
        grid=(bsz, nt),
        in_specs=in_specs,
        out_specs=out_specs,
        out_shape=out_shape,
        compiler_params=_params("parallel", "parallel"),
        name="inproj",
    )(x, x, x, sh, sc, w_cat, conv_w, conv_b, dt_bias, a_log, fmix, pw, ps, inv)


def _bf16_pieces(x, terms):
    pieces = []
    rem = x
    for t in range(terms):
        piece = rem.astype(BF16)
        pieces.append(piece)
        if t + 1 < terms:
            rem = rem - piece.astype(F32)
    return pieces


def _ssd_kernel(xsf_ref, bcf_ref, dtf_ref, daf_ref, xsb_ref, bcb_ref, dtb_ref, dab_ref, tri_ref, ex_ref, init_ref,
                yf_ref, yb_ref, fin_ref, st_ref, *, kc):
    s = pl.program_id(1)
    ns = pl.num_programs(1)

    @pl.when(s == 0)
    def _():
        st_ref[...] = init_ref[0]

    fwd = (xsf_ref, bcf_ref, dtf_ref, daf_ref, yf_ref)
    bwd = (xsb_ref, bcb_ref, dtb_ref, dab_ref, yb_ref)
    jobs = [(0, k, fwd) for k in range(kc)] + [(1, kc - 1 - k, bwd) for k in range(kc)]
    c = SSD_CHUNK
    gn = SSD_GROUPS * SSD_STATE
    pw = 2 * SSD_HEADDIM
    gw = HEADS_PER_GROUP * SSD_HEADDIM
    low_half = lax.broadcasted_iota(jnp.int32, (1, pw), 1) < SSD_HEADDIM
    zero = jnp.zeros((SSD_CHUNK, pw), BF16)
    tri_b = [tri_ref[d].astype(BF16) for d in range(2)]
    mask = [tri_ref[d] > 0.5 for d in range(2)]

    def rows(k):
        return slice(k * c, (k + 1) * c)

    cs3 = [jnp.dot(tri_b[d], jnp.concatenate(_bf16_pieces(r[3][0, 0, rows(k), :], 3), axis=1),
                   preferred_element_type=F32) for d, k, r in jobs]
    cs = [v[:, 0:LANE] + v[:, LANE:2 * LANE] + v[:, 2 * LANE:3 * LANE] for v in cs3]
    cs_t = [v.T for v in cs]
    ex4 = [jnp.dot(jnp.concatenate(_bf16_pieces(cs[n], 2) + _bf16_pieces(r[2][0, 0, rows(k), :], 2), axis=0),
                   ex_ref[...], preferred_element_type=F32) for n, (d, k, r) in enumerate(jobs)]
    cs_x = [v[0:c] + v[c:2 * c] for v in ex4]
    dt_x = [v[2 * c:3 * c] + v[3 * c:4 * c] for v in ex4]
    tot_x = [cs_x[n][(c - 1 if d == 0 else 0):(c if d == 0 else 1), :] for n, (d, k, r) in enumerate(jobs)]
    e_cs = [jnp.exp(v) for v in cs_x]
    e_tot = [jnp.exp(v) for v in tot_x]
    xd32 = [r[0][0, rows(k), :].astype(F32) * dt_x[n] for n, (d, k, r) in enumerate(jobs)]
    xd_dec = [(xd32[n] * jnp.exp(tot_x[n] - cs_x[n])).astype(BF16) for n in range(len(jobs))]
    xd = [v.astype(BF16) for v in xd32]
    bc = [r[1][0, rows(k), :] for d, k, r in jobs]
    bg_t = [[v[:, g * SSD_STATE:(g + 1) * SSD_STATE].astype(F32).T.astype(BF16) for g in range(SSD_GROUPS)]
            for v in bc]
    cg = [[v[:, gn + g * SSD_STATE:gn + (g + 1) * SSD_STATE] for g in range(SSD_GROUPS)] for v in bc]
    cb = [[jnp.dot(cg[n][g], bg_t[n][g], preferred_element_type=F32) for g in range(SSD_GROUPS)]
          for n in range(len(jobs))]

    def y_diagonal(n, d, g):
        parts = []
        for pp in range(HEADS_PER_GROUP // 2):
            p = g * (HEADS_PER_GROUP // 2) + pp
            decay = []
            for h in (2 * p, 2 * p + 1):
                seg = cs[n][:, h:h + 1] - cs_t[n][h:h + 1, :]
                decay.append(cb[n][g] * jnp.where(mask[d], jnp.exp(jnp.minimum(seg, 0.0)), 0.0))
            m = jnp.concatenate(decay, axis=1).astype(BF16)
            xp = xd[n][:, p * pw:(p + 1) * pw]
            x_bd = jnp.concatenate([jnp.where(low_half, xp, zero), jnp.where(low_half, zero, xp)], axis=0)
            parts.append(jnp.dot(m, x_bd, preferred_element_type=F32))
        return jnp.concatenate(parts, axis=1)

    y_diag = [[y_diagonal(n, d, g) for g in range(SSD_GROUPS)] for n, (d, k, r) in enumerate(jobs)]

    for n, (d, k, r) in enumerate(jobs):
        for g in range(SSD_GROUPS):
            glanes = slice(g * gw, (g + 1) * gw)
            st = st_ref[d, g]
            y_off = jnp.dot(cg[n][g], st.astype(BF16), preferred_element_type=F32) * e_cs[n][:, glanes]
            r[4][0, rows(k), glanes] = (y_diag[n][g] + y_off).astype(BF16)
            st_ref[d, g] = st * e_tot[n][:, glanes] + jnp.dot(bg_t[n][g], xd_dec[n][:, glanes],
                                                              preferred_element_type=F32)

    @pl.when(s == ns - 1)
    def _():
        fin_ref[0] = st_ref[...]


def _ssd_call(xs, bc, dt, da, tri, ex, init):
    bsz, seq, _ = xs.shape
    kc = min(SSD_CHUNKS_PER_STEP, seq // SSD_CHUNK)
    rows = kc * SSD_CHUNK
    ns = seq // rows
    st_block = (1,) + SSD_STATE_SHAPE

    def tok(w, rev):
        return pl.BlockSpec((1, rows, w), lambda b, s: (b, ns - 1 - s if rev else s, 0))

    def per_dir(w, d):
        return pl.BlockSpec((1, 1, rows, w), lambda b, s: (b, d, ns - 1 - s if d else s, 0))

    return pl.pallas_call(
        functools.partial(_ssd_kernel, kc=kc),
        grid=(bsz, ns),
        in_specs=[
            tok(SSD_WIDTH, False), tok(BC_WIDTH, False), per_dir(LANE, 0), per_dir(LANE, 0),
            tok(SSD_WIDTH, True), tok(BC_WIDTH, True), per_dir(LANE, 1), per_dir(LANE, 1),
            pl.BlockSpec(tri.shape, lambda b, s: (0, 0, 0)),
            pl.BlockSpec(ex.shape, lambda b, s: (0, 0)),
            pl.BlockSpec(st_block, lambda b, s: (b, 0, 0, 0, 0)),
        ],
        out_specs=[tok(SSD_WIDTH, False), tok(SSD_WIDTH, True), pl.BlockSpec(st_block, lambda b, s: (b, 0, 0, 0, 0))],
        out_shape=[
            jax.ShapeDtypeStruct((bsz, seq, SSD_WIDTH), BF16),
            jax.ShapeDtypeStruct((bsz, seq, SSD_WIDTH), BF16),
            jax.ShapeDtypeStruct((bsz,) + SSD_STATE_SHAPE, F32),
        ],
        scratch_shapes=[pltpu.VMEM(SSD_STATE_SHAPE, F32)],
        compiler_params=_params("parallel", "arbitrary"),
        name="ssd",
    )(xs, bc, dt, da, xs, bc, dt, da, tri, ex, init)


def _dft1_kernel(p_ref, q_ref, m_ref, tr_ref, ti_ref, *, n1):
    re, im = [], []
    for j in range(DFT_NB):
        m = m_ref[j]
        rp = _bdot(m, p_ref[0, 0, :, j * FNET_WIDTH:(j + 1) * FNET_WIDTH])
        rq = _bdot(m, q_ref[0, 0, :, j * FNET_WIDTH:(j + 1) * FNET_WIDTH])
        re.append(rp[:n1] + rq[n1:])
        im.append(rp[n1:] - rq[:n1])
    tr_ref[0] = jnp.stack(re).reshape(DFT_NB, n1 * FNET_WIDTH)
    ti_ref[0] = jnp.stack(im).reshape(DFT_NB, n1 * FNET_WIDTH)


def _dft1_call(pqv, mtab):
    bsz, _, n1, cols_all = pqv.shape
    w = cols_all // DFT_N2
    cols = DFT_NB * w
    t_shape = jax.ShapeDtypeStruct((bsz, DFT_N2, n1 * w), F32)
    t_spec = pl.BlockSpec((1, DFT_NB, n1 * w), lambda b, j: (b, j, 0))
    return pl.pallas_call(
        functools.partial(_dft1_kernel, n1=n1),
        grid=(bsz, DFT_N2 // DFT_NB),
        in_specs=[
            pl.BlockSpec((1, 1, n1, cols), lambda b, j: (b, 0, 0, j)),
            pl.BlockSpec((1, 1, n1, cols), lambda b, j: (b, 1, 0, j)),
            pl.BlockSpec((DFT_NB, 2 * n1, n1), lambda b, j: (j, 0, 0)),
        ],
        out_specs=[t_spec, t_spec],
        out_shape=[t_shape, t_shape],
        compiler_params=_params("parallel", "parallel"),
        name="dft1",
    )(pqv, pqv, mtab)


def _dft2_kernel(d_ref, *refs, split):
    *t_refs, o_ref = refs
    t = jnp.concatenate([r[0] for r in t_refs], axis=0) if len(t_refs) > 1 else t_refs[0][0]
    res = _bdot(d_ref[...], t)
    if split:
        res = res.reshape(res.shape[0], res.shape[1] // split, split)
    o_ref[0] = res.astype(BF16)


def _dft2_call(dmat, parts, *, tn, split=0):
    bsz, _, n = parts[0].shape
    mo = dmat.shape[0]
    if split:
        out_spec = pl.BlockSpec((1, mo, tn // split, split), lambda b, j: (b, 0, j, 0))
        out_shape = jax.ShapeDtypeStruct((bsz, mo, n // split, split), BF16)
    else:
        out_spec = pl.BlockSpec((1, mo, tn), lambda b, j: (b, 0, j))
        out_shape = jax.ShapeDtypeStruct((bsz, mo, n), BF16)
    return pl.pallas_call(
        functools.partial(_dft2_kernel, split=split),
        grid=(bsz, n // tn),
        in_specs=[pl.BlockSpec((mo, dmat.shape[1]), lambda b, j: (0, 0))] + [
            pl.BlockSpec((1, p.shape[1], tn), lambda b, j: (b, 0, j)) for p in parts],
        out_specs=out_spec,
        out_shape=out_shape,
        compiler_params=_params("parallel", "parallel"),
        name="dft2",
    )(dmat, *parts)


def _mix_kernel(yf_ref, yb_ref, xs_ref, z_ref, yfn_ref, ypl_ref, x_ref, dsk_ref, nw_ref, wo_ref, g1_ref,
                lg_ref, lb_ref, sh2_ref, sc2_ref, x1_ref, h2_ref):
    y = yf_ref[0].astype(F32) + yb_ref[0].astype(F32) + xs_ref[0].astype(F32) * dsk_ref[...]
    z = z_ref[0].astype(F32)
    y = y * (z * jax.nn.sigmoid(z))
    gw = SSD_WIDTH // SSD_GROUPS
    parts = []
    for g in range(SSD_GROUPS):
        yg = y[:, g * gw:(g + 1) * gw]
        ms = jnp.mean(yg * yg, axis=-1, keepdims=True)
        parts.append(yg * lax.rsqrt(ms + EPS))
    yn = jnp.concatenate(parts, axis=-1) * nw_ref[...]
    o_f = SSD_WIDTH
    o_p = SSD_WIDTH + FNET_WIDTH
    mix = (_bdot(yn, wo_ref[0:o_f, :]) + _bdot(yfn_ref[0], wo_ref[o_f:o_p, :])
           + _bdot(ypl_ref[0], wo_ref[o_p:, :]))
    x1 = _ln(ALPHA * x_ref[0] + g1_ref[0] * mix) * lg_ref[...] + lb_ref[...]
    x1_ref[0] = x1
    h2_ref[0] = (_ln(x1) * (1.0 + sc2_ref[0]) + sh2_ref[0]).astype(BF16)


def _mix_call(yf, yb, xs, z, yfn, ypl, x, dsk, nw, wo, g1, lg, lb, sh2, sc2, *, tm):
    bsz, seq, d = x.shape
    mod_map = (lambda b, i: (b, 0, 0)) if g1.shape[0] > 1 else (lambda b, i: (0, 0, 0))

    def full(a):
        return pl.BlockSpec(a.shape, lambda b, i: (0,) * a.ndim)

    def tok(w):
        return pl.BlockSpec((1, tm, w), lambda b, i: (b, i, 0))

    mod = pl.BlockSpec((1, 1, d), mod_map)
    return pl.pallas_call(
        _mix_kernel,
        grid=(bsz, seq // tm),
        in_specs=[
            tok(SSD_WIDTH), tok(SSD_WIDTH), tok(SSD_WIDTH), tok(SSD_WIDTH), tok(FNET_WIDTH), tok(POOL_WIDTH), tok(d),
            full(dsk), full(nw), full(wo), mod, full(lg), full(lb), mod, mod,
        ],
        out_specs=[tok(d), tok(d)],
        out_shape=[jax.ShapeDtypeStruct((bsz, seq, d), F32), jax.ShapeDtypeStruct((bsz, seq, d), BF16)],
        compiler_params=_params("parallel", "parallel"),
        name="mix",
    )(yf, yb, xs, z, yfn, ypl, x, dsk, nw, wo, g1, lg, lb, sh2, sc2)


def _ffn_kernel(h_ref, hp_ref, hn_ref, wu_ref, cw_ref, cb_ref, wd_ref, x1_ref, g2_ref, lg_ref, lb_ref, o_ref,
                hbuf, a0, a1, hid, acc, *, tm, on_grid):
    i = pl.program_id(1)
    nt = pl.num_programs(1)
    nj = D_FF // FFN_CB
    ext = tm + 2 * FFN_HALO
    ra = ext // FFN_UP_CHUNKS
    rb = tm // FFN_ROW_CHUNKS
    slots = (a0, a1)
    base = FFN_PAD + FFN_HALO
    rows = (-1, 0, 1) if on_grid else (0,)

    def block_cols(j):
        return slice(j * FFN_CB, (j + 1) * FFN_CB), slice(D_FF + j * FFN_CB, D_FF + (j + 1) * FFN_CB)

    def paired(ref, j):
        return jnp.concatenate([ref[:, cols] for cols in block_cols(j)], axis=1)

    def up_project(a_ref, j):
        for r in range(FFN_UP_CHUNKS):
            hb = hbuf[r * ra:(r + 1) * ra, :]
            for half, cols in enumerate(block_cols(j)):
                a = jnp.dot(hb, wu_ref[:, cols], preferred_element_type=F32)
                a_ref[FFN_PAD + r * ra:FFN_PAD + (r + 1) * ra, half * FFN_CB:(half + 1) * FFN_CB] = a.astype(BF16)

    def conv(a_ref, j, r):
        cw = paired(cw_ref, j).astype(BF16)
        m = FFN_PAD
        n = rb + 2 * m
        start = base + r * rb - m
        col = (lax.broadcasted_iota(jnp.int32, (n, 1), 0) + (r * rb - m)) & (GRID_W - 1)
        taps = {dr: a_ref[pl.ds(start + dr * GRID_W, n), :] for dr in rows}

        def column(dc, lo, hi):
            acc_dc = None
            for dr in rows:
                k = (dr + 1) * 3 + dc + 1
                term = taps[dr][lo:hi] * cw[k:k + 1]
                acc_dc = term if acc_dc is None else acc_dc + term
            return acc_dc

        left = column(-1, 0, n)
        right = column(1, 0, n)
        if on_grid:
            zero = jnp.zeros_like(left)
            left = jnp.where(col == GRID_W - 1, zero, left)
            right = jnp.where(col == 0, zero, right)
        left2 = jnp.concatenate([left[n - 2:], left[:n - 2]], axis=0)
        u = (left2 + right)[m + 1:m + 1 + rb]
        return column(0, m, m + rb) + u + paired(cb_ref, j).astype(BF16)

    hbuf[0:FFN_HALO] = jnp.where(i > 0, hp_ref[0], jnp.zeros_like(hp_ref[0]))
    hbuf[FFN_HALO:FFN_HALO + tm] = h_ref[0]
    hbuf[FFN_HALO + tm:] = jnp.where(i < nt - 1, hn_ref[0], jnp.zeros_like(hn_ref[0]))
    for a in slots:
        a[0:FFN_PAD] = jnp.zeros((FFN_PAD, 2 * FFN_CB), BF16)
        a[FFN_PAD + ext:] = jnp.zeros((FFN_PAD, 2 * FFN_CB), BF16)

    up_project(slots[0], 0)
    for j in range(nj):
        if j + 1 < nj:
            up_project(slots[(j + 1) % 2], j + 1)
        half = j % 2
        for r in range(FFN_ROW_CHUNKS):
            c = conv(slots[half], j, r)
            hidden = jax.nn.gelu(c[:, FFN_CB:], approximate=True) * c[:, :FFN_CB]
            hid[r * rb:(r + 1) * rb, half * FFN_CB:(half + 1) * FFN_CB] = hidden
        if half == 1 or j == nj - 1:
            first = j - half
            k = (half + 1) * FFN_CB
            part = jnp.dot(hid[:, 0:k], wd_ref[first * FFN_CB:first * FFN_CB + k, :], preferred_element_type=F32)
            if first == 0:
                acc[...] = part
            else:
                acc[...] += part

    o_ref[0] = _ln(ALPHA * x1_ref[0] + g2_ref[0] * acc[...]) * lg_ref[...] + lb_ref[...]


def _ffn_call(h, w_up, conv_w9, conv_b, w_down, x1, g2, lg, lb, *, tm, on_grid):
    bsz, seq, d = h.shape
    nt = seq // tm
    th = tm // FFN_HALO
    nh = seq // FFN_HALO
    ext = tm + 2 * FFN_HALO
    mod_map = (lambda b, i: (b, 0, 0)) if g2.shape[0] > 1 else (lambda b, i: (0, 0, 0))

    def resident(a):
        return pl.BlockSpec(a.shape, lambda b, i: (0,) * a.ndim, pipeline_mode=pl.Buffered(1))

    a_scratch = pltpu.VMEM((ext + 2 * FFN_PAD, 2 * FFN_CB), BF16)
    return pl.pallas_call(
        functools.partial(_ffn_kernel, tm=tm, on_grid=on_grid),
        grid=(bsz, nt),
        in_specs=[
            pl.BlockSpec((1, tm, d), lambda b, i: (b, i, 0)),
            pl.BlockSpec((1, FFN_HALO, d), lambda b, i: (b, jnp.maximum(i * th - 1, 0), 0)),
            pl.BlockSpec((1, FFN_HALO, d), lambda b, i: (b, jnp.minimum((i + 1) * th, nh - 1), 0)),
            resident(w_up), resident(conv_w9), resident(conv_b), resident(w_down),
            pl.BlockSpec((1, tm, d), lambda b, i: (b, i, 0)),
            pl.BlockSpec((1, 1, d), mod_map),
            resident(lg), resident(lb),
        ],
        out_specs=pl.BlockSpec((1, tm, d), lambda b, i: (b, i, 0)),
        out_shape=jax.ShapeDtypeStruct((bsz, seq, d), F32),
        scratch_shapes=[pltpu.VMEM((ext, d), BF16), a_scratch, a_scratch, pltpu.VMEM((tm, 2 * FFN_CB), BF16),
                        pltpu.VMEM((tm, d), F32)],
        compiler_params=_params("parallel", "arbitrary"),
        name="ffn",
    )(h, h, h, w_up, conv_w9, conv_b, w_down, x1, g2, lg, lb)


def _block_diag(blocks):
    g, n, m = blocks.shape
    out = jnp.zeros((g * n, g * m), blocks.dtype)
    for k in range(g):
        out = out.at[k * n:(k + 1) * n, k * m:(k + 1) * m].set(blocks[k])
    return out


def _dft_tables(seq):
    n1 = seq // DFT_N2
    k1 = np.arange(n1, dtype=np.int64)[:, None]
    nn1 = np.arange(n1, dtype=np.int64)[None, :]
    tabs = []
    for n2 in range(DFT_N2):
        th = 2.0 * np.pi * ((k1 * (DFT_N2 * nn1 + n2)) % seq) / seq
        tabs.append(np.concatenate([np.cos(th), -np.sin(th)], axis=0))
    mtab = np.stack(tabs)
    k2 = np.arange(DFT_N2, dtype=np.int64)[:, None]
    nn2 = np.arange(DFT_N2, dtype=np.int64)[None, :]
    th2 = 2.0 * np.pi * ((k2 * nn2) % DFT_N2) / DFT_N2
    scale = 1.0 / math.sqrt(seq * FNET_GDIM)
    dmat = scale * np.concatenate([np.cos(th2), np.sin(th2)], axis=1)
    return jnp.asarray(mtab, F32), jnp.asarray(dmat, F32)


def _dense_dft_table(seq):
    k = np.arange(seq, dtype=np.int64)
    th = 2.0 * np.pi * ((k[:, None] * k[None, :]) % seq) / seq
    scale = 1.0 / math.sqrt(seq * FNET_GDIM)
    return jnp.asarray(scale * np.concatenate([np.cos(th), -np.sin(th)], axis=1), F32)


def _channel_dft_tables():
    k = np.arange(FNET_GDIM, dtype=np.int64)
    th = 2.0 * np.pi * ((k[:, None] * k[None, :]) % FNET_GDIM) / FNET_GDIM
    eye = np.eye(FNET_GROUPS)
    return jnp.asarray(np.kron(eye, np.cos(th)), F32), jnp.asarray(np.kron(eye, np.sin(th)), F32)


def _pool_inverse_counts(seq):
    pos = np.arange(seq, dtype=np.int64)
    cols = []
    for win in POOL_WINDOWS:
        left = win // 2
        cnt = np.minimum(pos + (win - left), seq) - np.maximum(pos - left, 0)
        cols.append(np.repeat((1.0 / cnt)[:, None], POOL_GDIM, axis=1))
    return jnp.asarray(np.concatenate(cols, axis=1), F32)


def _scan_matrices():
    r = np.arange(SSD_CHUNK)
    lower = (r[:, None] >= r[None, :]).astype(np.float32)
    return jnp.asarray(np.stack([lower, lower.T]), F32)


def _head_expansion():
    ex = np.zeros((LANE, SSD_WIDTH), np.float32)
    for h in range(SSD_HEADS):
        ex[h, h * SSD_HEADDIM:(h + 1) * SSD_HEADDIM] = 1.0
    return jnp.asarray(ex, BF16)


def _token_mixer(x, sh1, sc1, lw, init, tri, ex, *, tm):
    bsz, seq, _ = x.shape
    z, xs, bc, dt, da, pq, ypl = _inproj_call(
        x, sh1, sc1, lw["w_cat"], lw["ssd_conv_w"], lw["ssd_conv_b"], lw["dt_bias"], lw["a_log"], lw["fmix"],
        lw["pool_bd"], lw["pool_scale"], _pool_inverse_counts(seq), tm=tm, pq_blocked=_two_stage_dft(seq))
    yf, yb, fin = _ssd_call(xs, bc, dt, da, tri, ex, init)
    return z, xs, yf, yb, pq, ypl, fin


def _two_stage_dft(seq):
    return seq % (DFT_N2 * 8) == 0


def _fourier_positions(pq, seq):
    bsz, w = pq.shape[0], FNET_WIDTH
    if _two_stage_dft(seq):
        n1 = seq // DFT_N2
        mtab, dmat = _dft_tables(seq)
        t_re, t_im = _dft1_call(pq, mtab)
        out = _dft2_call(dmat, [t_re, t_im], tn=min(4096, n1 * w), split=w)
        return out.reshape(bsz, seq, w)
    return _dft2_call(_dense_dft_table(seq), [pq.reshape(bsz, 2 * seq, w)], tn=w)


def kernel(x, c, ctx, c_ctx, w_ada, b_ada, w_in, ssd_conv_w, ssd_conv_b, ssd_dt_bias, ssd_a_log, ssd_d,
           ssd_norm_w, fnet_w, pool_w, pool_scale, w_out, ln1_g, ln1_b, ffn_w_up, ffn_conv_w, ffn_conv_b,
           ffn_w_down, ln2_g, ln2_b):
    bsz, seq, d = x.shape
    n_ctx = ctx.shape[1]
    assert d == D_MODEL and bsz <= 7 and seq % (GRID_W * 8) == 0 and n_ctx % SSD_CHUNK == 0
    tm_lat = min(1024, seq)
    tm_ffn = min(1024, seq)

    cv = jnp.zeros((8, d), F32).at[0:bsz].set(c).at[bsz].set(c_ctx)
    mods = _ada_call(cv, w_ada, b_ada)
    tri = _scan_matrices()
    ex = _head_expansion()
    fmix = _fmix_call(*_channel_dft_tables(), jnp.stack([_block_diag(fnet_w[l]) for l in range(DEPTH)]))
    zero_state = jnp.zeros((bsz,) + SSD_STATE_SHAPE, F32)
    lane_pad = jnp.zeros((d, LANE - SSD_HEADS), F32)

    for l in range(DEPTH):
        last = l == DEPTH - 1
        m = mods[l].reshape(8, 6, d)
        lat = [m[0:bsz, k][:, None, :] for k in range(6)]
        cx = [m[bsz:bsz + 1, k][:, None, :] for k in range(6)]
        w = w_in[l]
        w_cat = jnp.concatenate(
            [w[:, 0:OFF_DT], w[:, OFF_FNET:N_IN], w[:, OFF_DT:OFF_DT + SSD_HEADS], lane_pad,
             w[:, OFF_DT + SSD_HEADS:OFF_FNET], lane_pad], axis=1).astype(BF16)
        pad16 = ((0, 0), (0, LANE - SSD_HEADS))
        lw = {
            "w_cat": w_cat,
            "ssd_conv_w": ssd_conv_w[l],
            "ssd_conv_b": ssd_conv_b[l][None, :],
            "dt_bias": jnp.pad(ssd_dt_bias[l], pad16),
            "a_log": jnp.pad(ssd_a_log[l], pad16),
            "fmix": fmix[l],
            "pool_bd": _block_diag(pool_w[l]).astype(BF16),
            "pool_scale": pool_scale[l][None, :],
        }
        dsk = jnp.repeat(ssd_d[l], SSD_HEADDIM)[None, :]
        nw = ssd_norm_w[l][None, :]
        wo = w_out[l].astype(BF16)
        lg1, lb1 = ln1_g[l][None, :], ln1_b[l][None, :]
        lg2, lb2 = ln2_g[l][None, :], ln2_b[l][None, :]
        w_up = ffn_w_up[l].astype(BF16)
        w_down = ffn_w_down[l].astype(BF16)
        conv9 = ffn_conv_w[l].reshape(9, 2 * D_FF)
        conv_b = ffn_conv_b[l][None, :]

        cz, cxs, cyf, cyb, cpq, cypl, c_fin = _token_mixer(ctx, cx[0], cx[1], lw, zero_state, tri, ex, tm=n_ctx)
        z, xs, yf, yb, pq, ypl, _ = _token_mixer(x, lat[0], lat[1], lw, c_fin, tri, ex, tm=tm_lat)

        yfn = _fourier_positions(pq, seq)
        x1, h2 = _mix_call(yf, yb, xs, z, yfn, ypl, x, dsk, nw, wo, lat[2], lg1, lb1, lat[3], lat[4], tm=tm_lat)
        x = _ffn_call(h2, w_up, conv9, conv_b, w_down, x1, lat[5], lg2, lb2, tm=tm_ffn, on_grid=True)

        if not last:
            cyfn = _fourier_positions(cpq, n_ctx)
            c1, ch2 = _mix_call(cyf, cyb, cxs, cz, cyfn, cypl, ctx, dsk, nw, wo, cx[2], lg1, lb1, cx[3], cx[4],
                                tm=n_ctx)
            ctx = _ffn_call(ch2, w_up, conv9, conv_b, w_down, c1, cx[5], lg2, lb2, tm=n_ctx, on_grid=False)
    return x
```

```python
import functools
import math

import numpy as np
import jax
import jax.numpy as jnp
from jax import lax
from jax.experimental import pallas as pl
from jax.experimental.pallas import tpu as pltpu

F32 = jnp.float32
BF16 = jnp.bfloat16
HIGHEST = lax.Precision.HIGHEST

D_MODEL = 1024
DEPTH = 2
GRID_W = 64
SSD_WIDTH = 512
SSD_HEADDIM = 64
SSD_HEADS = 8
SSD_GROUPS = 2
HEADS_PER_GROUP = SSD_HEADS // SSD_GROUPS
SSD_STATE = 128
SSD_CHUNK = 128
SSD_CHUNKS_PER_STEP = 8
XBC_WIDTH = SSD_WIDTH + 2 * SSD_GROUPS * SSD_STATE
BC_WIDTH = 2 * SSD_GROUPS * SSD_STATE
SSD_STATE_SHAPE = (2, SSD_GROUPS, SSD_STATE, HEADS_PER_GROUP * SSD_HEADDIM)
FNET_WIDTH = 256
FNET_GROUPS = 4
FNET_GDIM = 64
POOL_WINDOWS = (2, 4, 8, 16)
POOL_WIDTH = 256
POOL_GDIM = 64
POOL_HALO = 8
D_FF = 2816
ALPHA = (2.0 * DEPTH) ** 0.25
EPS = 1e-6

OFF_XBC = SSD_WIDTH
OFF_DT = OFF_XBC + XBC_WIDTH
OFF_FNET = OFF_DT + 2 * SSD_HEADS
OFF_POOL = OFF_FNET + FNET_WIDTH
N_IN = OFF_POOL + POOL_WIDTH

LANE = 128
C_Z = 0
C_XBC = C_Z + SSD_WIDTH
C_FNET = C_XBC + XBC_WIDTH
C_POOL = C_FNET + FNET_WIDTH
C_DT = C_POOL + POOL_WIDTH
N_CAT = C_DT + 2 * LANE

DFT_N2 = 64
DFT_NB = 16
FFN_CB = 256
FFN_HALO = GRID_W
FFN_PAD = 16
FFN_ROW_CHUNKS = 2
FFN_UP_CHUNKS = 1
FFN_DOWN_BLOCKS = 4
VMEM_LIMIT = 56 * 1024 * 1024


def _ln(x):
    mu = jnp.mean(x, axis=-1, keepdims=True)
    xc = x - mu
    var = jnp.mean(xc * xc, axis=-1, keepdims=True)
    return xc * lax.rsqrt(var + EPS)


def _bdot(a, b):
    return jnp.dot(a.astype(BF16), b.astype(BF16), preferred_element_type=F32)


def _params(*sem):
    return pltpu.CompilerParams(dimension_semantics=sem, vmem_limit_bytes=VMEM_LIMIT)


def _ada_kernel(c_ref, w_ref, b_ref, o_ref):
    cv = c_ref[...]
    s = cv * jax.nn.sigmoid(cv)
    o_ref[0] = jnp.dot(s, w_ref[0], preferred_element_type=F32, precision=HIGHEST) + b_ref[0]


def _ada_call(cv, w_ada, b_ada):
    depth, d, n = w_ada.shape
    tn = 1536
    return pl.pallas_call(
        _ada_kernel,
        grid=(depth, n // tn),
        in_specs=[
            pl.BlockSpec((8, d), lambda l, j: (0, 0)),
            pl.BlockSpec((1, d, tn), lambda l, j: (l, 0, j)),
            pl.BlockSpec((1, 1, tn), lambda l, j: (l, 0, j)),
        ],
        out_specs=pl.BlockSpec((1, 8, tn), lambda l, j: (l, 0, j)),
        out_shape=jax.ShapeDtypeStruct((depth, 8, n), F32),
        compiler_params=_params("parallel", "parallel"),
        name="ada",
    )(cv, w_ada, b_ada.reshape(depth, 1, n))


def _fmix_kernel(cc_ref, cs_ref, fw_ref, o_ref):
    fw = fw_ref[0]
    o_ref[0, 0] = jnp.dot(cc_ref[...], fw, preferred_element_type=F32, precision=HIGHEST).astype(BF16)
    o_ref[0, 1] = jnp.dot(cs_ref[...], fw, preferred_element_type=F32, precision=HIGHEST).astype(BF16)


def _fmix_call(cc, cs, fw_bd):
    depth, w, _ = fw_bd.shape
    return pl.pallas_call(
        _fmix_kernel,
        grid=(depth,),
        in_specs=[
            pl.BlockSpec((w, w), lambda l: (0, 0)),
            pl.BlockSpec((w, w), lambda l: (0, 0)),
            pl.BlockSpec((1, w, w), lambda l: (l, 0, 0)),
        ],
        out_specs=pl.BlockSpec((1, 2, w, w), lambda l: (l, 0, 0, 0)),
        out_shape=jax.ShapeDtypeStruct((depth, 2, w, w), BF16),
        compiler_params=_params("parallel"),
        name="fmix",
    )(cc, cs, fw_bd)


def _inproj_kernel(x_ref, xp_ref, xn_ref, sh_ref, sc_ref, w_ref, cw_ref, cb_ref, dtb_ref, alog_ref,
                   fm_ref, pw_ref, ps_ref, inv_ref,
                   z_ref, xs_ref, bc_ref, dt_ref, da_ref, pq_ref, yp_ref, *, tm, pq_blocked):
    i = pl.program_id(1)
    nt = pl.num_programs(1)
    sh = sh_ref[0]
    sc = sc_ref[0]

    def modulated(xv):
        return (_ln(xv) * (1.0 + sc) + sh).astype(BF16)

    x_ext = jnp.concatenate([xp_ref[0], x_ref[0], xn_ref[0]], axis=0)
    u_ext = jnp.dot(modulated(x_ext), w_ref[...], preferred_element_type=F32)
    u = u_ext[POOL_HALO:POOL_HALO + tm]
    up = jnp.where(i > 0, u_ext[0:POOL_HALO, C_XBC:C_DT], 0.0)
    un = jnp.where(i < nt - 1, u_ext[POOL_HALO + tm:, C_XBC:C_DT], 0.0)

    z_ref[0] = u[:, C_Z:C_XBC].astype(BF16)

    xbc = u[:, C_XBC:C_FNET]
    row = lax.broadcasted_iota(jnp.int32, (tm, 1), 0)
    prev_row = up[POOL_HALO - 1:POOL_HALO, 0:XBC_WIDTH]
    next_row = un[0:1, 0:XBC_WIDTH]
    xm1 = jnp.where(row == 0, prev_row, pltpu.roll(xbc, 1, 0))
    xp1 = jnp.where(row == tm - 1, next_row, pltpu.roll(xbc, tm - 1, 0))
    cw = cw_ref[...]
    conv = xm1 * cw[0:1] + xbc * cw[1:2] + xp1 * cw[2:3] + cb_ref[...]
    act = conv * jax.nn.sigmoid(conv)
    xs_ref[0] = act[:, 0:SSD_WIDTH].astype(BF16)
    bc_ref[0] = act[:, SSD_WIDTH:XBC_WIDTH].astype(BF16)

    lane = lax.broadcasted_iota(jnp.int32, (1, LANE), 1)
    for d in range(2):
        raw = u[:, C_DT + d * LANE:C_DT + (d + 1) * LANE] + dtb_ref[d:d + 1, :]
        sp = jnp.maximum(raw, 0.0) + jnp.log1p(jnp.exp(-jnp.abs(raw)))
        dt = jnp.where(lane < SSD_HEADS, sp, 0.0)
        dt_ref[0, d] = dt
        da_ref[0, d] = dt * (-jnp.exp(alog_ref[d:d + 1, :]))

    uf = u[:, C_FNET:C_POOL].astype(BF16)
    for part in range(2):
        pq = jnp.dot(uf, fm_ref[part], preferred_element_type=F32)
        if pq_blocked:
            pq = pq.reshape(tm // DFT_N2, DFT_N2, FNET_WIDTH).reshape(tm // DFT_N2, DFT_N2 * FNET_WIDTH)
        pq_ref[0, part] = pq

    off_pool = C_POOL - C_XBC
    ext = jnp.concatenate([up[:, off_pool:off_pool + POOL_WIDTH], u[:, C_POOL:C_DT],
                           un[:, off_pool:off_pool + POOL_WIDTH]], axis=0)
    n_ext = tm + 2 * POOL_HALO
    sums = {}
    cur = ext
    width = 1
    while width < POOL_WINDOWS[-1]:
        cur = cur + pltpu.roll(cur, n_ext - width, 0)
        width *= 2
        sums[width] = cur
    lane_p = lax.broadcasted_iota(jnp.int32, (1, POOL_WIDTH), 1)
    wsum = None
    for gi, win in enumerate(POOL_WINDOWS):
        start = POOL_HALO - win // 2
        s_w = sums[win]
        if start:
            s_w = pltpu.roll(s_w, n_ext - start, 0)
        s_w = s_w[0:tm]
        wsum = s_w if wsum is None else jnp.where(lane_p >= gi * POOL_GDIM, s_w, wsum)
    pooled = wsum * inv_ref[...] - u[:, C_POOL:C_DT]
    yp_ref[0] = (_bdot(pooled, pw_ref[...]) * ps_ref[...]).astype(BF16)


def _inproj_call(x, sh, sc, w_cat, conv_w, conv_b, dt_bias, a_log, fmix, pw, ps, inv, *, tm, pq_blocked):
    bsz, seq, d = x.shape
    nt = seq // tm
    nb8 = seq // 8
    tb = tm // 8
    mod_map = (lambda b, i: (b, 0, 0)) if sh.shape[0] > 1 else (lambda b, i: (0, 0, 0))

    def full(a):
        return pl.BlockSpec(a.shape, lambda b, i: (0,) * a.ndim)

    in_specs = [
        pl.BlockSpec((1, tm, d), lambda b, i: (b, i, 0)),
        pl.BlockSpec((1, 8, d), lambda b, i: (b, jnp.maximum(i * tb - 1, 0), 0)),
        pl.BlockSpec((1, 8, d), lambda b, i: (b, jnp.minimum((i + 1) * tb, nb8 - 1), 0)),
        pl.BlockSpec((1, 1, d), mod_map),
        pl.BlockSpec((1, 1, d), mod_map),
    ] + [full(a) for a in (w_cat, conv_w, conv_b, dt_bias, a_log, fmix, pw, ps)] + [
        pl.BlockSpec((tm, POOL_WIDTH), lambda b, i: (i, 0))]
    if pq_blocked:
        pq_shape = jax.ShapeDtypeStruct((bsz, 2, seq // DFT_N2, DFT_N2 * FNET_WIDTH), F32)
        pq_spec = pl.BlockSpec((1, 2, tm // DFT_N2, DFT_N2 * FNET_WIDTH), lambda b, i: (b, 0, i, 0))
    else:
        pq_shape = jax.ShapeDtypeStruct((bsz, 2, seq, FNET_WIDTH), F32)
        pq_spec = pl.BlockSpec((1, 2, tm, FNET_WIDTH), lambda b, i: (b, 0, i, 0))
    out_shape = [
        jax.ShapeDtypeStruct((bsz, seq, SSD_WIDTH), BF16),
        jax.ShapeDtypeStruct((bsz, seq, SSD_WIDTH), BF16),
        jax.ShapeDtypeStruct((bsz, seq, BC_WIDTH), BF16),
        jax.ShapeDtypeStruct((bsz, 2, seq, LANE), F32),
        jax.ShapeDtypeStruct((bsz, 2, seq, LANE), F32),
        pq_shape,
        jax.ShapeDtypeStruct((bsz, seq, POOL_WIDTH), BF16),
    ]
    out_specs = [
        pl.BlockSpec((1, tm, SSD_WIDTH), lambda b, i: (b, i, 0)),
        pl.BlockSpec((1, tm, SSD_WIDTH), lambda b, i: (b, i, 0)),
        pl.BlockSpec((1, tm, BC_WIDTH), lambda b, i: (b, i, 0)),
        pl.BlockSpec((1, 2, tm, LANE), lambda b, i: (b, 0, i, 0)),
        pl.BlockSpec((1, 2, tm, LANE), lambda b, i: (b, 0, i, 0)),
        pq_spec,
        pl.BlockSpec((1, tm, POOL_WIDTH), lambda b, i: (b, i, 0)),
    ]
    return pl.pallas_call(
        functools.partial(_inproj_kernel, tm=tm, pq_blocked=pq_blocked),
        grid=(bsz, nt),
        in_specs=in_specs,
        out_specs=out_specs,
        out_shape=out_shape,
        compiler_params=_params("parallel", "parallel"),
        name="inproj",
    )(x, x, x, sh, sc, w_cat, conv_w, conv_b, dt_bias, a_log, fmix, pw, ps, inv)


def _bf16_pieces(x, terms):
    pieces = []
    rem = x
    for t in range(terms):
        piece = rem.astype(BF16)
        pieces.append(piece)
        if t + 1 < terms:
            rem = rem - piece.astype(F32)
    return pieces


def _ssd_kernel(xsf_ref, bcf_ref, dtf_ref, daf_ref, xsb_ref, bcb_ref, dtb_ref, dab_ref, tri_ref, ex_ref, init_ref,
                yf_ref, yb_ref, fin_ref, st_ref, *, kc):
    s = pl.program_id(1)
    ns = pl.num_programs(1)

    @pl.when(s == 0)
    def _():
        st_ref[...] = init_ref[0]

    fwd = (xsf_ref, bcf_ref, dtf_ref, daf_ref, yf_ref)
    bwd = (xsb_ref, bcb_ref, dtb_ref, dab_ref, yb_ref)
    jobs = [(0, k, fwd) for k in range(kc)] + [(1, kc - 1 - k, bwd) for k in range(kc)]
    c = SSD_CHUNK
    gn = SSD_GROUPS * SSD_STATE
    pw = 2 * SSD_HEADDIM
    gw = HEADS_PER_GROUP * SSD_HEADDIM
    low_half = lax.broadcasted_iota(jnp.int32, (1, pw), 1) < SSD_HEADDIM
    zero = jnp.zeros((SSD_CHUNK, pw), BF16)
    tri_b = [tri_ref[d].astype(BF16) for d in range(2)]
    mask = [tri_ref[d] > 0.5 for d in range(2)]

    def rows(k):
        return slice(k * c, (k + 1) * c)

    cs3 = [jnp.dot(tri_b[d], jnp.concatenate(_bf16_pieces(r[3][0, 0, rows(k), :], 3), axis=1),
                   preferred_element_type=F32) for d, k, r in jobs]
    cs = [v[:, 0:LANE] + v[:, LANE:2 * LANE] + v[:, 2 * LANE:3 * LANE] for v in cs3]
    cs_t = [v.T for v in cs]
    ex4 = [jnp.dot(jnp.concatenate(_bf16_pieces(cs[n], 2) + _bf16_pieces(r[2][0, 0, rows(k), :], 2), axis=0),
                   ex_ref[...], preferred_element_type=F32) for n, (d, k, r) in enumerate(jobs)]
    cs_x = [v[0:c] + v[c:2 * c] for v in ex4]
    dt_x = [v[2 * c:3 * c] + v[3 * c:4 * c] for v in ex4]
    tot_x = [cs_x[n][(c - 1 if d == 0 else 0):(c if d == 0 else 1), :] for n, (d, k, r) in enumerate(jobs)]
    e_cs = [jnp.exp(v) for v in cs_x]
    e_tot = [jnp.exp(v) for v in tot_x]
    xd32 = [r[0][0, rows(k), :].astype(F32) * dt_x[n] for n, (d, k, r) in enumerate(jobs)]
    xd_dec = [(xd32[n] * jnp.exp(tot_x[n] - cs_x[n])).astype(BF16) for n in range(len(jobs))]
    xd = [v.astype(BF16) for v in xd32]
    bc = [r[1][0, rows(k), :] for d, k, r in jobs]
    bg_t = [[v[:, g * SSD_STATE:(g + 1) * SSD_STATE].astype(F32).T.astype(BF16) for g in range(SSD_GROUPS)]
            for v in bc]
    cg = [[v[:, gn + g * SSD_STATE:gn + (g + 1) * SSD_STATE] for g in range(SSD_GROUPS)] for v in bc]
    cb = [[jnp.dot(cg[n][g], bg_t[n][g], preferred_element_type=F32) for g in range(SSD_GROUPS)]
          for n in range(len(jobs))]

    def y_diagonal(n, d, g):
        parts = []
        for pp in range(HEADS_PER_GROUP // 2):
            p = g * (HEADS_PER_GROUP // 2) + pp
            decay = []
            for h in (2 * p, 2 * p + 1):
                seg = cs[n][:, h:h + 1] - cs_t[n][h:h + 1, :]
                decay.append(cb[n][g] * jnp.where(mask[d], jnp.exp(jnp.minimum(seg, 0.0)), 0.0))
            m = jnp.concatenate(decay, axis=1).astype(BF16)
            xp = xd[n][:, p * pw:(p + 1) * pw]
            x_bd = jnp.concatenate([jnp.where(low_half, xp, zero), jnp.where(low_half, zero, xp)], axis=0)
            parts.append(jnp.dot(m, x_bd, preferred_element_type=F32))
        return jnp.concatenate(parts, axis=1)

    y_diag = [[y_diagonal(n, d, g) for g in range(SSD_GROUPS)] for n, (d, k, r) in enumerate(jobs)]

    for n, (d, k, r) in enumerate(jobs):
        for g in range(SSD_GROUPS):
            glanes = slice(g * gw, (g + 1) * gw)
            st = st_ref[d, g]
            y_off = jnp.dot(cg[n][g], st.astype(BF16), preferred_element_type=F32) * e_cs[n][:, glanes]
            r[4][0, rows(k), glanes] = (y_diag[n][g] + y_off).astype(BF16)
            st_ref[d, g] = st * e_tot[n][:, glanes] + jnp.dot(bg_t[n][g], xd_dec[n][:, glanes],
                                                              preferred_element_type=F32)

    @pl.when(s == ns - 1)
    def _():
        fin_ref[0] = st_ref[...]


def _ssd_call(xs, bc, dt, da, tri, ex, init):
    bsz, seq, _ = xs.shape
    kc = min(SSD_CHUNKS_PER_STEP, seq // SSD_CHUNK)
    rows = kc * SSD_CHUNK
    ns = seq // rows
    st_block = (1,) + SSD_STATE_SHAPE

    def tok(w, rev):
        return pl.BlockSpec((1, rows, w), lambda b, s: (b, ns - 1 - s if rev else s, 0))

    def per_dir(w, d):
        return pl.BlockSpec((1, 1, rows, w), lambda b, s: (b, d, ns - 1 - s if d else s, 0))

    return pl.pallas_call(
        functools.partial(_ssd_kernel, kc=kc),
        grid=(bsz, ns),
        in_specs=[
            tok(SSD_WIDTH, False), tok(BC_WIDTH, False), per_dir(LANE, 0), per_dir(LANE, 0),
            tok(SSD_WIDTH, True), tok(BC_WIDTH, True), per_dir(LANE, 1), per_dir(LANE, 1),
            pl.BlockSpec(tri.shape, lambda b, s: (0, 0, 0)),
            pl.BlockSpec(ex.shape, lambda b, s: (0, 0)),
            pl.BlockSpec(st_block, lambda b, s: (b, 0, 0, 0, 0)),
        ],
        out_specs=[tok(SSD_WIDTH, False), tok(SSD_WIDTH, True), pl.BlockSpec(st_block, lambda b, s: (b, 0, 0, 0, 0))],
        out_shape=[
            jax.ShapeDtypeStruct((bsz, seq, SSD_WIDTH), BF16),
            jax.ShapeDtypeStruct((bsz, seq, SSD_WIDTH), BF16),
            jax.ShapeDtypeStruct((bsz,) + SSD_STATE_SHAPE, F32),
        ],
        scratch_shapes=[pltpu.VMEM(SSD_STATE_SHAPE, F32)],
        compiler_params=_params("parallel", "arbitrary"),
        name="ssd",
    )(xs, bc, dt, da, xs, bc, dt, da, tri, ex, init)


def _dft1_kernel(p_ref, q_ref, m_ref, tr_ref, ti_ref, *, n1):
    re, im = [], []
    for j in range(DFT_NB):
        m = m_ref[j]
        rp = _bdot(m, p_ref[0, 0, :, j * FNET_WIDTH:(j + 1) * FNET_WIDTH])
        rq = _bdot(m, q_ref[0, 0, :, j * FNET_WIDTH:(j + 1) * FNET_WIDTH])
        re.append(rp[:n1] + rq[n1:])
        im.append(rp[n1:] - rq[:n1])
    tr_ref[0] = jnp.stack(re).reshape(DFT_NB, n1 * FNET_WIDTH).astype(BF16)
    ti_ref[0] = jnp.stack(im).reshape(DFT_NB, n1 * FNET_WIDTH).astype(BF16)


def _dft1_call(pqv, mtab):
    bsz, _, n1, cols_all = pqv.shape
    w = cols_all // DFT_N2
    cols = DFT_NB * w
    t_shape = jax.ShapeDtypeStruct((bsz, DFT_N2, n1 * w), BF16)
    t_spec = pl.BlockSpec((1, DFT_NB, n1 * w), lambda b, j: (b, j, 0))
    return pl.pallas_call(
        functools.partial(_dft1_kernel, n1=n1),
        grid=(bsz, DFT_N2 // DFT_NB),
        in_specs=[
            pl.BlockSpec((1, 1, n1, cols), lambda b, j: (b, 0, 0, j)),
            pl.BlockSpec((1, 1, n1, cols), lambda b, j: (b, 1, 0, j)),
            pl.BlockSpec((DFT_NB, 2 * n1, n1), lambda b, j: (j, 0, 0)),
        ],
        out_specs=[t_spec, t_spec],
        out_shape=[t_shape, t_shape],
        compiler_params=_params("parallel", "parallel"),
        name="dft1",
    )(pqv, pqv, mtab)


def _dft2_kernel(d_ref, *refs, split):
    *t_refs, o_ref = refs
    t = jnp.concatenate([r[0] for r in t_refs], axis=0) if len(t_refs) > 1 else t_refs[0][0]
    res = _bdot(d_ref[...], t)
    if split:
        res = res.reshape(res.shape[0], res.shape[1] // split, split)
    o_ref[0] = res.astype(BF16)


def _dft2_call(dmat, parts, *, tn, split=0):
    bsz, _, n = parts[0].shape
    mo = dmat.shape[0]
    if split:
        out_spec = pl.BlockSpec((1, mo, tn // split, split), lambda b, j: (b, 0, j, 0))
        out_shape = jax.ShapeDtypeStruct((bsz, mo, n // split, split), BF16)
    else:
        out_spec = pl.BlockSpec((1, mo, tn), lambda b, j: (b, 0, j))
        out_shape = jax.ShapeDtypeStruct((bsz, mo, n), BF16)
    return pl.pallas_call(
        functools.partial(_dft2_kernel, split=split),
        grid=(bsz, n // tn),
        in_specs=[pl.BlockSpec((mo, dmat.shape[1]), lambda b, j: (0, 0))] + [
            pl.BlockSpec((1, p.shape[1], tn), lambda b, j: (b, 0, j)) for p in parts],
        out_specs=out_spec,
        out_shape=out_shape,
        compiler_params=_params("parallel", "parallel"),
        name="dft2",
    )(dmat, *parts)


def _mix_kernel(yf_ref, yb_ref, xs_ref, z_ref, yfn_ref, ypl_ref, x_ref, dsk_ref, nw_ref, wo_ref, g1_ref,
                lg_ref, lb_ref, sh2_ref, sc2_ref, x1_ref, h2_ref):
    y = yf_ref[0].astype(F32) + yb_ref[0].astype(F32) + xs_ref[0].astype(F32) * dsk_ref[...]
    z = z_ref[0].astype(F32)
    y = y * (z * jax.nn.sigmoid(z))
    gw = SSD_WIDTH // SSD_GROUPS
    parts = []
    for g in range(SSD_GROUPS):
        yg = y[:, g * gw:(g + 1) * gw]
        ms = jnp.mean(yg * yg, axis=-1, keepdims=True)
        parts.append(yg * lax.rsqrt(ms + EPS))
    yn = jnp.concatenate(parts, axis=-1) * nw_ref[...]
    o_f = SSD_WIDTH
    o_p = SSD_WIDTH + FNET_WIDTH
    mix = (_bdot(yn, wo_ref[0:o_f, :]) + _bdot(yfn_ref[0], wo_ref[o_f:o_p, :])
           + _bdot(ypl_ref[0], wo_ref[o_p:, :]))
    x1 = _ln(ALPHA * x_ref[0] + g1_ref[0] * mix) * lg_ref[...] + lb_ref[...]
    x1_ref[0] = x1
    h2_ref[0] = (_ln(x1) * (1.0 + sc2_ref[0]) + sh2_ref[0]).astype(BF16)


def _mix_call(yf, yb, xs, z, yfn, ypl, x, dsk, nw, wo, g1, lg, lb, sh2, sc2, *, tm):
    bsz, seq, d = x.shape
    mod_map = (lambda b, i: (b, 0, 0)) if g1.shape[0] > 1 else (lambda b, i: (0, 0, 0))

    def full(a):
        return pl.BlockSpec(a.shape, lambda b, i: (0,) * a.ndim)

    def tok(w):
        return pl.BlockSpec((1, tm, w), lambda b, i: (b, i, 0))

    mod = pl.BlockSpec((1, 1, d), mod_map)
    return pl.pallas_call(
        _mix_kernel,
        grid=(bsz, seq // tm),
        in_specs=[
            tok(SSD_WIDTH), tok(SSD_WIDTH), tok(SSD_WIDTH), tok(SSD_WIDTH), tok(FNET_WIDTH), tok(POOL_WIDTH), tok(d),
            full(dsk), full(nw), full(wo), mod, full(lg), full(lb), mod, mod,
        ],
        out_specs=[tok(d), tok(d)],
        out_shape=[jax.ShapeDtypeStruct((bsz, seq, d), F32), jax.ShapeDtypeStruct((bsz, seq, d), BF16)],
        compiler_params=_params("parallel", "parallel"),
        name="mix",
    )(yf, yb, xs, z, yfn, ypl, x, dsk, nw, wo, g1, lg, lb, sh2, sc2)


def _ffn_kernel(h_ref, hp_ref, hn_ref, wu_ref, cw_ref, cb_ref, wd_ref, x1_ref, g2_ref, lg_ref, lb_ref, o_ref,
                hbuf, a0, a1, hid, acc, *, tm, on_grid):
    i = pl.program_id(1)
    nt = pl.num_programs(1)
    nj = D_FF // FFN_CB
    ext = tm + 2 * FFN_HALO
    ra = ext // FFN_UP_CHUNKS
    rb = tm // FFN_ROW_CHUNKS
    slots = (a0, a1)
    base = FFN_PAD + FFN_HALO
    rows = (-1, 0, 1) if on_grid else (0,)

    def block_cols(j):
        return slice(j * FFN_CB, (j + 1) * FFN_CB), slice(D_FF + j * FFN_CB, D_FF + (j + 1) * FFN_CB)

    def paired(ref, j):
        return jnp.concatenate([ref[:, cols] for cols in block_cols(j)], axis=1)

    def up_project(a_ref, j):
        for r in range(FFN_UP_CHUNKS):
            hb = hbuf[r * ra:(r + 1) * ra, :]
            for half, cols in enumerate(block_cols(j)):
                a = jnp.dot(hb, wu_ref[:, cols], preferred_element_type=F32)
                a_ref[FFN_PAD + r * ra:FFN_PAD + (r + 1) * ra, half * FFN_CB:(half + 1) * FFN_CB] = a.astype(BF16)

    def conv(a_ref, j, r):
        cw = paired(cw_ref, j).astype(BF16)
        m = FFN_PAD
        n = rb + 2 * m
        start = base + r * rb - m
        col = (lax.broadcasted_iota(jnp.int32, (n, 1), 0) + (r * rb - m)) & (GRID_W - 1)
        taps = {dr: a_ref[pl.ds(start + dr * GRID_W, n), :] for dr in rows}

        def column(dc, lo, hi):
            acc_dc = None
            for dr in rows:
                k = (dr + 1) * 3 + dc + 1
                term = taps[dr][lo:hi] * cw[k:k + 1]
                acc_dc = term if acc_dc is None else acc_dc + term
            return acc_dc

        left = column(-1, 0, n)
        right = column(1, 0, n)
        if on_grid:
            zero = jnp.zeros_like(left)
            left = jnp.where(col == GRID_W - 1, zero, left)
            right = jnp.where(col == 0, zero, right)
        left2 = jnp.concatenate([left[n - 2:], left[:n - 2]], axis=0)
        u = (left2 + right)[m + 1:m + 1 + rb]
        return column(0, m, m + rb) + u + paired(cb_ref, j).astype(BF16)

    hbuf[0:FFN_HALO] = jnp.where(i > 0, hp_ref[0], jnp.zeros_like(hp_ref[0]))
    hbuf[FFN_HALO:FFN_HALO + tm] = h_ref[0]
    hbuf[FFN_HALO + tm:] = jnp.where(i < nt - 1, hn_ref[0], jnp.zeros_like(hn_ref[0]))
    for a in slots:
        a[0:FFN_PAD] = jnp.zeros((FFN_PAD, 2 * FFN_CB), BF16)
        a[FFN_PAD + ext:] = jnp.zeros((FFN_PAD, 2 * FFN_CB), BF16)

    up_project(slots[0], 0)
    for j in range(nj):
        if j + 1 < nj:
            up_project(slots[(j + 1) % 2], j + 1)
        pos = j % FFN_DOWN_BLOCKS
        for r in range(FFN_ROW_CHUNKS):
            c = conv(slots[j % 2], j, r)
            hidden = jax.nn.gelu(c[:, FFN_CB:], approximate=True) * c[:, :FFN_CB]
            hid[r * rb:(r + 1) * rb, pos * FFN_CB:(pos + 1) * FFN_CB] = hidden
        if pos == FFN_DOWN_BLOCKS - 1 or j == nj - 1:
            first = j - pos
            k = (pos + 1) * FFN_CB
            part = jnp.dot(hid[:, 0:k], wd_ref[first * FFN_CB:first * FFN_CB + k, :], preferred_element_type=F32)
            if first == 0:
                acc[...] = part
            else:
                acc[...] += part

    o_ref[0] = _ln(ALPHA * x1_ref[0] + g2_ref[0] * acc[...]) * lg_ref[...] + lb_ref[...]


def _ffn_call(h, w_up, conv_w9, conv_b, w_down, x1, g2, lg, lb, *, tm, on_grid):
    bsz, seq, d = h.shape
    nt = seq // tm
    th = tm // FFN_HALO
    nh = seq // FFN_HALO
    ext = tm + 2 * FFN_HALO
    mod_map = (lambda b, i: (b, 0, 0)) if g2.shape[0] > 1 else (lambda b, i: (0, 0, 0))

    def resident(a):
        return pl.BlockSpec(a.shape, lambda b, i: (0,) * a.ndim, pipeline_mode=pl.Buffered(1))

    a_scratch = pltpu.VMEM((ext + 2 * FFN_PAD, 2 * FFN_CB), BF16)
    return pl.pallas_call(
        functools.partial(_ffn_kernel, tm=tm, on_grid=on_grid),
        grid=(bsz, nt),
        in_specs=[
            pl.BlockSpec((1, tm, d), lambda b, i: (b, i, 0)),
            pl.BlockSpec((1, FFN_HALO, d), lambda b, i: (b, jnp.maximum(i * th - 1, 0), 0)),
            pl.BlockSpec((1, FFN_HALO, d), lambda b, i: (b, jnp.minimum((i + 1) * th, nh - 1), 0)),
            resident(w_up), resident(conv_w9), resident(conv_b), resident(w_down),
            pl.BlockSpec((1, tm, d), lambda b, i: (b, i, 0)),
            pl.BlockSpec((1, 1, d), mod_map),
            resident(lg), resident(lb),
        ],
        out_specs=pl.BlockSpec((1, tm, d), lambda b, i: (b, i, 0)),
        out_shape=jax.ShapeDtypeStruct((bsz, seq, d), F32),
        scratch_shapes=[pltpu.VMEM((ext, d), BF16), a_scratch, a_scratch,
                        pltpu.VMEM((tm, FFN_DOWN_BLOCKS * FFN_CB), BF16),
                        pltpu.VMEM((tm, d), F32)],
        compiler_params=_params("parallel", "arbitrary"),
        name="ffn",
    )(h, h, h, w_up, conv_w9, conv_b, w_down, x1, g2, lg, lb)


def _block_diag(blocks):
    g, n, m = blocks.shape
    out = jnp.zeros((g * n, g * m), blocks.dtype)
    for k in range(g):
        out = out.at[k * n:(k + 1) * n, k * m:(k + 1) * m].set(blocks[k])
    return out


def _dft_tables(seq):
    n1 = seq // DFT_N2
    k1 = np.arange(n1, dtype=np.int64)[:, None]
    nn1 = np.arange(n1, dtype=np.int64)[None, :]
    tabs = []
    for n2 in range(DFT_N2):
        th = 2.0 * np.pi * ((k1 * (DFT_N2 * nn1 + n2)) % seq) / seq
        tabs.append(np.concatenate([np.cos(th), -np.sin(th)], axis=0))
    mtab = np.stack(tabs)
    k2 = np.arange(DFT_N2, dtype=np.int64)[:, None]
    nn2 = np.arange(DFT_N2, dtype=np.int64)[None, :]
    th2 = 2.0 * np.pi * ((k2 * nn2) % DFT_N2) / DFT_N2
    scale = 1.0 / math.sqrt(seq * FNET_GDIM)
    dmat = scale * np.concatenate([np.cos(th2), np.sin(th2)], axis=1)
    return jnp.asarray(mtab, F32), jnp.asarray(dmat, F32)


def _dense_dft_table(seq):
    k = np.arange(seq, dtype=np.int64)
    th = 2.0 * np.pi * ((k[:, None] * k[None, :]) % seq) / seq
    scale = 1.0 / math.sqrt(seq * FNET_GDIM)
    return jnp.asarray(scale * np.concatenate([np.cos(th), -np.sin(th)], axis=1), F32)


def _channel_dft_tables():
    k = np.arange(FNET_GDIM, dtype=np.int64)
    th = 2.0 * np.pi * ((k[:, None] * k[None, :]) % FNET_GDIM) / FNET_GDIM
    eye = np.eye(FNET_GROUPS)
    return jnp.asarray(np.kron(eye, np.cos(th)), F32), jnp.asarray(np.kron(eye, np.sin(th)), F32)


def _pool_inverse_counts(seq):
    pos = np.arange(seq, dtype=np.int64)
    cols = []
    for win in POOL_WINDOWS:
        left = win // 2
        cnt = np.minimum(pos + (win - left), seq) - np.maximum(pos - left, 0)
        cols.append(np.repeat((1.0 / cnt)[:, None], POOL_GDIM, axis=1))
    return jnp.asarray(np.concatenate(cols, axis=1), F32)


def _scan_matrices():
    r = np.arange(SSD_CHUNK)
    lower = (r[:, None] >= r[None, :]).astype(np.float32)
    return jnp.asarray(np.stack([lower, lower.T]), F32)


def _head_expansion():
    ex = np.zeros((LANE, SSD_WIDTH), np.float32)
    for h in range(SSD_HEADS):
        ex[h, h * SSD_HEADDIM:(h + 1) * SSD_HEADDIM] = 1.0
    return jnp.asarray(ex, BF16)


def _token_mixer(x, sh1, sc1, lw, init, tri, ex, *, tm):
    bsz, seq, _ = x.shape
    z, xs, bc, dt, da, pq, ypl = _inproj_call(
        x, sh1, sc1, lw["w_cat"], lw["ssd_conv_w"], lw["ssd_conv_b"], lw["dt_bias"], lw["a_log"], lw["fmix"],
        lw["pool_bd"], lw["pool_scale"], _pool_inverse_counts(seq), tm=tm, pq_blocked=_two_stage_dft(seq))
    yf, yb, fin = _ssd_call(xs, bc, dt, da, tri, ex, init)
    return z, xs, yf, yb, pq, ypl, fin


def _two_stage_dft(seq):
    return seq % (DFT_N2 * 8) == 0


def _fourier_positions(pq, seq):
    bsz, w = pq.shape[0], FNET_WIDTH
    if _two_stage_dft(seq):
        n1 = seq // DFT_N2
        mtab, dmat = _dft_tables(seq)
        t_re, t_im = _dft1_call(pq, mtab)
        out = _dft2_call(dmat, [t_re, t_im], tn=min(4096, n1 * w), split=w)
        return out.reshape(bsz, seq, w)
    return _dft2_call(_dense_dft_table(seq), [pq.reshape(bsz, 2 * seq, w)], tn=w)


def kernel(x, c, ctx, c_ctx, w_ada, b_ada, w_in, ssd_conv_w, ssd_conv_b, ssd_dt_bias, ssd_a_log, ssd_d,
           ssd_norm_w, fnet_w, pool_w, pool_scale, w_out, ln1_g, ln1_b, ffn_w_up, ffn_conv_w, ffn_conv_b,
           ffn_w_down, ln2_g, ln2_b):
    bsz, seq, d = x.shape
    n_ctx = ctx.shape[1]
    assert d == D_MODEL and bsz <= 7 and seq % (GRID_W * 8) == 0 and n_ctx % SSD_CHUNK == 0
    tm_lat = min(1024, seq)
    tm_ffn = min(1024, seq)

    cv = jnp.zeros((8, d), F32).at[0:bsz].set(c).at[bsz].set(c_ctx)
    mods = _ada_call(cv, w_ada, b_ada)
    tri = _scan_matrices()
    ex = _head_expansion()
    fmix = _fmix_call(*_channel_dft_tables(), jnp.stack([_block_diag(fnet_w[l]) for l in range(DEPTH)]))
    zero_state = jnp.zeros((bsz,) + SSD_STATE_SHAPE, F32)
    lane_pad = jnp.zeros((d, LANE - SSD_HEADS), F32)

    for l in range(DEPTH):
        last = l == DEPTH - 1
        m = mods[l].reshape(8, 6, d)
        lat = [m[0:bsz, k][:, None, :] for k in range(6)]
        cx = [m[bsz:bsz + 1, k][:, None, :] for k in range(6)]
        w = w_in[l]
        w_cat = jnp.concatenate(
            [w[:, 0:OFF_DT], w[:, OFF_FNET:N_IN], w[:, OFF_DT:OFF_DT + SSD_HEADS], lane_pad,
             w[:, OFF_DT + SSD_HEADS:OFF_FNET], lane_pad], axis=1).astype(BF16)
        pad16 = ((0, 0), (0, LANE - SSD_HEADS))
        lw = {
            "w_cat": w_cat,
            "ssd_conv_w": ssd_conv_w[l],
            "ssd_conv_b": ssd_conv_b[l][None, :],
            "dt_bias": jnp.pad(ssd_dt_bias[l], pad16),
            "a_log": jnp.pad(ssd_a_log[l], pad16),
            "fmix": fmix[l],
            "pool_bd": _block_diag(pool_w[l]).astype(BF16),
            "pool_scale": pool_scale[l][None, :],
        }
        dsk = jnp.repeat(ssd_d[l], SSD_HEADDIM)[None, :]
        nw = ssd_norm_w[l][None, :]
        wo = w_out[l].astype(BF16)
        lg1, lb1 = ln1_g[l][None, :], ln1_b[l][None, :]
        lg2, lb2 = ln2_g[l][None, :], ln2_b[l][None, :]
        w_up = ffn_w_up[l].astype(BF16)
        w_down = ffn_w_down[l].astype(BF16)
        conv9 = ffn_conv_w[l].reshape(9, 2 * D_FF)
        conv_b = ffn_conv_b[l][None, :]

        cz, cxs, cyf, cyb, cpq, cypl, c_fin = _token_mixer(ctx, cx[0], cx[1], lw, zero_state, tri, ex, tm=n_ctx)
        z, xs, yf, yb, pq, ypl, _ = _token_mixer(x, lat[0], lat[1], lw, c_fin, tri, ex, tm=tm_lat)

        yfn = _fourier_positions(pq, seq)
        x1, h2 = _mix_call(yf, yb, xs, z, yfn, ypl, x, dsk, nw, wo, lat[2], lg1, lb1, lat[3], lat[4], tm=tm_lat)
        x = _ffn_call(h2, w_up, conv9, conv_b, w_down, x1, lat[5], lg2, lb2, tm=tm_ffn, on_grid=True)

        if not last:
            cyfn = _fourier_positions(cpq, n_ctx)
            c1, ch2 = _mix_call(cyf, cyb, cxs, cz, cyfn, cypl, ctx, dsk, nw, wo, cx[2], lg1, lb1, cx[3], cx[4],
                                tm=n_ctx)
            ctx = _ffn_call(ch2, w_up, conv9, conv_b, w_down, c1, cx[5], lg2, lb2, tm=n_ctx, on_grid=False)
    return x
```

```python
import functools
import math

import numpy as np
import jax
import jax.numpy as jnp
from jax import lax
from jax.experimental import pallas as pl
from jax.experimental.pallas import tpu as pltpu

F32 = jnp.float32
BF16 = jnp.bfloat16
HIGHEST = lax.Precision.HIGHEST

D_MODEL = 1024
DEPTH = 2
GRID_W = 64
SSD_WIDTH = 512
SSD_HEADDIM = 64
SSD_HEADS = 8
SSD_GROUPS = 2
HEADS_PER_GROUP = SSD_HEADS // SSD_GROUPS
SSD_STATE = 128
SSD_CHUNK = 128
SSD_CHUNKS_PER_STEP = 8
XBC_WIDTH = SSD_WIDTH + 2 * SSD_GROUPS * SSD_STATE
BC_WIDTH = 2 * SSD_GROUPS * SSD_STATE
SSD_STATE_SHAPE = (2, SSD_GROUPS, SSD_STATE, HEADS_PER_GROUP * SSD_HEADDIM)
FNET_WIDTH = 256
FNET_GROUPS = 4
FNET_GDIM = 64
POOL_WINDOWS = (2, 4, 8, 16)
POOL_WIDTH = 256
POOL_GDIM = 64
POOL_HALO = 8
D_FF = 2816
ALPHA = (2.0 * DEPTH) ** 0.25
EPS = 1e-6

OFF_XBC = SSD_WIDTH
OFF_DT = OFF_XBC + XBC_WIDTH
OFF_FNET = OFF_DT + 2 * SSD_HEADS
OFF_POOL = OFF_FNET + FNET_WIDTH
N_IN = OFF_POOL + POOL_WIDTH

LANE = 128
C_Z = 0
C_XBC = C_Z + SSD_WIDTH
C_FNET = C_XBC + XBC_WIDTH
C_POOL = C_FNET + FNET_WIDTH
C_DT = C_POOL + POOL_WIDTH
N_CAT = C_DT + 2 * LANE

DFT_N2 = 64
DFT_NB = 16
FFN_CB = 256
FFN_HALO = GRID_W
FFN_PAD = 16
FFN_ROW_CHUNKS = 2
FFN_UP_CHUNKS = 1
FFN_DOWN_BLOCKS = 6
VMEM_LIMIT = 56 * 1024 * 1024


def _ln(x):
    mu = jnp.mean(x, axis=-1, keepdims=True)
    xc = x - mu
    var = jnp.mean(xc * xc, axis=-1, keepdims=True)
    return xc * lax.rsqrt(var + EPS)


def _bdot(a, b):
    return jnp.dot(a.astype(BF16), b.astype(BF16), preferred_element_type=F32)


def _params(*sem):
    return pltpu.CompilerParams(dimension_semantics=sem, vmem_limit_bytes=VMEM_LIMIT)


def _ada_kernel(c_ref, w_ref, b_ref, o_ref):
    cv = c_ref[...]
    s = cv * jax.nn.sigmoid(cv)
    o_ref[0] = jnp.dot(s, w_ref[0], preferred_element_type=F32, precision=HIGHEST) + b_ref[0]


def _ada_call(cv, w_ada, b_ada):
    depth, d, n = w_ada.shape
    tn = 1536
    return pl.pallas_call(
        _ada_kernel,
        grid=(depth, n // tn),
        in_specs=[
            pl.BlockSpec((8, d), lambda l, j: (0, 0)),
            pl.BlockSpec((1, d, tn), lambda l, j: (l, 0, j)),
            pl.BlockSpec((1, 1, tn), lambda l, j: (l, 0, j)),
        ],
        out_specs=pl.BlockSpec((1, 8, tn), lambda l, j: (l, 0, j)),
        out_shape=jax.ShapeDtypeStruct((depth, 8, n), F32),
        compiler_params=_params("parallel", "parallel"),
        name="ada",
    )(cv, w_ada, b_ada.reshape(depth, 1, n))


def _fmix_kernel(cc_ref, cs_ref, fw_ref, o_ref):
    fw = fw_ref[0]
    o_ref[0, 0] = jnp.dot(cc_ref[...], fw, preferred_element_type=F32, precision=HIGHEST).astype(BF16)
    o_ref[0, 1] = jnp.dot(cs_ref[...], fw, preferred_element_type=F32, precision=HIGHEST).astype(BF16)


def _fmix_call(cc, cs, fw_bd):
    depth, w, _ = fw_bd.shape
    return pl.pallas_call(
        _fmix_kernel,
        grid=(depth,),
        in_specs=[
            pl.BlockSpec((w, w), lambda l: (0, 0)),
            pl.BlockSpec((w, w), lambda l: (0, 0)),
            pl.BlockSpec((1, w, w), lambda l: (l, 0, 0)),
        ],
        out_specs=pl.BlockSpec((1, 2, w, w), lambda l: (l, 0, 0, 0)),
        out_shape=jax.ShapeDtypeStruct((depth, 2, w, w), BF16),
        compiler_params=_params("parallel"),
        name="fmix",
    )(cc, cs, fw_bd)


def _inproj_kernel(x_ref, xp_ref, xn_ref, sh_ref, sc_ref, w_ref, cw_ref, cb_ref, dtb_ref, alog_ref,
                   fm_ref, pw_ref, ps_ref, inv_ref,
                   z_ref, xs_ref, bc_ref, dt_ref, da_ref, pq_ref, yp_ref, *, tm, pq_blocked):
    i = pl.program_id(1)
    nt = pl.num_programs(1)
    sh = sh_ref[0]
    sc = sc_ref[0]

    def modulated(xv):
        return (_ln(xv) * (1.0 + sc) + sh).astype(BF16)

    x_ext = jnp.concatenate([xp_ref[0], x_ref[0], xn_ref[0]], axis=0)
    u_ext = jnp.dot(modulated(x_ext), w_ref[...], preferred_element_type=F32)
    u = u_ext[POOL_HALO:POOL_HALO + tm]
    up = jnp.where(i > 0, u_ext[0:POOL_HALO, C_XBC:C_DT], 0.0)
    un = jnp.where(i < nt - 1, u_ext[POOL_HALO + tm:, C_XBC:C_DT], 0.0)

    z_ref[0] = u[:, C_Z:C_XBC].astype(BF16)

    xbc = u[:, C_XBC:C_FNET]
    row = lax.broadcasted_iota(jnp.int32, (tm, 1), 0)
    prev_row = up[POOL_HALO - 1:POOL_HALO, 0:XBC_WIDTH]
    next_row = un[0:1, 0:XBC_WIDTH]
    xm1 = jnp.where(row == 0, prev_row, pltpu.roll(xbc, 1, 0))
    xp1 = jnp.where(row == tm - 1, next_row, pltpu.roll(xbc, tm - 1, 0))
    cw = cw_ref[...]
    conv = xm1 * cw[0:1] + xbc * cw[1:2] + xp1 * cw[2:3] + cb_ref[...]
    act = conv * jax.nn.sigmoid(conv)
    xs_ref[0] = act[:, 0:SSD_WIDTH].astype(BF16)
    bc_ref[0] = act[:, SSD_WIDTH:XBC_WIDTH].astype(BF16)

    lane = lax.broadcasted_iota(jnp.int32, (1, LANE), 1)
    for d in range(2):
        raw = u[:, C_DT + d * LANE:C_DT + (d + 1) * LANE] + dtb_ref[d:d + 1, :]
        sp = jnp.maximum(raw, 0.0) + jnp.log(1.0 + jnp.exp(-jnp.abs(raw)))
        dt = jnp.where(lane < SSD_HEADS, sp, 0.0)
        dt_ref[0, d] = dt
        da_ref[0, d] = dt * (-jnp.exp(alog_ref[d:d + 1, :]))

    uf = u[:, C_FNET:C_POOL].astype(BF16)
    for part in range(2):
        pq = jnp.dot(uf, fm_ref[part], preferred_element_type=F32)
        if pq_blocked:
            pq = pq.reshape(tm // DFT_N2, DFT_N2, FNET_WIDTH).reshape(tm // DFT_N2, DFT_N2 * FNET_WIDTH)
        pq_ref[0, part] = pq

    off_pool = C_POOL - C_XBC
    ext = jnp.concatenate([up[:, off_pool:off_pool + POOL_WIDTH], u[:, C_POOL:C_DT],
                           un[:, off_pool:off_pool + POOL_WIDTH]], axis=0)
    n_ext = tm + 2 * POOL_HALO
    sums = {}
    cur = ext
    width = 1
    while width < POOL_WINDOWS[-1]:
        cur = cur + pltpu.roll(cur, n_ext - width, 0)
        width *= 2
        sums[width] = cur
    lane_p = lax.broadcasted_iota(jnp.int32, (1, POOL_WIDTH), 1)
    wsum = None
    for gi, win in enumerate(POOL_WINDOWS):
        start = POOL_HALO - win // 2
        s_w = sums[win]
        if start:
            s_w = pltpu.roll(s_w, n_ext - start, 0)
        s_w = s_w[0:tm]
        wsum = s_w if wsum is None else jnp.where(lane_p >= gi * POOL_GDIM, s_w, wsum)
    pooled = wsum * inv_ref[...] - u[:, C_POOL:C_DT]
    yp_ref[0] = (_bdot(pooled, pw_ref[...]) * ps_ref[...]).astype(BF16)


def _inproj_call(x, sh, sc, w_cat, conv_w, conv_b, dt_bias, a_log, fmix, pw, ps, inv, *, tm, pq_blocked):
    bsz, seq, d = x.shape
    nt = seq // tm
    nb8 = seq // 8
    tb = tm // 8
    mod_map = (lambda b, i: (b, 0, 0)) if sh.shape[0] > 1 else (lambda b, i: (0, 0, 0))

    def full(a):
        return pl.BlockSpec(a.shape, lambda b, i: (0,) * a.ndim)

    in_specs = [
        pl.BlockSpec((1, tm, d), lambda b, i: (b, i, 0)),
        pl.BlockSpec((1, 8, d), lambda b, i: (b, jnp.maximum(i * tb - 1, 0), 0)),
        pl.BlockSpec((1, 8, d), lambda b, i: (b, jnp.minimum((i + 1) * tb, nb8 - 1), 0)),
        pl.BlockSpec((1, 1, d), mod_map),
        pl.BlockSpec((1, 1, d), mod_map),
    ] + [full(a) for a in (w_cat, conv_w, conv_b, dt_bias, a_log, fmix, pw, ps)] + [
        pl.BlockSpec((tm, POOL_WIDTH), lambda b, i: (i, 0))]
    if pq_blocked:
        pq_shape = jax.ShapeDtypeStruct((bsz, 2, seq // DFT_N2, DFT_N2 * FNET_WIDTH), F32)
        pq_spec = pl.BlockSpec((1, 2, tm // DFT_N2, DFT_N2 * FNET_WIDTH), lambda b, i: (b, 0, i, 0))
    else:
        pq_shape = jax.ShapeDtypeStruct((bsz, 2, seq, FNET_WIDTH), F32)
        pq_spec = pl.BlockSpec((1, 2, tm, FNET_WIDTH), lambda b, i: (b, 0, i, 0))
    out_shape = [
        jax.ShapeDtypeStruct((bsz, seq, SSD_WIDTH), BF16),
        jax.ShapeDtypeStruct((bsz, seq, SSD_WIDTH), BF16),
        jax.ShapeDtypeStruct((bsz, seq, BC_WIDTH), BF16),
        jax.ShapeDtypeStruct((bsz, 2, seq, LANE), F32),
        jax.ShapeDtypeStruct((bsz, 2, seq, LANE), F32),
        pq_shape,
        jax.ShapeDtypeStruct((bsz, seq, POOL_WIDTH), BF16),
    ]
    out_specs = [
        pl.BlockSpec((1, tm, SSD_WIDTH), lambda b, i: (b, i, 0)),
        pl.BlockSpec((1, tm, SSD_WIDTH), lambda b, i: (b, i, 0)),
        pl.BlockSpec((1, tm, BC_WIDTH), lambda b, i: (b, i, 0)),
        pl.BlockSpec((1, 2, tm, LANE), lambda b, i: (b, 0, i, 0)),
        pl.BlockSpec((1, 2, tm, LANE), lambda b, i: (b, 0, i, 0)),
        pq_spec,
        pl.BlockSpec((1, tm, POOL_WIDTH), lambda b, i: (b, i, 0)),
    ]
    return pl.pallas_call(
        functools.partial(_inproj_kernel, tm=tm, pq_blocked=pq_blocked),
        grid=(bsz, nt),
        in_specs=in_specs,
        out_specs=out_specs,
        out_shape=out_shape,
        compiler_params=_params("parallel", "parallel"),
        name="inproj",
    )(x, x, x, sh, sc, w_cat, conv_w, conv_b, dt_bias, a_log, fmix, pw, ps, inv)


def _bf16_pieces(x, terms):
    pieces = []
    rem = x
    for t in range(terms):
        piece = rem.astype(BF16)
        pieces.append(piece)
        if t + 1 < terms:
            rem = rem - piece.astype(F32)
    return pieces


def _ssd_kernel(xsf_ref, bcf_ref, dtf_ref, daf_ref, xsb_ref, bcb_ref, dtb_ref, dab_ref, tri_ref, ex_ref, init_ref,
                yf_ref, yb_ref, fin_ref, st_ref, *, kc):
    s = pl.program_id(1)
    ns = pl.num_programs(1)

    @pl.when(s == 0)
    def _():
        st_ref[...] = init_ref[0]

    fwd = (xsf_ref, bcf_ref, dtf_ref, daf_ref, yf_ref)
    bwd = (xsb_ref, bcb_ref, dtb_ref, dab_ref, yb_ref)
    jobs = [(0, k, fwd) for k in range(kc)] + [(1, kc - 1 - k, bwd) for k in range(kc)]
    c = SSD_CHUNK
    gn = SSD_GROUPS * SSD_STATE
    pw = 2 * SSD_HEADDIM
    gw = HEADS_PER_GROUP * SSD_HEADDIM
    low_half = lax.broadcasted_iota(jnp.int32, (1, pw), 1) < SSD_HEADDIM
    zero = jnp.zeros((SSD_CHUNK, pw), BF16)
    tri_b = [tri_ref[d].astype(BF16) for d in range(2)]
    mask = [tri_ref[d] > 0.5 for d in range(2)]

    def rows(k):
        return slice(k * c, (k + 1) * c)

    cs3 = [jnp.dot(tri_b[d], jnp.concatenate(_bf16_pieces(r[3][0, 0, rows(k), :], 3), axis=1),
                   preferred_element_type=F32) for d, k, r in jobs]
    cs = [v[:, 0:LANE] + v[:, LANE:2 * LANE] + v[:, 2 * LANE:3 * LANE] for v in cs3]
    cs_t = [v.T for v in cs]
    ex4 = [jnp.dot(jnp.concatenate(_bf16_pieces(cs[n], 2) + _bf16_pieces(r[2][0, 0, rows(k), :], 2), axis=0),
                   ex_ref[...], preferred_element_type=F32) for n, (d, k, r) in enumerate(jobs)]
    cs_x = [v[0:c] + v[c:2 * c] for v in ex4]
    dt_x = [v[2 * c:3 * c] + v[3 * c:4 * c] for v in ex4]
    tot_x = [cs_x[n][(c - 1 if d == 0 else 0):(c if d == 0 else 1), :] for n, (d, k, r) in enumerate(jobs)]
    e_cs = [jnp.exp(v) for v in cs_x]
    e_tot = [jnp.exp(v) for v in tot_x]
    xd32 = [r[0][0, rows(k), :].astype(F32) * dt_x[n] for n, (d, k, r) in enumerate(jobs)]
    xd_dec = [(xd32[n] * jnp.exp(tot_x[n] - cs_x[n])).astype(BF16) for n in range(len(jobs))]
    xd = [v.astype(BF16) for v in xd32]
    bc = [r[1][0, rows(k), :] for d, k, r in jobs]
    bg_t = [[v[:, g * SSD_STATE:(g + 1) * SSD_STATE].astype(F32).T.astype(BF16) for g in range(SSD_GROUPS)]
            for v in bc]
    cg = [[v[:, gn + g * SSD_STATE:gn + (g + 1) * SSD_STATE] for g in range(SSD_GROUPS)] for v in bc]
    cb = [[jnp.dot(cg[n][g], bg_t[n][g], preferred_element_type=F32) for g in range(SSD_GROUPS)]
          for n in range(len(jobs))]

    def y_diagonal(n, d, g):
        parts = []
        for pp in range(HEADS_PER_GROUP // 2):
            p = g * (HEADS_PER_GROUP // 2) + pp
            decay = []
            for h in (2 * p, 2 * p + 1):
                seg = cs[n][:, h:h + 1] - cs_t[n][h:h + 1, :]
                decay.append(cb[n][g] * jnp.where(mask[d], jnp.exp(jnp.minimum(seg, 0.0)), 0.0))
            m = jnp.concatenate(decay, axis=1).astype(BF16)
            xp = xd[n][:, p * pw:(p + 1) * pw]
            x_bd = jnp.concatenate([jnp.where(low_half, xp, zero), jnp.where(low_half, zero, xp)], axis=0)
            parts.append(jnp.dot(m, x_bd, preferred_element_type=F32))
        return jnp.concatenate(parts, axis=1)

    y_diag = [[y_diagonal(n, d, g) for g in range(SSD_GROUPS)] for n, (d, k, r) in enumerate(jobs)]

    for n, (d, k, r) in enumerate(jobs):
        for g in range(SSD_GROUPS):
            glanes = slice(g * gw, (g + 1) * gw)
            st = st_ref[d, g]
            y_off = jnp.dot(cg[n][g], st.astype(BF16), preferred_element_type=F32) * e_cs[n][:, glanes]
            r[4][0, rows(k), glanes] = (y_diag[n][g] + y_off).astype(BF16)
            st_ref[d, g] = st * e_tot[n][:, glanes] + jnp.dot(bg_t[n][g], xd_dec[n][:, glanes],
                                                              preferred_element_type=F32)

    @pl.when(s == ns - 1)
    def _():
        fin_ref[0] = st_ref[...]


def _ssd_call(xs, bc, dt, da, tri, ex, init):
    bsz, seq, _ = xs.shape
    kc = min(SSD_CHUNKS_PER_STEP, seq // SSD_CHUNK)
    rows = kc * SSD_CHUNK
    ns = seq // rows
    st_block = (1,) + SSD_STATE_SHAPE

    def tok(w, rev):
        return pl.BlockSpec((1, rows, w), lambda b, s: (b, ns - 1 - s if rev else s, 0))

    def per_dir(w, d):
        return pl.BlockSpec((1, 1, rows, w), lambda b, s: (b, d, ns - 1 - s if d else s, 0))

    return pl.pallas_call(
        functools.partial(_ssd_kernel, kc=kc),
        grid=(bsz, ns),
        in_specs=[
            tok(SSD_WIDTH, False), tok(BC_WIDTH, False), per_dir(LANE, 0), per_dir(LANE, 0),
            tok(SSD_WIDTH, True), tok(BC_WIDTH, True), per_dir(LANE, 1), per_dir(LANE, 1),
            pl.BlockSpec(tri.shape, lambda b, s: (0, 0, 0)),
            pl.BlockSpec(ex.shape, lambda b, s: (0, 0)),
            pl.BlockSpec(st_block, lambda b, s: (b, 0, 0, 0, 0)),
        ],
        out_specs=[tok(SSD_WIDTH, False), tok(SSD_WIDTH, True), pl.BlockSpec(st_block, lambda b, s: (b, 0, 0, 0, 0))],
        out_shape=[
            jax.ShapeDtypeStruct((bsz, seq, SSD_WIDTH), BF16),
            jax.ShapeDtypeStruct((bsz, seq, SSD_WIDTH), BF16),
            jax.ShapeDtypeStruct((bsz,) + SSD_STATE_SHAPE, F32),
        ],
        scratch_shapes=[pltpu.VMEM(SSD_STATE_SHAPE, F32)],
        compiler_params=_params("parallel", "arbitrary"),
        name="ssd",
    )(xs, bc, dt, da, xs, bc, dt, da, tri, ex, init)


def _dft1_kernel(p_ref, q_ref, m_ref, tr_ref, ti_ref, *, n1):
    re, im = [], []
    for j in range(DFT_NB):
        m = m_ref[j]
        rp = _bdot(m, p_ref[0, 0, :, j * FNET_WIDTH:(j + 1) * FNET_WIDTH])
        rq = _bdot(m, q_ref[0, 0, :, j * FNET_WIDTH:(j + 1) * FNET_WIDTH])
        re.append(rp[:n1] + rq[n1:])
        im.append(rp[n1:] - rq[:n1])
    tr_ref[0] = jnp.stack(re).reshape(DFT_NB, n1 * FNET_WIDTH).astype(BF16)
    ti_ref[0] = jnp.stack(im).reshape(DFT_NB, n1 * FNET_WIDTH).astype(BF16)


def _dft1_call(pqv, mtab):
    bsz, _, n1, cols_all = pqv.shape
    w = cols_all // DFT_N2
    cols = DFT_NB * w
    t_shape = jax.ShapeDtypeStruct((bsz, DFT_N2, n1 * w), BF16)
    t_spec = pl.BlockSpec((1, DFT_NB, n1 * w), lambda b, j: (b, j, 0))
    return pl.pallas_call(
        functools.partial(_dft1_kernel, n1=n1),
        grid=(bsz, DFT_N2 // DFT_NB),
        in_specs=[
            pl.BlockSpec((1, 1, n1, cols), lambda b, j: (b, 0, 0, j)),
            pl.BlockSpec((1, 1, n1, cols), lambda b, j: (b, 1, 0, j)),
            pl.BlockSpec((DFT_NB, 2 * n1, n1), lambda b, j: (j, 0, 0)),
        ],
        out_specs=[t_spec, t_spec],
        out_shape=[t_shape, t_shape],
        compiler_params=_params("parallel", "parallel"),
        name="dft1",
    )(pqv, pqv, mtab)


def _dft2_kernel(d_ref, *refs, split):
    *t_refs, o_ref = refs
    t = jnp.concatenate([r[0] for r in t_refs], axis=0) if len(t_refs) > 1 else t_refs[0][0]
    res = _bdot(d_ref[...], t)
    if split:
        res = res.reshape(res.shape[0], res.shape[1] // split, split)
    o_ref[0] = res.astype(BF16)


def _dft2_call(dmat, parts, *, tn, split=0):
    bsz, _, n = parts[0].shape
    mo = dmat.shape[0]
    if split:
        out_spec = pl.BlockSpec((1, mo, tn // split, split), lambda b, j: (b, 0, j, 0))
        out_shape = jax.ShapeDtypeStruct((bsz, mo, n // split, split), BF16)
    else:
        out_spec = pl.BlockSpec((1, mo, tn), lambda b, j: (b, 0, j))
        out_shape = jax.ShapeDtypeStruct((bsz, mo, n), BF16)
    return pl.pallas_call(
        functools.partial(_dft2_kernel, split=split),
        grid=(bsz, n // tn),
        in_specs=[pl.BlockSpec((mo, dmat.shape[1]), lambda b, j: (0, 0))] + [
            pl.BlockSpec((1, p.shape[1], tn), lambda b, j: (b, 0, j)) for p in parts],
        out_specs=out_spec,
        out_shape=out_shape,
        compiler_params=_params("parallel", "parallel"),
        name="dft2",
    )(dmat, *parts)


def _mix_kernel(yf_ref, yb_ref, xs_ref, z_ref, yfn_ref, ypl_ref, x_ref, dsk_ref, nw_ref, wo_ref, g1_ref,
                lg_ref, lb_ref, sh2_ref, sc2_ref, x1_ref, h2_ref):
    y = yf_ref[0].astype(F32) + yb_ref[0].astype(F32) + xs_ref[0].astype(F32) * dsk_ref[...]
    z = z_ref[0].astype(F32)
    y = y * (z * jax.nn.sigmoid(z))
    gw = SSD_WIDTH // SSD_GROUPS
    parts = []
    for g in range(SSD_GROUPS):
        yg = y[:, g * gw:(g + 1) * gw]
        ms = jnp.mean(yg * yg, axis=-1, keepdims=True)
        parts.append(yg * lax.rsqrt(ms + EPS))
    yn = jnp.concatenate(parts, axis=-1) * nw_ref[...]
    o_f = SSD_WIDTH
    o_p = SSD_WIDTH + FNET_WIDTH
    mix = (_bdot(yn, wo_ref[0:o_f, :]) + _bdot(yfn_ref[0], wo_ref[o_f:o_p, :])
           + _bdot(ypl_ref[0], wo_ref[o_p:, :]))
    x1 = _ln(ALPHA * x_ref[0] + g1_ref[0] * mix) * lg_ref[...] + lb_ref[...]
    x1_ref[0] = x1
    h2_ref[0] = (_ln(x1) * (1.0 + sc2_ref[0]) + sh2_ref[0]).astype(BF16)


def _mix_call(yf, yb, xs, z, yfn, ypl, x, dsk, nw, wo, g1, lg, lb, sh2, sc2, *, tm):
    bsz, seq, d = x.shape
    mod_map = (lambda b, i: (b, 0, 0)) if g1.shape[0] > 1 else (lambda b, i: (0, 0, 0))

    def full(a):
        return pl.BlockSpec(a.shape, lambda b, i: (0,) * a.ndim)

    def tok(w):
        return pl.BlockSpec((1, tm, w), lambda b, i: (b, i, 0))

    mod = pl.BlockSpec((1, 1, d), mod_map)
    return pl.pallas_call(
        _mix_kernel,
        grid=(bsz, seq // tm),
        in_specs=[
            tok(SSD_WIDTH), tok(SSD_WIDTH), tok(SSD_WIDTH), tok(SSD_WIDTH), tok(FNET_WIDTH), tok(POOL_WIDTH), tok(d),
            full(dsk), full(nw), full(wo), mod, full(lg), full(lb), mod, mod,
        ],
        out_specs=[tok(d), tok(d)],
        out_shape=[jax.ShapeDtypeStruct((bsz, seq, d), F32), jax.ShapeDtypeStruct((bsz, seq, d), BF16)],
        compiler_params=_params("parallel", "parallel"),
        name="mix",
    )(yf, yb, xs, z, yfn, ypl, x, dsk, nw, wo, g1, lg, lb, sh2, sc2)


def _ffn_kernel(h_ref, hp_ref, hn_ref, wu_ref, cw_ref, cb_ref, wd_ref, x1_ref, g2_ref, lg_ref, lb_ref, o_ref,
                hbuf, a0, a1, hid, acc, *, tm, on_grid):
    i = pl.program_id(1)
    nt = pl.num_programs(1)
    nj = D_FF // FFN_CB
    ext = tm + 2 * FFN_HALO
    ra = ext // FFN_UP_CHUNKS
    rb = tm // FFN_ROW_CHUNKS
    slots = (a0, a1)
    base = FFN_PAD + FFN_HALO
    rows = (-1, 0, 1) if on_grid else (0,)

    def block_cols(j):
        return slice(j * FFN_CB, (j + 1) * FFN_CB), slice(D_FF + j * FFN_CB, D_FF + (j + 1) * FFN_CB)

    def paired(ref, j):
        return jnp.concatenate([ref[:, cols] for cols in block_cols(j)], axis=1)

    def up_project(a_ref, j):
        for r in range(FFN_UP_CHUNKS):
            hb = hbuf[r * ra:(r + 1) * ra, :]
            for half, cols in enumerate(block_cols(j)):
                a = jnp.dot(hb, wu_ref[:, cols], preferred_element_type=F32)
                a_ref[FFN_PAD + r * ra:FFN_PAD + (r + 1) * ra, half * FFN_CB:(half + 1) * FFN_CB] = a.astype(BF16)

    def conv(a_ref, j, r):
        cw = paired(cw_ref, j).astype(BF16)
        m = FFN_PAD
        n = rb + 2 * m
        start = base + r * rb - m
        col = (lax.broadcasted_iota(jnp.int32, (n, 1), 0) + (r * rb - m)) & (GRID_W - 1)
        taps = {dr: a_ref[pl.ds(start + dr * GRID_W, n), :] for dr in rows}

        def column(dc, lo, hi):
            acc_dc = None
            for dr in rows:
                k = (dr + 1) * 3 + dc + 1
                term = taps[dr][lo:hi] * cw[k:k + 1]
                acc_dc = term if acc_dc is None else acc_dc + term
            return acc_dc

        left = column(-1, 0, n)
        right = column(1, 0, n)
        if on_grid:
            zero = jnp.zeros_like(left)
            left = jnp.where(col == GRID_W - 1, zero, left)
            right = jnp.where(col == 0, zero, right)
        left2 = jnp.concatenate([left[n - 2:], left[:n - 2]], axis=0)
        u = (left2 + right)[m + 1:m + 1 + rb]
        return column(0, m, m + rb) + u + paired(cb_ref, j).astype(BF16)

    hbuf[0:FFN_HALO] = jnp.where(i > 0, hp_ref[0], jnp.zeros_like(hp_ref[0]))
    hbuf[FFN_HALO:FFN_HALO + tm] = h_ref[0]
    hbuf[FFN_HALO + tm:] = jnp.where(i < nt - 1, hn_ref[0], jnp.zeros_like(hn_ref[0]))
    for a in slots:
        a[0:FFN_PAD] = jnp.zeros((FFN_PAD, 2 * FFN_CB), BF16)
        a[FFN_PAD + ext:] = jnp.zeros((FFN_PAD, 2 * FFN_CB), BF16)

    up_project(slots[0], 0)
    for j in range(nj):
        if j + 1 < nj:
            up_project(slots[(j + 1) % 2], j + 1)
        pos = j % FFN_DOWN_BLOCKS
        for r in range(FFN_ROW_CHUNKS):
            c = conv(slots[j % 2], j, r)
            hidden = jax.nn.gelu(c[:, FFN_CB:], approximate=True) * c[:, :FFN_CB]
            hid[r * rb:(r + 1) * rb, pos * FFN_CB:(pos + 1) * FFN_CB] = hidden
        if pos == FFN_DOWN_BLOCKS - 1 or j == nj - 1:
            first = j - pos
            k = (pos + 1) * FFN_CB
            part = jnp.dot(hid[:, 0:k], wd_ref[first * FFN_CB:first * FFN_CB + k, :], preferred_element_type=F32)
            if first == 0:
                acc[...] = part
            else:
                acc[...] += part

    o_ref[0] = _ln(ALPHA * x1_ref[0] + g2_ref[0] * acc[...]) * lg_ref[...] + lb_ref[...]


def _ffn_call(h, w_up, conv_w9, conv_b, w_down, x1, g2, lg, lb, *, tm, on_grid):
    bsz, seq, d = h.shape
    nt = seq // tm
    th = tm // FFN_HALO
    nh = seq // FFN_HALO
    ext = tm + 2 * FFN_HALO
    mod_map = (lambda b, i: (b, 0, 0)) if g2.shape[0] > 1 else (lambda b, i: (0, 0, 0))

    def resident(a):
        return pl.BlockSpec(a.shape, lambda b, i: (0,) * a.ndim, pipeline_mode=pl.Buffered(1))

    a_scratch = pltpu.VMEM((ext + 2 * FFN_PAD, 2 * FFN_CB), BF16)
    return pl.pallas_call(
        functools.partial(_ffn_kernel, tm=tm, on_grid=on_grid),
        grid=(bsz, nt),
        in_specs=[
            pl.BlockSpec((1, tm, d), lambda b, i: (b, i, 0)),
            pl.BlockSpec((1, FFN_HALO, d), lambda b, i: (b, jnp.maximum(i * th - 1, 0), 0)),
            pl.BlockSpec((1, FFN_HALO, d), lambda b, i: (b, jnp.minimum((i + 1) * th, nh - 1), 0)),
            resident(w_up), resident(conv_w9), resident(conv_b), resident(w_down),
            pl.BlockSpec((1, tm, d), lambda b, i: (b, i, 0)),
            pl.BlockSpec((1, 1, d), mod_map),
            resident(lg), resident(lb),
        ],
        out_specs=pl.BlockSpec((1, tm, d), lambda b, i: (b, i, 0)),
        out_shape=jax.ShapeDtypeStruct((bsz, seq, d), F32),
        scratch_shapes=[pltpu.VMEM((ext, d), BF16), a_scratch, a_scratch,
                        pltpu.VMEM((tm, FFN_DOWN_BLOCKS * FFN_CB), BF16),
                        pltpu.VMEM((tm, d), F32)],
        compiler_params=_params("parallel", "arbitrary"),
        name="ffn",
    )(h, h, h, w_up, conv_w9, conv_b, w_down, x1, g2, lg, lb)


def _block_diag(blocks):
    g, n, m = blocks.shape
    out = jnp.zeros((g * n, g * m), blocks.dtype)
    for k in range(g):
        out = out.at[k * n:(k + 1) * n, k * m:(k + 1) * m].set(blocks[k])
    return out


def _dft_tables(seq):
    n1 = seq // DFT_N2
    k1 = np.arange(n1, dtype=np.int64)[:, None]
    nn1 = np.arange(n1, dtype=np.int64)[None, :]
    tabs = []
    for n2 in range(DFT_N2):
        th = 2.0 * np.pi * ((k1 * (DFT_N2 * nn1 + n2)) % seq) / seq
        tabs.append(np.concatenate([np.cos(th), -np.sin(th)], axis=0))
    mtab = np.stack(tabs)
    k2 = np.arange(DFT_N2, dtype=np.int64)[:, None]
    nn2 = np.arange(DFT_N2, dtype=np.int64)[None, :]
    th2 = 2.0 * np.pi * ((k2 * nn2) % DFT_N2) / DFT_N2
    scale = 1.0 / math.sqrt(seq * FNET_GDIM)
    dmat = scale * np.concatenate([np.cos(th2), np.sin(th2)], axis=1)
    return jnp.asarray(mtab, F32), jnp.asarray(dmat, F32)


def _dense_dft_table(seq):
    k = np.arange(seq, dtype=np.int64)
    th = 2.0 * np.pi * ((k[:, None] * k[None, :]) % seq) / seq
    scale = 1.0 / math.sqrt(seq * FNET_GDIM)
    return jnp.asarray(scale * np.concatenate([np.cos(th), -np.sin(th)], axis=1), F32)


def _channel_dft_tables():
    k = np.arange(FNET_GDIM, dtype=np.int64)
    th = 2.0 * np.pi * ((k[:, None] * k[None, :]) % FNET_GDIM) / FNET_GDIM
    eye = np.eye(FNET_GROUPS)
    return jnp.asarray(np.kron(eye, np.cos(th)), F32), jnp.asarray(np.kron(eye, np.sin(th)), F32)


def _pool_inverse_counts(seq):
    pos = np.arange(seq, dtype=np.int64)
    cols = []
    for win in POOL_WINDOWS:
        left = win // 2
        cnt = np.minimum(pos + (win - left), seq) - np.maximum(pos - left, 0)
        cols.append(np.repeat((1.0 / cnt)[:, None], POOL_GDIM, axis=1))
    return jnp.asarray(np.concatenate(cols, axis=1), F32)


def _scan_matrices():
    r = np.arange(SSD_CHUNK)
    lower = (r[:, None] >= r[None, :]).astype(np.float32)
    return jnp.asarray(np.stack([lower, lower.T]), F32)


def _head_expansion():
    ex = np.zeros((LANE, SSD_WIDTH), np.float32)
    for h in range(SSD_HEADS):
        ex[h, h * SSD_HEADDIM:(h + 1) * SSD_HEADDIM] = 1.0
    return jnp.asarray(ex, BF16)


def _token_mixer(x, sh1, sc1, lw, init, tri, ex, *, tm):
    bsz, seq, _ = x.shape
    z, xs, bc, dt, da, pq, ypl = _inproj_call(
        x, sh1, sc1, lw["w_cat"], lw["ssd_conv_w"], lw["ssd_conv_b"], lw["dt_bias"], lw["a_log"], lw["fmix"],
        lw["pool_bd"], lw["pool_scale"], _pool_inverse_counts(seq), tm=tm, pq_blocked=_two_stage_dft(seq))
    yf, yb, fin = _ssd_call(xs, bc, dt, da, tri, ex, init)
    return z, xs, yf, yb, pq, ypl, fin


def _two_stage_dft(seq):
    return seq % (DFT_N2 * 8) == 0


def _fourier_positions(pq, seq):
    bsz, w = pq.shape[0], FNET_WIDTH
    if _two_stage_dft(seq):
        n1 = seq // DFT_N2
        mtab, dmat = _dft_tables(seq)
        t_re, t_im = _dft1_call(pq, mtab)
        out = _dft2_call(dmat, [t_re, t_im], tn=min(4096, n1 * w), split=w)
        return out.reshape(bsz, seq, w)
    return _dft2_call(_dense_dft_table(seq), [pq.reshape(bsz, 2 * seq, w)], tn=w)


def kernel(x, c, ctx, c_ctx, w_ada, b_ada, w_in, ssd_conv_w, ssd_conv_b, ssd_dt_bias, ssd_a_log, ssd_d,
           ssd_norm_w, fnet_w, pool_w, pool_scale, w_out, ln1_g, ln1_b, ffn_w_up, ffn_conv_w, ffn_conv_b,
           ffn_w_down, ln2_g, ln2_b):
    bsz, seq, d = x.shape
    n_ctx = ctx.shape[1]
    assert d == D_MODEL and bsz <= 7 and seq % (GRID_W * 8) == 0 and n_ctx % SSD_CHUNK == 0
    tm_lat = min(1024, seq)
    tm_ffn = min(1024, seq)

    cv = jnp.zeros((8, d), F32).at[0:bsz].set(c).at[bsz].set(c_ctx)
    mods = _ada_call(cv, w_ada, b_ada)
    tri = _scan_matrices()
    ex = _head_expansion()
    fmix = _fmix_call(*_channel_dft_tables(), jnp.stack([_block_diag(fnet_w[l]) for l in range(DEPTH)]))
    zero_state = jnp.zeros((bsz,) + SSD_STATE_SHAPE, F32)
    lane_pad = jnp.zeros((d, LANE - SSD_HEADS), F32)

    for l in range(DEPTH):
        last = l == DEPTH - 1
        m = mods[l].reshape(8, 6, d)
        lat = [m[0:bsz, k][:, None, :] for k in range(6)]
        cx = [m[bsz:bsz + 1, k][:, None, :] for k in range(6)]
        w = w_in[l]
        w_cat = jnp.concatenate(
            [w[:, 0:OFF_DT], w[:, OFF_FNET:N_IN], w[:, OFF_DT:OFF_DT + SSD_HEADS], lane_pad,
             w[:, OFF_DT + SSD_HEADS:OFF_FNET], lane_pad], axis=1).astype(BF16)
        pad16 = ((0, 0), (0, LANE - SSD_HEADS))
        lw = {
            "w_cat": w_cat,
            "ssd_conv_w": ssd_conv_w[l],
            "ssd_conv_b": ssd_conv_b[l][None, :],
            "dt_bias": jnp.pad(ssd_dt_bias[l], pad16),
            "a_log": jnp.pad(ssd_a_log[l], pad16),
            "fmix": fmix[l],
            "pool_bd": _block_diag(pool_w[l]).astype(BF16),
            "pool_scale": pool_scale[l][None, :],
        }
        dsk = jnp.repeat(ssd_d[l], SSD_HEADDIM)[None, :]
        nw = ssd_norm_w[l][None, :]
        wo = w_out[l].astype(BF16)
        lg1, lb1 = ln1_g[l][None, :], ln1_b[l][None, :]
        lg2, lb2 = ln2_g[l][None, :], ln2_b[l][None, :]
        w_up = ffn_w_up[l].astype(BF16)
        w_down = ffn_w_down[l].astype(BF16)
        conv9 = ffn_conv_w[l].reshape(9, 2 * D_FF)
        conv_b = ffn_conv_b[l][None, :]

        cz, cxs, cyf, cyb, cpq, cypl, c_fin = _token_mixer(ctx, cx[0], cx[1], lw, zero_state, tri, ex, tm=n_ctx)
        z, xs, yf, yb, pq, ypl, _ = _token_mixer(x, lat[0], lat[1], lw, c_fin, tri, ex, tm=tm_lat)

        yfn = _fourier_positions(pq, seq)
        x1, h2 = _mix_call(yf, yb, xs, z, yfn, ypl, x, dsk, nw, wo, lat[2], lg1, lb1, lat[3], lat[4], tm=tm_lat)
        x = _ffn_call(h2, w_up, conv9, conv_b, w_down, x1, lat[5], lg2, lb2, tm=tm_ffn, on_grid=True)

        if not last:
            cyfn = _fourier_positions(cpq, n_ctx)
            c1, ch2 = _mix_call(cyf, cyb, cxs, cz, cyfn, cypl, ctx, dsk, nw, wo, cx[2], lg1, lb1, cx[3], cx[4],
                                tm=n_ctx)
            ctx = _ffn_call(ch2, w_up, conv9, conv_b, w_down, c1, cx[5], lg2, lb2, tm=n_ctx, on_grid=False)
    return x
```

```python
import functools
import math

import numpy as np
import jax
import jax.numpy as jnp
from jax import lax
from jax.experimental import pallas as pl
from jax.experimental.pallas import tpu as pltpu

F32 = jnp.float32
BF16 = jnp.bfloat16
HIGHEST = lax.Precision.HIGHEST

D_MODEL = 1024
DEPTH = 2
GRID_W = 64
SSD_WIDTH = 512
SSD_HEADDIM = 64
SSD_HEADS = 8
SSD_GROUPS = 2
HEADS_PER_GROUP = SSD_HEADS // SSD_GROUPS
SSD_STATE = 128
SSD_CHUNK = 128
SSD_CHUNKS_PER_STEP = 8
XBC_WIDTH = SSD_WIDTH + 2 * SSD_GROUPS * SSD_STATE
BC_WIDTH = 2 * SSD_GROUPS * SSD_STATE
SSD_STATE_SHAPE = (2, SSD_GROUPS, SSD_STATE, HEADS_PER_GROUP * SSD_HEADDIM)
FNET_WIDTH = 256
FNET_GROUPS = 4
FNET_GDIM = 64
POOL_WINDOWS = (2, 4, 8, 16)
POOL_WIDTH = 256
POOL_GDIM = 64
POOL_HALO = 8
D_FF = 2816
ALPHA = (2.0 * DEPTH) ** 0.25
EPS = 1e-6

OFF_XBC = SSD_WIDTH
OFF_DT = OFF_XBC + XBC_WIDTH
OFF_FNET = OFF_DT + 2 * SSD_HEADS
OFF_POOL = OFF_FNET + FNET_WIDTH
N_IN = OFF_POOL + POOL_WIDTH

LANE = 128
C_Z = 0
C_XBC = C_Z + SSD_WIDTH
C_FNET = C_XBC + XBC_WIDTH
C_POOL = C_FNET + FNET_WIDTH
C_DT = C_POOL + POOL_WIDTH
N_CAT = C_DT + 2 * LANE

DFT_N2 = 64
DFT_NB = 16
FFN_CB = 256
FFN_HALO = GRID_W
FFN_PAD = 16
FFN_ROW_CHUNKS = 2
FFN_UP_CHUNKS = 1
FFN_DOWN_BLOCKS = 4
VMEM_LIMIT = 56 * 1024 * 1024


def _ln(x):
    mu = jnp.mean(x, axis=-1, keepdims=True)
    xc = x - mu
    var = jnp.mean(xc * xc, axis=-1, keepdims=True)
    return xc * lax.rsqrt(var + EPS)


def _bdot(a, b):
    return jnp.dot(a.astype(BF16), b.astype(BF16), preferred_element_type=F32)


def _params(*sem):
    return pltpu.CompilerParams(dimension_semantics=sem, vmem_limit_bytes=VMEM_LIMIT)


def _ada_kernel(c_ref, w_ref, b_ref, o_ref):
    cv = c_ref[...]
    s = cv * jax.nn.sigmoid(cv)
    o_ref[0] = jnp.dot(s, w_ref[0], preferred_element_type=F32, precision=HIGHEST) + b_ref[0]


def _ada_call(cv, w_ada, b_ada):
    depth, d, n = w_ada.shape
    tn = 1536
    return pl.pallas_call(
        _ada_kernel,
        grid=(depth, n // tn),
        in_specs=[
            pl.BlockSpec((8, d), lambda l, j: (0, 0)),
            pl.BlockSpec((1, d, tn), lambda l, j: (l, 0, j)),
            pl.BlockSpec((1, 1, tn), lambda l, j: (l, 0, j)),
        ],
        out_specs=pl.BlockSpec((1, 8, tn), lambda l, j: (l, 0, j)),
        out_shape=jax.ShapeDtypeStruct((depth, 8, n), F32),
        compiler_params=_params("parallel", "parallel"),
        name="ada",
    )(cv, w_ada, b_ada.reshape(depth, 1, n))


def _fmix_kernel(cc_ref, cs_ref, fw_ref, o_ref):
    fw = fw_ref[0]
    o_ref[0, 0] = jnp.dot(cc_ref[...], fw, preferred_element_type=F32, precision=HIGHEST).astype(BF16)
    o_ref[0, 1] = jnp.dot(cs_ref[...], fw, preferred_element_type=F32, precision=HIGHEST).astype(BF16)


def _fmix_call(cc, cs, fw_bd):
    depth, w, _ = fw_bd.shape
    return pl.pallas_call(
        _fmix_kernel,
        grid=(depth,),
        in_specs=[
            pl.BlockSpec((w, w), lambda l: (0, 0)),
            pl.BlockSpec((w, w), lambda l: (0, 0)),
            pl.BlockSpec((1, w, w), lambda l: (l, 0, 0)),
        ],
        out_specs=pl.BlockSpec((1, 2, w, w), lambda l: (l, 0, 0, 0)),
        out_shape=jax.ShapeDtypeStruct((depth, 2, w, w), BF16),
        compiler_params=_params("parallel"),
        name="fmix",
    )(cc, cs, fw_bd)


def _inproj_kernel(x_ref, xp_ref, xn_ref, sh_ref, sc_ref, w_ref, cw_ref, cb_ref, dtb_ref, alog_ref,
                   fm_ref, pw_ref, ps_ref, inv_ref,
                   z_ref, xs_ref, bc_ref, dt_ref, da_ref, pq_ref, yp_ref, *, tm, pq_blocked):
    i = pl.program_id(1)
    nt = pl.num_programs(1)
    sh = sh_ref[0]
    sc = sc_ref[0]

    def modulated(xv):
        return (_ln(xv) * (1.0 + sc) + sh).astype(BF16)

    x_ext = jnp.concatenate([xp_ref[0], x_ref[0], xn_ref[0]], axis=0)
    u_ext = jnp.dot(modulated(x_ext), w_ref[...], preferred_element_type=F32)
    u = u_ext[POOL_HALO:POOL_HALO + tm]
    up = jnp.where(i > 0, u_ext[0:POOL_HALO, C_XBC:C_DT], 0.0)
    un = jnp.where(i < nt - 1, u_ext[POOL_HALO + tm:, C_XBC:C_DT], 0.0)

    z_ref[0] = u[:, C_Z:C_XBC].astype(BF16)

    xbc = u[:, C_XBC:C_FNET]
    row = lax.broadcasted_iota(jnp.int32, (tm, 1), 0)
    prev_row = up[POOL_HALO - 1:POOL_HALO, 0:XBC_WIDTH]
    next_row = un[0:1, 0:XBC_WIDTH]
    xm1 = jnp.where(row == 0, prev_row, pltpu.roll(xbc, 1, 0))
    xp1 = jnp.where(row == tm - 1, next_row, pltpu.roll(xbc, tm - 1, 0))
    cw = cw_ref[...]
    conv = xm1 * cw[0:1] + xbc * cw[1:2] + xp1 * cw[2:3] + cb_ref[...]
    act = conv * jax.nn.sigmoid(conv)
    xs_ref[0] = act[:, 0:SSD_WIDTH].astype(BF16)
    bc_ref[0] = act[:, SSD_WIDTH:XBC_WIDTH].astype(BF16)

    lane = lax.broadcasted_iota(jnp.int32, (1, LANE), 1)
    for d in range(2):
        raw = u[:, C_DT + d * LANE:C_DT + (d + 1) * LANE] + dtb_ref[d:d + 1, :]
        sp = jnp.maximum(raw, 0.0) + jnp.log(1.0 + jnp.exp(-jnp.abs(raw)))
        dt = jnp.where(lane < SSD_HEADS, sp, 0.0)
        dt_ref[0, d] = dt
        da_ref[0, d] = dt * (-jnp.exp(alog_ref[d:d + 1, :]))

    uf = u[:, C_FNET:C_POOL].astype(BF16)
    for part in range(2):
        pq = jnp.dot(uf, fm_ref[part], preferred_element_type=F32)
        if pq_blocked:
            pq = pq.reshape(tm // DFT_N2, DFT_N2, FNET_WIDTH).reshape(tm // DFT_N2, DFT_N2 * FNET_WIDTH)
        pq_ref[0, part] = pq

    off_pool = C_POOL - C_XBC
    ext = jnp.concatenate([up[:, off_pool:off_pool + POOL_WIDTH], u[:, C_POOL:C_DT],
                           un[:, off_pool:off_pool + POOL_WIDTH]], axis=0)
    n_ext = tm + 2 * POOL_HALO
    sums = {}
    cur = ext
    width = 1
    while width < POOL_WINDOWS[-1]:
        cur = cur + pltpu.roll(cur, n_ext - width, 0)
        width *= 2
        sums[width] = cur
    lane_p = lax.broadcasted_iota(jnp.int32, (1, POOL_WIDTH), 1)
    wsum = None
    for gi, win in enumerate(POOL_WINDOWS):
        start = POOL_HALO - win // 2
        s_w = sums[win]
        if start:
            s_w = pltpu.roll(s_w, n_ext - start, 0)
        s_w = s_w[0:tm]
        wsum = s_w if wsum is None else jnp.where(lane_p >= gi * POOL_GDIM, s_w, wsum)
    pooled = wsum * inv_ref[...] - u[:, C_POOL:C_DT]
    yp_ref[0] = (_bdot(pooled, pw_ref[...]) * ps_ref[...]).astype(BF16)


def _inproj_call(x, sh, sc, w_cat, conv_w, conv_b, dt_bias, a_log, fmix, pw, ps, inv, *, tm, pq_blocked):
    bsz, seq, d = x.shape
    nt = seq // tm
    nb8 = seq // 8
    tb = tm // 8
    mod_map = (lambda b, i: (b, 0, 0)) if sh.shape[0] > 1 else (lambda b, i: (0, 0, 0))

    def full(a):
        return pl.BlockSpec(a.shape, lambda b, i: (0,) * a.ndim)

    in_specs = [
        pl.BlockSpec((1, tm, d), lambda b, i: (b, i, 0)),
        pl.BlockSpec((1, 8, d), lambda b, i: (b, jnp.maximum(i * tb - 1, 0), 0)),
        pl.BlockSpec((1, 8, d), lambda b, i: (b, jnp.minimum((i + 1) * tb, nb8 - 1), 0)),
        pl.BlockSpec((1, 1, d), mod_map),
        pl.BlockSpec((1, 1, d), mod_map),
    ] + [full(a) for a in (w_cat, conv_w, conv_b, dt_bias, a_log, fmix, pw, ps)] + [
        pl.BlockSpec((tm, POOL_WIDTH), lambda b, i: (i, 0))]
    if pq_blocked:
        pq_shape = jax.ShapeDtypeStruct((bsz, 2, seq // DFT_N2, DFT_N2 * FNET_WIDTH), F32)
        pq_spec = pl.BlockSpec((1, 2, tm // DFT_N2, DFT_N2 * FNET_WIDTH), lambda b, i: (b, 0, i, 0))
    else:
        pq_shape = jax.ShapeDtypeStruct((bsz, 2, seq, FNET_WIDTH), F32)
        pq_spec = pl.BlockSpec((1, 2, tm, FNET_WIDTH), lambda b, i: (b, 0, i, 0))
    out_shape = [
        jax.ShapeDtypeStruct((bsz, seq, SSD_WIDTH), BF16),
        jax.ShapeDtypeStruct((bsz, seq, SSD_WIDTH), BF16),
        jax.ShapeDtypeStruct((bsz, seq, BC_WIDTH), BF16),
        jax.ShapeDtypeStruct((bsz, 2, seq, LANE), F32),
        jax.ShapeDtypeStruct((bsz, 2, seq, LANE), F32),
        pq_shape,
        jax.ShapeDtypeStruct((bsz, seq, POOL_WIDTH), BF16),
    ]
    out_specs = [
        pl.BlockSpec((1, tm, SSD_WIDTH), lambda b, i: (b, i, 0)),
        pl.BlockSpec((1, tm, SSD_WIDTH), lambda b, i: (b, i, 0)),
        pl.BlockSpec((1, tm, BC_WIDTH), lambda b, i: (b, i, 0)),
        pl.BlockSpec((1, 2, tm, LANE), lambda b, i: (b, 0, i, 0)),
        pl.BlockSpec((1, 2, tm, LANE), lambda b, i: (b, 0, i, 0)),
        pq_spec,
        pl.BlockSpec((1, tm, POOL_WIDTH), lambda b, i: (b, i, 0)),
    ]
    return pl.pallas_call(
        functools.partial(_inproj_kernel, tm=tm, pq_blocked=pq_blocked),
        grid=(bsz, nt),
        in_specs=in_specs,
        out_specs=out_specs,
        out_shape=out_shape,
        compiler_params=_params("parallel", "parallel"),
        name="inproj",
    )(x, x, x, sh, sc, w_cat, conv_w, conv_b, dt_bias, a_log, fmix, pw, ps, inv)


def _bf16_pieces(x, terms):
    pieces = []
    rem = x
    for t in range(terms):
        piece = rem.astype(BF16)
        pieces.append(piece)
        if t + 1 < terms:
            rem = rem - piece.astype(F32)
    return pieces


def _ssd_kernel(xsf_ref, bcf_ref, dtf_ref, daf_ref, xsb_ref, bcb_ref, dtb_ref, dab_ref, tri_ref, ex_ref, init_ref,
                yf_ref, yb_ref, fin_ref, st_ref, *, kc):
    s = pl.program_id(1)
    ns = pl.num_programs(1)

    @pl.when(s == 0)
    def _():
        st_ref[...] = init_ref[0]

    fwd = (xsf_ref, bcf_ref, dtf_ref, daf_ref, yf_ref)
    bwd = (xsb_ref, bcb_ref, dtb_ref, dab_ref, yb_ref)
    jobs = [(0, k, fwd) for k in range(kc)] + [(1, kc - 1 - k, bwd) for k in range(kc)]
    c = SSD_CHUNK
    gn = SSD_GROUPS * SSD_STATE
    pw = 2 * SSD_HEADDIM
    gw = HEADS_PER_GROUP * SSD_HEADDIM
    low_half = lax.broadcasted_iota(jnp.int32, (1, pw), 1) < SSD_HEADDIM
    zero = jnp.zeros((SSD_CHUNK, pw), BF16)
    tri_b = [tri_ref[d].astype(BF16) for d in range(2)]
    mask = [tri_ref[d] > 0.5 for d in range(2)]

    def rows(k):
        return slice(k * c, (k + 1) * c)

    cs3 = [jnp.dot(tri_b[d], jnp.concatenate(_bf16_pieces(r[3][0, 0, rows(k), :], 3), axis=1),
                   preferred_element_type=F32) for d, k, r in jobs]
    cs = [v[:, 0:LANE] + v[:, LANE:2 * LANE] + v[:, 2 * LANE:3 * LANE] for v in cs3]
    cs_t = [v.T for v in cs]
    ex4 = [jnp.dot(jnp.concatenate(_bf16_pieces(cs[n], 2) + _bf16_pieces(r[2][0, 0, rows(k), :], 2), axis=0),
                   ex_ref[...], preferred_element_type=F32) for n, (d, k, r) in enumerate(jobs)]
    cs_x = [v[0:c] + v[c:2 * c] for v in ex4]
    dt_x = [v[2 * c:3 * c] + v[3 * c:4 * c] for v in ex4]
    tot_x = [cs_x[n][(c - 1 if d == 0 else 0):(c if d == 0 else 1), :] for n, (d, k, r) in enumerate(jobs)]
    e_cs = [jnp.exp(v) for v in cs_x]
    e_tot = [jnp.exp(v) for v in tot_x]
    xd32 = [r[0][0, rows(k), :].astype(F32) * dt_x[n] for n, (d, k, r) in enumerate(jobs)]
    xd_dec = [(xd32[n] * jnp.exp(tot_x[n] - cs_x[n])).astype(BF16) for n in range(len(jobs))]
    xd = [v.astype(BF16) for v in xd32]
    bc = [r[1][0, rows(k), :] for d, k, r in jobs]
    bg_t = [[v[:, g * SSD_STATE:(g + 1) * SSD_STATE].astype(F32).T.astype(BF16) for g in range(SSD_GROUPS)]
            for v in bc]
    cg = [[v[:, gn + g * SSD_STATE:gn + (g + 1) * SSD_STATE] for g in range(SSD_GROUPS)] for v in bc]
    cb = [[jnp.dot(cg[n][g], bg_t[n][g], preferred_element_type=F32) for g in range(SSD_GROUPS)]
          for n in range(len(jobs))]

    def y_diagonal(n, d, g):
        parts = []
        for pp in range(HEADS_PER_GROUP // 2):
            p = g * (HEADS_PER_GROUP // 2) + pp
            decay = []
            for h in (2 * p, 2 * p + 1):
                seg = cs[n][:, h:h + 1] - cs_t[n][h:h + 1, :]
                decay.append(cb[n][g] * jnp.where(mask[d], jnp.exp(jnp.minimum(seg, 0.0)), 0.0))
            m = jnp.concatenate(decay, axis=1).astype(BF16)
            xp = xd[n][:, p * pw:(p + 1) * pw]
            x_bd = jnp.concatenate([jnp.where(low_half, xp, zero), jnp.where(low_half, zero, xp)], axis=0)
            parts.append(jnp.dot(m, x_bd, preferred_element_type=F32))
        return jnp.concatenate(parts, axis=1)

    y_diag = [[y_diagonal(n, d, g) for g in range(SSD_GROUPS)] for n, (d, k, r) in enumerate(jobs)]

    for n, (d, k, r) in enumerate(jobs):
        for g in range(SSD_GROUPS):
            glanes = slice(g * gw, (g + 1) * gw)
            st = st_ref[d, g]
            y_off = jnp.dot(cg[n][g], st.astype(BF16), preferred_element_type=F32) * e_cs[n][:, glanes]
            r[4][0, rows(k), glanes] = (y_diag[n][g] + y_off).astype(BF16)
            st_ref[d, g] = st * e_tot[n][:, glanes] + jnp.dot(bg_t[n][g], xd_dec[n][:, glanes],
                                                              preferred_element_type=F32)

    @pl.when(s == ns - 1)
    def _():
        fin_ref[0] = st_ref[...]


def _ssd_call(xs, bc, dt, da, tri, ex, init):
    bsz, seq, _ = xs.shape
    kc = min(SSD_CHUNKS_PER_STEP, seq // SSD_CHUNK)
    rows = kc * SSD_CHUNK
    ns = seq // rows
    st_block = (1,) + SSD_STATE_SHAPE

    def tok(w, rev):
        return pl.BlockSpec((1, rows, w), lambda b, s: (b, ns - 1 - s if rev else s, 0))

    def per_dir(w, d):
        return pl.BlockSpec((1, 1, rows, w), lambda b, s: (b, d, ns - 1 - s if d else s, 0))

    return pl.pallas_call(
        functools.partial(_ssd_kernel, kc=kc),
        grid=(bsz, ns),
        in_specs=[
            tok(SSD_WIDTH, False), tok(BC_WIDTH, False), per_dir(LANE, 0), per_dir(LANE, 0),
            tok(SSD_WIDTH, True), tok(BC_WIDTH, True), per_dir(LANE, 1), per_dir(LANE, 1),
            pl.BlockSpec(tri.shape, lambda b, s: (0, 0, 0)),
            pl.BlockSpec(ex.shape, lambda b, s: (0, 0)),
            pl.BlockSpec(st_block, lambda b, s: (b, 0, 0, 0, 0)),
        ],
        out_specs=[tok(SSD_WIDTH, False), tok(SSD_WIDTH, True), pl.BlockSpec(st_block, lambda b, s: (b, 0, 0, 0, 0))],
        out_shape=[
            jax.ShapeDtypeStruct((bsz, seq, SSD_WIDTH), BF16),
            jax.ShapeDtypeStruct((bsz, seq, SSD_WIDTH), BF16),
            jax.ShapeDtypeStruct((bsz,) + SSD_STATE_SHAPE, F32),
        ],
        scratch_shapes=[pltpu.VMEM(SSD_STATE_SHAPE, F32)],
        compiler_params=_params("parallel", "arbitrary"),
        name="ssd",
    )(xs, bc, dt, da, xs, bc, dt, da, tri, ex, init)


def _dft1_kernel(p_ref, q_ref, m_ref, tr_ref, ti_ref, *, n1):
    re, im = [], []
    for j in range(DFT_NB):
        m = m_ref[j]
        rp = _bdot(m, p_ref[0, 0, :, j * FNET_WIDTH:(j + 1) * FNET_WIDTH])
        rq = _bdot(m, q_ref[0, 0, :, j * FNET_WIDTH:(j + 1) * FNET_WIDTH])
        re.append(rp[:n1] + rq[n1:])
        im.append(rp[n1:] - rq[:n1])
    tr_ref[0] = jnp.stack(re).reshape(DFT_NB, n1 * FNET_WIDTH).astype(BF16)
    ti_ref[0] = jnp.stack(im).reshape(DFT_NB, n1 * FNET_WIDTH).astype(BF16)


def _dft1_call(pqv, mtab):
    bsz, _, n1, cols_all = pqv.shape
    w = cols_all // DFT_N2
    cols = DFT_NB * w
    t_shape = jax.ShapeDtypeStruct((bsz, DFT_N2, n1 * w), BF16)
    t_spec = pl.BlockSpec((1, DFT_NB, n1 * w), lambda b, j: (b, j, 0))
    return pl.pallas_call(
        functools.partial(_dft1_kernel, n1=n1),
        grid=(bsz, DFT_N2 // DFT_NB),
        in_specs=[
            pl.BlockSpec((1, 1, n1, cols), lambda b, j: (b, 0, 0, j)),
            pl.BlockSpec((1, 1, n1, cols), lambda b, j: (b, 1, 0, j)),
            pl.BlockSpec((DFT_NB, 2 * n1, n1), lambda b, j: (j, 0, 0)),
        ],
        out_specs=[t_spec, t_spec],
        out_shape=[t_shape, t_shape],
        compiler_params=_params("parallel", "parallel"),
        name="dft1",
    )(pqv, pqv, mtab)


def _dft2_kernel(d_ref, *refs, split):
    *t_refs, o_ref = refs
    t = jnp.concatenate([r[0] for r in t_refs], axis=0) if len(t_refs) > 1 else t_refs[0][0]
    res = _bdot(d_ref[...], t)
    if split:
        res = res.reshape(res.shape[0], res.shape[1] // split, split)
    o_ref[0] = res.astype(BF16)


def _dft2_call(dmat, parts, *, tn, split=0):
    bsz, _, n = parts[0].shape
    mo = dmat.shape[0]
    if split:
        out_spec = pl.BlockSpec((1, mo, tn // split, split), lambda b, j: (b, 0, j, 0))
        out_shape = jax.ShapeDtypeStruct((bsz, mo, n // split, split), BF16)
    else:
        out_spec = pl.BlockSpec((1, mo, tn), lambda b, j: (b, 0, j))
        out_shape = jax.ShapeDtypeStruct((bsz, mo, n), BF16)
    return pl.pallas_call(
        functools.partial(_dft2_kernel, split=split),
        grid=(bsz, n // tn),
        in_specs=[pl.BlockSpec((mo, dmat.shape[1]), lambda b, j: (0, 0))] + [
            pl.BlockSpec((1, p.shape[1], tn), lambda b, j: (b, 0, j)) for p in parts],
        out_specs=out_spec,
        out_shape=out_shape,
        compiler_params=_params("parallel", "parallel"),
        name="dft2",
    )(dmat, *parts)


def _mix_kernel(yf_ref, yb_ref, xs_ref, z_ref, yfn_ref, ypl_ref, x_ref, dsk_ref, nw_ref, wo_ref, g1_ref,
                lg_ref, lb_ref, sh2_ref, sc2_ref, x1_ref, h2_ref):
    y = yf_ref[0].astype(F32) + yb_ref[0].astype(F32) + xs_ref[0].astype(F32) * dsk_ref[...]
    z = z_ref[0].astype(F32)
    y = y * (z * jax.nn.sigmoid(z))
    gw = SSD_WIDTH // SSD_GROUPS
    parts = []
    for g in range(SSD_GROUPS):
        yg = y[:, g * gw:(g + 1) * gw]
        ms = jnp.mean(yg * yg, axis=-1, keepdims=True)
        parts.append(yg * lax.rsqrt(ms + EPS))
    yn = jnp.concatenate(parts, axis=-1) * nw_ref[...]
    o_f = SSD_WIDTH
    o_p = SSD_WIDTH + FNET_WIDTH
    mix = (_bdot(yn, wo_ref[0:o_f, :]) + _bdot(yfn_ref[0], wo_ref[o_f:o_p, :])
           + _bdot(ypl_ref[0], wo_ref[o_p:, :]))
    x1 = _ln(ALPHA * x_ref[0] + g1_ref[0] * mix) * lg_ref[...] + lb_ref[...]
    x1_ref[0] = x1
    h2_ref[0] = (_ln(x1) * (1.0 + sc2_ref[0]) + sh2_ref[0]).astype(BF16)


def _mix_call(yf, yb, xs, z, yfn, ypl, x, dsk, nw, wo, g1, lg, lb, sh2, sc2, *, tm):
    bsz, seq, d = x.shape
    mod_map = (lambda b, i: (b, 0, 0)) if g1.shape[0] > 1 else (lambda b, i: (0, 0, 0))

    def full(a):
        return pl.BlockSpec(a.shape, lambda b, i: (0,) * a.ndim)

    def tok(w):
        return pl.BlockSpec((1, tm, w), lambda b, i: (b, i, 0))

    mod = pl.BlockSpec((1, 1, d), mod_map)
    return pl.pallas_call(
        _mix_kernel,
        grid=(bsz, seq // tm),
        in_specs=[
            tok(SSD_WIDTH), tok(SSD_WIDTH), tok(SSD_WIDTH), tok(SSD_WIDTH), tok(FNET_WIDTH), tok(POOL_WIDTH), tok(d),
            full(dsk), full(nw), full(wo), mod, full(lg), full(lb), mod, mod,
        ],
        out_specs=[tok(d), tok(d)],
        out_shape=[jax.ShapeDtypeStruct((bsz, seq, d), F32), jax.ShapeDtypeStruct((bsz, seq, d), BF16)],
        compiler_params=_params("parallel", "parallel"),
        name="mix",
    )(yf, yb, xs, z, yfn, ypl, x, dsk, nw, wo, g1, lg, lb, sh2, sc2)


def _ffn_kernel(h_ref, hp_ref, hn_ref, wu_ref, cw_ref, cb_ref, wd_ref, x1_ref, g2_ref, lg_ref, lb_ref, o_ref,
                hbuf, a0, a1, hid, acc, *, tm, on_grid, period):
    i = pl.program_id(1)
    nt = pl.num_programs(1)
    nj = D_FF // FFN_CB
    ext = tm + 2 * FFN_HALO
    ra = ext // FFN_UP_CHUNKS
    rb = tm // FFN_ROW_CHUNKS
    slots = (a0, a1)
    base = FFN_PAD + FFN_HALO
    rows = (-1, 0, 1) if on_grid else (0,)

    def block_cols(j):
        return slice(j * FFN_CB, (j + 1) * FFN_CB), slice(D_FF + j * FFN_CB, D_FF + (j + 1) * FFN_CB)

    def paired(ref, j):
        return jnp.concatenate([ref[:, cols] for cols in block_cols(j)], axis=1)

    def up_project(a_ref, j):
        for r in range(FFN_UP_CHUNKS):
            hb = hbuf[r * ra:(r + 1) * ra, :]
            for half, cols in enumerate(block_cols(j)):
                a = jnp.dot(hb, wu_ref[:, cols], preferred_element_type=F32)
                a_ref[FFN_PAD + r * ra:FFN_PAD + (r + 1) * ra, half * FFN_CB:(half + 1) * FFN_CB] = a.astype(BF16)

    def conv(a_ref, j, r):
        cw = paired(cw_ref, j).astype(BF16)
        m = FFN_PAD
        n = rb + 2 * m
        start = base + r * rb - m
        col = (lax.broadcasted_iota(jnp.int32, (n, 1), 0) + (r * rb - m)) & (period - 1)
        taps = {dr: a_ref[pl.ds(start + dr * GRID_W, n), :] for dr in rows}

        def column(dc, lo, hi):
            acc_dc = None
            for dr in rows:
                k = (dr + 1) * 3 + dc + 1
                term = taps[dr][lo:hi] * cw[k:k + 1]
                acc_dc = term if acc_dc is None else acc_dc + term
            return acc_dc

        left = column(-1, 0, n)
        right = column(1, 0, n)
        zero = jnp.zeros_like(left)
        left = jnp.where(col == period - 1, zero, left)
        right = jnp.where(col == 0, zero, right)
        left2 = jnp.concatenate([left[n - 2:], left[:n - 2]], axis=0)
        u = (left2 + right)[m + 1:m + 1 + rb]
        return column(0, m, m + rb) + u + paired(cb_ref, j).astype(BF16)

    hbuf[0:FFN_HALO] = jnp.where(i > 0, hp_ref[0], jnp.zeros_like(hp_ref[0]))
    hbuf[FFN_HALO:FFN_HALO + tm] = h_ref[0]
    hbuf[FFN_HALO + tm:] = jnp.where(i < nt - 1, hn_ref[0], jnp.zeros_like(hn_ref[0]))
    for a in slots:
        a[0:FFN_PAD] = jnp.zeros((FFN_PAD, 2 * FFN_CB), BF16)
        a[FFN_PAD + ext:] = jnp.zeros((FFN_PAD, 2 * FFN_CB), BF16)

    up_project(slots[0], 0)
    for j in range(nj):
        if j + 1 < nj:
            up_project(slots[(j + 1) % 2], j + 1)
        pos = j % FFN_DOWN_BLOCKS
        for r in range(FFN_ROW_CHUNKS):
            c = conv(slots[j % 2], j, r)
            hidden = jax.nn.gelu(c[:, FFN_CB:], approximate=True) * c[:, :FFN_CB]
            hid[r * rb:(r + 1) * rb, pos * FFN_CB:(pos + 1) * FFN_CB] = hidden
        if pos == FFN_DOWN_BLOCKS - 1 or j == nj - 1:
            first = j - pos
            k = (pos + 1) * FFN_CB
            part = jnp.dot(hid[:, 0:k], wd_ref[first * FFN_CB:first * FFN_CB + k, :], preferred_element_type=F32)
            if first == 0:
                acc[...] = part
            else:
                acc[...] += part

    o_ref[0] = _ln(ALPHA * x1_ref[0] + g2_ref[0] * acc[...]) * lg_ref[...] + lb_ref[...]


def _ffn_call(h, w_up, conv_w9, conv_b, w_down, x1, g2, lg, lb, *, tm, on_grid, period):
    bsz, seq, d = h.shape
    nt = seq // tm
    th = tm // FFN_HALO
    nh = seq // FFN_HALO
    ext = tm + 2 * FFN_HALO
    mod_map = (lambda b, i: (b, 0, 0)) if g2.shape[0] > 1 else (lambda b, i: (0, 0, 0))

    def resident(a):
        return pl.BlockSpec(a.shape, lambda b, i: (0,) * a.ndim, pipeline_mode=pl.Buffered(1))

    a_scratch = pltpu.VMEM((ext + 2 * FFN_PAD, 2 * FFN_CB), BF16)
    return pl.pallas_call(
        functools.partial(_ffn_kernel, tm=tm, on_grid=on_grid, period=period),
        grid=(bsz, nt),
        in_specs=[
            pl.BlockSpec((1, tm, d), lambda b, i: (b, i, 0)),
            pl.BlockSpec((1, FFN_HALO, d), lambda b, i: (b, jnp.maximum(i * th - 1, 0), 0)),
            pl.BlockSpec((1, FFN_HALO, d), lambda b, i: (b, jnp.minimum((i + 1) * th, nh - 1), 0)),
            resident(w_up), resident(conv_w9), resident(conv_b), resident(w_down),
            pl.BlockSpec((1, tm, d), lambda b, i: (b, i, 0)),
            pl.BlockSpec((1, 1, d), mod_map),
            resident(lg), resident(lb),
        ],
        out_specs=pl.BlockSpec((1, tm, d), lambda b, i: (b, i, 0)),
        out_shape=jax.ShapeDtypeStruct((bsz, seq, d), F32),
        scratch_shapes=[pltpu.VMEM((ext, d), BF16), a_scratch, a_scratch,
                        pltpu.VMEM((tm, FFN_DOWN_BLOCKS * FFN_CB), BF16),
                        pltpu.VMEM((tm, d), F32)],
        compiler_params=_params("parallel", "arbitrary"),
        name="ffn",
    )(h, h, h, w_up, conv_w9, conv_b, w_down, x1, g2, lg, lb)


def _block_diag(blocks):
    g, n, m = blocks.shape
    out = jnp.zeros((g * n, g * m), blocks.dtype)
    for k in range(g):
        out = out.at[k * n:(k + 1) * n, k * m:(k + 1) * m].set(blocks[k])
    return out


def _dft_tables(seq):
    n1 = seq // DFT_N2
    k1 = np.arange(n1, dtype=np.int64)[:, None]
    nn1 = np.arange(n1, dtype=np.int64)[None, :]
    tabs = []
    for n2 in range(DFT_N2):
        th = 2.0 * np.pi * ((k1 * (DFT_N2 * nn1 + n2)) % seq) / seq
        tabs.append(np.concatenate([np.cos(th), -np.sin(th)], axis=0))
    mtab = np.stack(tabs)
    k2 = np.arange(DFT_N2, dtype=np.int64)[:, None]
    nn2 = np.arange(DFT_N2, dtype=np.int64)[None, :]
    th2 = 2.0 * np.pi * ((k2 * nn2) % DFT_N2) / DFT_N2
    scale = 1.0 / math.sqrt(seq * FNET_GDIM)
    dmat = scale * np.concatenate([np.cos(th2), np.sin(th2)], axis=1)
    return jnp.asarray(mtab, F32), jnp.asarray(dmat, F32)


def _dense_dft_table(seq):
    k = np.arange(seq, dtype=np.int64)
    th = 2.0 * np.pi * ((k[:, None] * k[None, :]) % seq) / seq
    scale = 1.0 / math.sqrt(seq * FNET_GDIM)
    return jnp.asarray(scale * np.concatenate([np.cos(th), -np.sin(th)], axis=1), F32)


def _channel_dft_tables():
    k = np.arange(FNET_GDIM, dtype=np.int64)
    th = 2.0 * np.pi * ((k[:, None] * k[None, :]) % FNET_GDIM) / FNET_GDIM
    eye = np.eye(FNET_GROUPS)
    return jnp.asarray(np.kron(eye, np.cos(th)), F32), jnp.asarray(np.kron(eye, np.sin(th)), F32)


def _pool_inverse_counts(seq):
    pos = np.arange(seq, dtype=np.int64)
    cols = []
    for win in POOL_WINDOWS:
        left = win // 2
        cnt = np.minimum(pos + (win - left), seq) - np.maximum(pos - left, 0)
        cols.append(np.repeat((1.0 / cnt)[:, None], POOL_GDIM, axis=1))
    return jnp.asarray(np.concatenate(cols, axis=1), F32)


def _scan_matrices():
    r = np.arange(SSD_CHUNK)
    lower = (r[:, None] >= r[None, :]).astype(np.float32)
    return jnp.asarray(np.stack([lower, lower.T]), F32)


def _head_expansion():
    ex = np.zeros((LANE, SSD_WIDTH), np.float32)
    for h in range(SSD_HEADS):
        ex[h, h * SSD_HEADDIM:(h + 1) * SSD_HEADDIM] = 1.0
    return jnp.asarray(ex, BF16)


def _token_mixer(x, sh1, sc1, lw, init, tri, ex, *, tm):
    bsz, seq, _ = x.shape
    z, xs, bc, dt, da, pq, ypl = _inproj_call(
        x, sh1, sc1, lw["w_cat"], lw["ssd_conv_w"], lw["ssd_conv_b"], lw["dt_bias"], lw["a_log"], lw["fmix"],
        lw["pool_bd"], lw["pool_scale"], _pool_inverse_counts(seq), tm=tm, pq_blocked=_two_stage_dft(seq))
    yf, yb, fin = _ssd_call(xs, bc, dt, da, tri, ex, init)
    return z, xs, yf, yb, pq, ypl, fin


def _two_stage_dft(seq):
    return seq % (DFT_N2 * 8) == 0


def _fourier_positions(pq, seq):
    bsz, w = pq.shape[0], FNET_WIDTH
    if _two_stage_dft(seq):
        n1 = seq // DFT_N2
        mtab, dmat = _dft_tables(seq)
        t_re, t_im = _dft1_call(pq, mtab)
        out = _dft2_call(dmat, [t_re, t_im], tn=min(4096, n1 * w), split=w)
        return out.reshape(bsz, seq, w)
    return _dft2_call(_dense_dft_table(seq), [pq.reshape(bsz, 2 * seq, w)], tn=w)


def kernel(x, c, ctx, c_ctx, w_ada, b_ada, w_in, ssd_conv_w, ssd_conv_b, ssd_dt_bias, ssd_a_log, ssd_d,
           ssd_norm_w, fnet_w, pool_w, pool_scale, w_out, ln1_g, ln1_b, ffn_w_up, ffn_conv_w, ffn_conv_b,
           ffn_w_down, ln2_g, ln2_b):
    bsz, seq, d = x.shape
    n_ctx = ctx.shape[1]
    assert d == D_MODEL and bsz <= 7 and seq % (GRID_W * 8) == 0 and n_ctx % SSD_CHUNK == 0
    tm_lat = min(1024, seq)
    tm_ffn = min(1024, seq)

    cv = jnp.zeros((8, d), F32).at[0:bsz].set(c).at[bsz].set(c_ctx)
    mods = _ada_call(cv, w_ada, b_ada)
    tri = _scan_matrices()
    ex = _head_expansion()
    fmix = _fmix_call(*_channel_dft_tables(), jnp.stack([_block_diag(fnet_w[l]) for l in range(DEPTH)]))
    zero_state = jnp.zeros((bsz,) + SSD_STATE_SHAPE, F32)
    lane_pad = jnp.zeros((d, LANE - SSD_HEADS), F32)

    for l in range(DEPTH):
        last = l == DEPTH - 1
        m = mods[l].reshape(8, 6, d)
        lat = [m[0:bsz, k][:, None, :] for k in range(6)]
        cx = [m[bsz:bsz + 1, k][:, None, :] for k in range(6)]
        w = w_in[l]
        w_cat = jnp.concatenate(
            [w[:, 0:OFF_DT], w[:, OFF_FNET:N_IN], w[:, OFF_DT:OFF_DT + SSD_HEADS], lane_pad,
             w[:, OFF_DT + SSD_HEADS:OFF_FNET], lane_pad], axis=1).astype(BF16)
        pad16 = ((0, 0), (0, LANE - SSD_HEADS))
        lw = {
            "w_cat": w_cat,
            "ssd_conv_w": ssd_conv_w[l],
            "ssd_conv_b": ssd_conv_b[l][None, :],
            "dt_bias": jnp.pad(ssd_dt_bias[l], pad16),
            "a_log": jnp.pad(ssd_a_log[l], pad16),
            "fmix": fmix[l],
            "pool_bd": _block_diag(pool_w[l]).astype(BF16),
            "pool_scale": pool_scale[l][None, :],
        }
        dsk = jnp.repeat(ssd_d[l], SSD_HEADDIM)[None, :]
        nw = ssd_norm_w[l][None, :]
        wo = w_out[l].astype(BF16)
        lg1, lb1 = ln1_g[l][None, :], ln1_b[l][None, :]
        lg2, lb2 = ln2_g[l][None, :], ln2_b[l][None, :]
        w_up = ffn_w_up[l].astype(BF16)
        w_down = ffn_w_down[l].astype(BF16)
        conv9 = ffn_conv_w[l].reshape(9, 2 * D_FF)
        conv_b = ffn_conv_b[l][None, :]

        cz, cxs, cyf, cyb, cpq, cypl, c_fin = _token_mixer(ctx, cx[0], cx[1], lw, zero_state, tri, ex, tm=n_ctx)
        z, xs, yf, yb, pq, ypl, _ = _token_mixer(x, lat[0], lat[1], lw, c_fin, tri, ex, tm=tm_lat)

        yfn = _fourier_positions(pq, seq)
        x1, h2 = _mix_call(yf, yb, xs, z, yfn, ypl, x, dsk, nw, wo, lat[2], lg1, lb1, lat[3], lat[4], tm=tm_lat)
        x = _ffn_call(h2, w_up, conv9, conv_b, w_down, x1, lat[5], lg2, lb2, tm=tm_ffn, on_grid=True, period=GRID_W)

        if not last:
            cyfn = _fourier_positions(cpq, n_ctx)
            c1, ch2 = _mix_call(cyf, cyb, cxs, cz, cyfn, cypl, ctx, dsk, nw, wo, cx[2], lg1, lb1, cx[3], cx[4],
                                tm=n_ctx)
            ctx = _ffn_call(ch2.reshape(1, bsz * n_ctx, d), w_up, conv9, conv_b, w_down, c1.reshape(1, bsz * n_ctx, d),
                            cx[5], lg2, lb2, tm=bsz * n_ctx, on_grid=False, period=n_ctx).reshape(bsz, n_ctx, d)
    return x
```

```python
import functools
import math

import numpy as np
import jax
import jax.numpy as jnp
from jax import lax
from jax.experimental import pallas as pl
from jax.experimental.pallas import tpu as pltpu

F32 = jnp.float32
BF16 = jnp.bfloat16
HIGHEST = lax.Precision.HIGHEST

D_MODEL = 1024
DEPTH = 2
GRID_W = 64
SSD_WIDTH = 512
SSD_HEADDIM = 64
SSD_HEADS = 8
SSD_GROUPS = 2
HEADS_PER_GROUP = SSD_HEADS // SSD_GROUPS
SSD_STATE = 128
SSD_CHUNK = 128
SSD_CHUNKS_PER_STEP = 8
XBC_WIDTH = SSD_WIDTH + 2 * SSD_GROUPS * SSD_STATE
BC_WIDTH = 2 * SSD_GROUPS * SSD_STATE
SSD_STATE_SHAPE = (2, SSD_GROUPS, SSD_STATE, HEADS_PER_GROUP * SSD_HEADDIM)
FNET_WIDTH = 256
FNET_GROUPS = 4
FNET_GDIM = 64
POOL_WINDOWS = (2, 4, 8, 16)
POOL_WIDTH = 256
POOL_GDIM = 64
POOL_HALO = 8
D_FF = 2816
ALPHA = (2.0 * DEPTH) ** 0.25
EPS = 1e-6

OFF_XBC = SSD_WIDTH
OFF_DT = OFF_XBC + XBC_WIDTH
OFF_FNET = OFF_DT + 2 * SSD_HEADS
OFF_POOL = OFF_FNET + FNET_WIDTH
N_IN = OFF_POOL + POOL_WIDTH

LANE = 128
C_Z = 0
C_XBC = C_Z + SSD_WIDTH
C_FNET = C_XBC + XBC_WIDTH
C_POOL = C_FNET + FNET_WIDTH
C_DT = C_POOL + POOL_WIDTH
N_CAT = C_DT + 2 * LANE

DFT_N2 = 64
DFT_NB = 16
DFT2_TN = 4096
ADA_TN = 1536
FFN_CB = 256
FFN_HALO = GRID_W
FFN_PAD = 16
FFN_ROW_CHUNKS = 1
FFN_UP_CHUNKS = 1
FFN_DOWN_BLOCKS = 4
VMEM_LIMIT = 56 * 1024 * 1024


def _ln(x):
    mu = jnp.mean(x, axis=-1, keepdims=True)
    xc = x - mu
    var = jnp.mean(xc * xc, axis=-1, keepdims=True)
    return xc * lax.rsqrt(var + EPS)


def _bdot(a, b):
    return jnp.dot(a.astype(BF16), b.astype(BF16), preferred_element_type=F32)


def _params(*sem):
    return pltpu.CompilerParams(dimension_semantics=sem, vmem_limit_bytes=VMEM_LIMIT)


def _ada_kernel(c_ref, w_ref, b_ref, o_ref):
    cv = c_ref[...]
    s = cv * jax.nn.sigmoid(cv)
    o_ref[0] = jnp.dot(s, w_ref[0], preferred_element_type=F32, precision=HIGHEST) + b_ref[0]


def _ada_call(cv, w_ada, b_ada):
    depth, d, n = w_ada.shape
    tn = ADA_TN
    return pl.pallas_call(
        _ada_kernel,
        grid=(depth, n // tn),
        in_specs=[
            pl.BlockSpec((8, d), lambda l, j: (0, 0)),
            pl.BlockSpec((1, d, tn), lambda l, j: (l, 0, j)),
            pl.BlockSpec((1, 1, tn), lambda l, j: (l, 0, j)),
        ],
        out_specs=pl.BlockSpec((1, 8, tn), lambda l, j: (l, 0, j)),
        out_shape=jax.ShapeDtypeStruct((depth, 8, n), F32),
        compiler_params=_params("parallel", "parallel"),
        name="ada",
    )(cv, w_ada, b_ada.reshape(depth, 1, n))


def _fmix_kernel(cc_ref, cs_ref, fw_ref, o_ref):
    fw = fw_ref[0]
    o_ref[0, 0] = jnp.dot(cc_ref[...], fw, preferred_element_type=F32, precision=HIGHEST).astype(BF16)
    o_ref[0, 1] = jnp.dot(cs_ref[...], fw, preferred_element_type=F32, precision=HIGHEST).astype(BF16)


def _fmix_call(cc, cs, fw_bd):
    depth, w, _ = fw_bd.shape
    return pl.pallas_call(
        _fmix_kernel,
        grid=(depth,),
        in_specs=[
            pl.BlockSpec((w, w), lambda l: (0, 0)),
            pl.BlockSpec((w, w), lambda l: (0, 0)),
            pl.BlockSpec((1, w, w), lambda l: (l, 0, 0)),
        ],
        out_specs=pl.BlockSpec((1, 2, w, w), lambda l: (l, 0, 0, 0)),
        out_shape=jax.ShapeDtypeStruct((depth, 2, w, w), BF16),
        compiler_params=_params("parallel"),
        name="fmix",
    )(cc, cs, fw_bd)


def _inproj_kernel(x_ref, xp_ref, xn_ref, sh_ref, sc_ref, w_ref, cw_ref, cb_ref, dtb_ref, alog_ref,
                   fm_ref, pw_ref, ps_ref, inv_ref,
                   z_ref, xs_ref, bc_ref, dt_ref, da_ref, pq_ref, yp_ref, *, tm, pq_blocked):
    i = pl.program_id(1)
    nt = pl.num_programs(1)
    sh = sh_ref[0]
    sc = sc_ref[0]

    def modulated(xv):
        return (_ln(xv) * (1.0 + sc) + sh).astype(BF16)

    x_ext = jnp.concatenate([xp_ref[0], x_ref[0], xn_ref[0]], axis=0)
    u_ext = jnp.dot(modulated(x_ext), w_ref[...], preferred_element_type=F32)
    u = u_ext[POOL_HALO:POOL_HALO + tm]
    up = jnp.where(i > 0, u_ext[0:POOL_HALO, C_XBC:C_DT], 0.0)
    un = jnp.where(i < nt - 1, u_ext[POOL_HALO + tm:, C_XBC:C_DT], 0.0)

    z_ref[0] = u[:, C_Z:C_XBC].astype(BF16)

    xbc = u[:, C_XBC:C_FNET]
    row = lax.broadcasted_iota(jnp.int32, (tm, 1), 0)
    prev_row = up[POOL_HALO - 1:POOL_HALO, 0:XBC_WIDTH]
    next_row = un[0:1, 0:XBC_WIDTH]
    xm1 = jnp.where(row == 0, prev_row, pltpu.roll(xbc, 1, 0))
    xp1 = jnp.where(row == tm - 1, next_row, pltpu.roll(xbc, tm - 1, 0))
    cw = cw_ref[...]
    conv = xm1 * cw[0:1] + xbc * cw[1:2] + xp1 * cw[2:3] + cb_ref[...]
    act = conv * jax.nn.sigmoid(conv)
    xs_ref[0] = act[:, 0:SSD_WIDTH].astype(BF16)
    bc_ref[0] = act[:, SSD_WIDTH:XBC_WIDTH].astype(BF16)

    lane = lax.broadcasted_iota(jnp.int32, (1, LANE), 1)
    for d in range(2):
        raw = u[:, C_DT + d * LANE:C_DT + (d + 1) * LANE] + dtb_ref[d:d + 1, :]
        sp = jnp.maximum(raw, 0.0) + jnp.log(1.0 + jnp.exp(-jnp.abs(raw)))
        dt = jnp.where(lane < SSD_HEADS, sp, 0.0)
        dt_ref[0, d] = dt
        da_ref[0, d] = dt * (-jnp.exp(alog_ref[d:d + 1, :]))

    uf = u[:, C_FNET:C_POOL].astype(BF16)
    for part in range(2):
        pq = jnp.dot(uf, fm_ref[part], preferred_element_type=F32)
        if pq_blocked:
            pq = pq.reshape(tm // DFT_N2, DFT_N2, FNET_WIDTH).reshape(tm // DFT_N2, DFT_N2 * FNET_WIDTH)
        pq_ref[0, part] = pq

    off_pool = C_POOL - C_XBC
    ext = jnp.concatenate([up[:, off_pool:off_pool + POOL_WIDTH], u[:, C_POOL:C_DT],
                           un[:, off_pool:off_pool + POOL_WIDTH]], axis=0)
    n_ext = tm + 2 * POOL_HALO
    sums = {}
    cur = ext
    width = 1
    while width < POOL_WINDOWS[-1]:
        cur = cur + pltpu.roll(cur, n_ext - width, 0)
        width *= 2
        sums[width] = cur
    lane_p = lax.broadcasted_iota(jnp.int32, (1, POOL_WIDTH), 1)
    wsum = None
    for gi, win in enumerate(POOL_WINDOWS):
        start = POOL_HALO - win // 2
        s_w = sums[win]
        if start:
            s_w = pltpu.roll(s_w, n_ext - start, 0)
        s_w = s_w[0:tm]
        wsum = s_w if wsum is None else jnp.where(lane_p >= gi * POOL_GDIM, s_w, wsum)
    pooled = wsum * inv_ref[...] - u[:, C_POOL:C_DT]
    yp_ref[0] = (_bdot(pooled, pw_ref[...]) * ps_ref[...]).astype(BF16)


def _inproj_call(x, sh, sc, w_cat, conv_w, conv_b, dt_bias, a_log, fmix, pw, ps, inv, *, tm, pq_blocked):
    bsz, seq, d = x.shape
    nt = seq // tm
    nb8 = seq // 8
    tb = tm // 8
    mod_map = (lambda b, i: (b, 0, 0)) if sh.shape[0] > 1 else (lambda b, i: (0, 0, 0))

    def full(a):
        return pl.BlockSpec(a.shape, lambda b, i: (0,) * a.ndim)

    in_specs = [
        pl.BlockSpec((1, tm, d), lambda b, i: (b, i, 0)),
        pl.BlockSpec((1, 8, d), lambda b, i: (b, jnp.maximum(i * tb - 1, 0), 0)),
        pl.BlockSpec((1, 8, d), lambda b, i: (b, jnp.minimum((i + 1) * tb, nb8 - 1), 0)),
        pl.BlockSpec((1, 1, d), mod_map),
        pl.BlockSpec((1, 1, d), mod_map),
    ] + [full(a) for a in (w_cat, conv_w, conv_b, dt_bias, a_log, fmix, pw, ps)] + [
        pl.BlockSpec((tm, POOL_WIDTH), lambda b, i: (i, 0))]
    if pq_blocked:
        pq_shape = jax.ShapeDtypeStruct((bsz, 2, seq // DFT_N2, DFT_N2 * FNET_WIDTH), F32)
        pq_spec = pl.BlockSpec((1, 2, tm // DFT_N2, DFT_N2 * FNET_WIDTH), lambda b, i: (b, 0, i, 0))
    else:
        pq_shape = jax.ShapeDtypeStruct((bsz, 2, seq, FNET_WIDTH), F32)
        pq_spec = pl.BlockSpec((1, 2, tm, FNET_WIDTH), lambda b, i: (b, 0, i, 0))
    out_shape = [
        jax.ShapeDtypeStruct((bsz, seq, SSD_WIDTH), BF16),
        jax.ShapeDtypeStruct((bsz, seq, SSD_WIDTH), BF16),
        jax.ShapeDtypeStruct((bsz, seq, BC_WIDTH), BF16),
        jax.ShapeDtypeStruct((bsz, 2, seq, LANE), F32),
        jax.ShapeDtypeStruct((bsz, 2, seq, LANE), F32),
        pq_shape,
        jax.ShapeDtypeStruct((bsz, seq, POOL_WIDTH), BF16),
    ]
    out_specs = [
        pl.BlockSpec((1, tm, SSD_WIDTH), lambda b, i: (b, i, 0)),
        pl.BlockSpec((1, tm, SSD_WIDTH), lambda b, i: (b, i, 0)),
        pl.BlockSpec((1, tm, BC_WIDTH), lambda b, i: (b, i, 0)),
        pl.BlockSpec((1, 2, tm, LANE), lambda b, i: (b, 0, i, 0)),
        pl.BlockSpec((1, 2, tm, LANE), lambda b, i: (b, 0, i, 0)),
        pq_spec,
        pl.BlockSpec((1, tm, POOL_WIDTH), lambda b, i: (b, i, 0)),
    ]
    return pl.pallas_call(
        functools.partial(_inproj_kernel, tm=tm, pq_blocked=pq_blocked),
        grid=(bsz, nt),
        in_specs=in_specs,
        out_specs=out_specs,
        out_shape=out_shape,
        compiler_params=_params("parallel", "parallel"),
        name="inproj",
    )(x, x, x, sh, sc, w_cat, conv_w, conv_b, dt_bias, a_log, fmix, pw, ps, inv)


def _bf16_pieces(x, terms):
    pieces = []
    rem = x
    for t in range(terms):
        piece = rem.astype(BF16)
        pieces.append(piece)
        if t + 1 < terms:
            rem = rem - piece.astype(F32)
    return pieces


def _ssd_kernel(xsf_ref, bcf_ref, dtf_ref, daf_ref, xsb_ref, bcb_ref, dtb_ref, dab_ref, tri_ref, ex_ref, init_ref,
                yf_ref, yb_ref, fin_ref, st_ref, *, kc):
    s = pl.program_id(1)
    ns = pl.num_programs(1)

    @pl.when(s == 0)
    def _():
        st_ref[...] = init_ref[0]

    fwd = (xsf_ref, bcf_ref, dtf_ref, daf_ref, yf_ref)
    bwd = (xsb_ref, bcb_ref, dtb_ref, dab_ref, yb_ref)
    jobs = [(0, k, fwd) for k in range(kc)] + [(1, kc - 1 - k, bwd) for k in range(kc)]
    c = SSD_CHUNK
    gn = SSD_GROUPS * SSD_STATE
    pw = 2 * SSD_HEADDIM
    gw = HEADS_PER_GROUP * SSD_HEADDIM
    low_half = lax.broadcasted_iota(jnp.int32, (1, pw), 1) < SSD_HEADDIM
    zero = jnp.zeros((SSD_CHUNK, pw), BF16)
    tri_b = [tri_ref[d].astype(BF16) for d in range(2)]
    mask = [tri_ref[d] > 0.5 for d in range(2)]

    def rows(k):
        return slice(k * c, (k + 1) * c)

    cs3 = [jnp.dot(tri_b[d], jnp.concatenate(_bf16_pieces(r[3][0, 0, rows(k), :], 3), axis=1),
                   preferred_element_type=F32) for d, k, r in jobs]
    cs = [v[:, 0:LANE] + v[:, LANE:2 * LANE] + v[:, 2 * LANE:3 * LANE] for v in cs3]
    cs_t = [v.T for v in cs]
    ex4 = [jnp.dot(jnp.concatenate(_bf16_pieces(cs[n], 2) + _bf16_pieces(r[2][0, 0, rows(k), :], 2), axis=0),
                   ex_ref[...], preferred_element_type=F32) for n, (d, k, r) in enumerate(jobs)]
    cs_x = [v[0:c] + v[c:2 * c] for v in ex4]
    dt_x = [v[2 * c:3 * c] + v[3 * c:4 * c] for v in ex4]
    tot_x = [cs_x[n][(c - 1 if d == 0 else 0):(c if d == 0 else 1), :] for n, (d, k, r) in enumerate(jobs)]
    e_cs = [jnp.exp(v) for v in cs_x]
    e_tot = [jnp.exp(v) for v in tot_x]
    xd32 = [r[0][0, rows(k), :].astype(F32) * dt_x[n] for n, (d, k, r) in enumerate(jobs)]
    xd_dec = [(xd32[n] * jnp.exp(tot_x[n] - cs_x[n])).astype(BF16) for n in range(len(jobs))]
    xd = [v.astype(BF16) for v in xd32]
    bc = [r[1][0, rows(k), :] for d, k, r in jobs]
    bg_t = [[v[:, g * SSD_STATE:(g + 1) * SSD_STATE].astype(F32).T.astype(BF16) for g in range(SSD_GROUPS)]
            for v in bc]
    cg = [[v[:, gn + g * SSD_STATE:gn + (g + 1) * SSD_STATE] for g in range(SSD_GROUPS)] for v in bc]
    cb = [[jnp.dot(cg[n][g], bg_t[n][g], preferred_element_type=F32) for g in range(SSD_GROUPS)]
          for n in range(len(jobs))]

    def y_diagonal(n, d, g):
        parts = []
        for pp in range(HEADS_PER_GROUP // 2):
            p = g * (HEADS_PER_GROUP // 2) + pp
            decay = []
            for h in (2 * p, 2 * p + 1):
                seg = cs[n][:, h:h + 1] - cs_t[n][h:h + 1, :]
                decay.append(cb[n][g] * jnp.where(mask[d], jnp.exp(jnp.minimum(seg, 0.0)), 0.0))
            m = jnp.concatenate(decay, axis=1).astype(BF16)
            xp = xd[n][:, p * pw:(p + 1) * pw]
            x_bd = jnp.concatenate([jnp.where(low_half, xp, zero), jnp.where(low_half, zero, xp)], axis=0)
            parts.append(jnp.dot(m, x_bd, preferred_element_type=F32))
        return jnp.concatenate(parts, axis=1)

    y_diag = [[y_diagonal(n, d, g) for g in range(SSD_GROUPS)] for n, (d, k, r) in enumerate(jobs)]

    for n, (d, k, r) in enumerate(jobs):
        for g in range(SSD_GROUPS):
            glanes = slice(g * gw, (g + 1) * gw)
            st = st_ref[d, g]
            y_off = jnp.dot(cg[n][g], st.astype(BF16), preferred_element_type=F32) * e_cs[n][:, glanes]
            r[4][0, rows(k), glanes] = (y_diag[n][g] + y_off).astype(BF16)
            st_ref[d, g] = st * e_tot[n][:, glanes] + jnp.dot(bg_t[n][g], xd_dec[n][:, glanes],
                                                              preferred_element_type=F32)

    @pl.when(s == ns - 1)
    def _():
        fin_ref[0] = st_ref[...]


def _ssd_call(xs, bc, dt, da, tri, ex, init):
    bsz, seq, _ = xs.shape
    kc = min(SSD_CHUNKS_PER_STEP, seq // SSD_CHUNK)
    rows = kc * SSD_CHUNK
    ns = seq // rows
    st_block = (1,) + SSD_STATE_SHAPE

    def tok(w, rev):
        return pl.BlockSpec((1, rows, w), lambda b, s: (b, ns - 1 - s if rev else s, 0))

    def per_dir(w, d):
        return pl.BlockSpec((1, 1, rows, w), lambda b, s: (b, d, ns - 1 - s if d else s, 0))

    return pl.pallas_call(
        functools.partial(_ssd_kernel, kc=kc),
        grid=(bsz, ns),
        in_specs=[
            tok(SSD_WIDTH, False), tok(BC_WIDTH, False), per_dir(LANE, 0), per_dir(LANE, 0),
            tok(SSD_WIDTH, True), tok(BC_WIDTH, True), per_dir(LANE, 1), per_dir(LANE, 1),
            pl.BlockSpec(tri.shape, lambda b, s: (0, 0, 0)),
            pl.BlockSpec(ex.shape, lambda b, s: (0, 0)),
            pl.BlockSpec(st_block, lambda b, s: (b, 0, 0, 0, 0)),
        ],
        out_specs=[tok(SSD_WIDTH, False), tok(SSD_WIDTH, True), pl.BlockSpec(st_block, lambda b, s: (b, 0, 0, 0, 0))],
        out_shape=[
            jax.ShapeDtypeStruct((bsz, seq, SSD_WIDTH), BF16),
            jax.ShapeDtypeStruct((bsz, seq, SSD_WIDTH), BF16),
            jax.ShapeDtypeStruct((bsz,) + SSD_STATE_SHAPE, F32),
        ],
        scratch_shapes=[pltpu.VMEM(SSD_STATE_SHAPE, F32)],
        compiler_params=_params("parallel", "arbitrary"),
        name="ssd",
    )(xs, bc, dt, da, xs, bc, dt, da, tri, ex, init)


def _dft1_kernel(p_ref, q_ref, m_ref, tr_ref, ti_ref, *, n1):
    re, im = [], []
    for j in range(DFT_NB):
        m = m_ref[j]
        rp = _bdot(m, p_ref[0, 0, :, j * FNET_WIDTH:(j + 1) * FNET_WIDTH])
        rq = _bdot(m, q_ref[0, 0, :, j * FNET_WIDTH:(j + 1) * FNET_WIDTH])
        re.append(rp[:n1] + rq[n1:])
        im.append(rp[n1:] - rq[:n1])
    tr_ref[0] = jnp.stack(re).reshape(DFT_NB, n1 * FNET_WIDTH).astype(BF16)
    ti_ref[0] = jnp.stack(im).reshape(DFT_NB, n1 * FNET_WIDTH).astype(BF16)


def _dft1_call(pqv, mtab):
    bsz, _, n1, cols_all = pqv.shape
    w = cols_all // DFT_N2
    cols = DFT_NB * w
    t_shape = jax.ShapeDtypeStruct((bsz, DFT_N2, n1 * w), BF16)
    t_spec = pl.BlockSpec((1, DFT_NB, n1 * w), lambda b, j: (b, j, 0))
    return pl.pallas_call(
        functools.partial(_dft1_kernel, n1=n1),
        grid=(bsz, DFT_N2 // DFT_NB),
        in_specs=[
            pl.BlockSpec((1, 1, n1, cols), lambda b, j: (b, 0, 0, j)),
            pl.BlockSpec((1, 1, n1, cols), lambda b, j: (b, 1, 0, j)),
            pl.BlockSpec((DFT_NB, 2 * n1, n1), lambda b, j: (j, 0, 0)),
        ],
        out_specs=[t_spec, t_spec],
        out_shape=[t_shape, t_shape],
        compiler_params=_params("parallel", "parallel"),
        name="dft1",
    )(pqv, pqv, mtab)


def _dft2_kernel(d_ref, *refs, split):
    *t_refs, o_ref = refs
    t = jnp.concatenate([r[0] for r in t_refs], axis=0) if len(t_refs) > 1 else t_refs[0][0]
    res = _bdot(d_ref[...], t)
    if split:
        res = res.reshape(res.shape[0], res.shape[1] // split, split)
    o_ref[0] = res.astype(BF16)


def _dft2_call(dmat, parts, *, tn, split=0):
    bsz, _, n = parts[0].shape
    mo = dmat.shape[0]
    if split:
        out_spec = pl.BlockSpec((1, mo, tn // split, split), lambda b, j: (b, 0, j, 0))
        out_shape = jax.ShapeDtypeStruct((bsz, mo, n // split, split), BF16)
    else:
        out_spec = pl.BlockSpec((1, mo, tn), lambda b, j: (b, 0, j))
        out_shape = jax.ShapeDtypeStruct((bsz, mo, n), BF16)
    return pl.pallas_call(
        functools.partial(_dft2_kernel, split=split),
        grid=(bsz, n // tn),
        in_specs=[pl.BlockSpec((mo, dmat.shape[1]), lambda b, j: (0, 0))] + [
            pl.BlockSpec((1, p.shape[1], tn), lambda b, j: (b, 0, j)) for p in parts],
        out_specs=out_spec,
        out_shape=out_shape,
        compiler_params=_params("parallel", "parallel"),
        name="dft2",
    )(dmat, *parts)


def _mix_kernel(yf_ref, yb_ref, xs_ref, z_ref, yfn_ref, ypl_ref, x_ref, dsk_ref, nw_ref, wo_ref, g1_ref,
                lg_ref, lb_ref, sh2_ref, sc2_ref, x1_ref, h2_ref):
    y = yf_ref[0].astype(F32) + yb_ref[0].astype(F32) + xs_ref[0].astype(F32) * dsk_ref[...]
    z = z_ref[0].astype(F32)
    y = y * (z * jax.nn.sigmoid(z))
    gw = SSD_WIDTH // SSD_GROUPS
    parts = []
    for g in range(SSD_GROUPS):
        yg = y[:, g * gw:(g + 1) * gw]
        ms = jnp.mean(yg * yg, axis=-1, keepdims=True)
        parts.append(yg * lax.rsqrt(ms + EPS))
    yn = jnp.concatenate(parts, axis=-1) * nw_ref[...]
    o_f = SSD_WIDTH
    o_p = SSD_WIDTH + FNET_WIDTH
    mix = (_bdot(yn, wo_ref[0:o_f, :]) + _bdot(yfn_ref[0], wo_ref[o_f:o_p, :])
           + _bdot(ypl_ref[0], wo_ref[o_p:, :]))
    x1 = _ln(ALPHA * x_ref[0] + g1_ref[0] * mix) * lg_ref[...] + lb_ref[...]
    x1_ref[0] = x1
    h2_ref[0] = (_ln(x1) * (1.0 + sc2_ref[0]) + sh2_ref[0]).astype(BF16)


def _mix_call(yf, yb, xs, z, yfn, ypl, x, dsk, nw, wo, g1, lg, lb, sh2, sc2, *, tm):
    bsz, seq, d = x.shape
    mod_map = (lambda b, i: (b, 0, 0)) if g1.shape[0] > 1 else (lambda b, i: (0, 0, 0))

    def full(a):
        return pl.BlockSpec(a.shape, lambda b, i: (0,) * a.ndim)

    def tok(w):
        return pl.BlockSpec((1, tm, w), lambda b, i: (b, i, 0))

    mod = pl.BlockSpec((1, 1, d), mod_map)
    return pl.pallas_call(
        _mix_kernel,
        grid=(bsz, seq // tm),
        in_specs=[
            tok(SSD_WIDTH), tok(SSD_WIDTH), tok(SSD_WIDTH), tok(SSD_WIDTH), tok(FNET_WIDTH), tok(POOL_WIDTH), tok(d),
            full(dsk), full(nw), full(wo), mod, full(lg), full(lb), mod, mod,
        ],
        out_specs=[tok(d), tok(d)],
        out_shape=[jax.ShapeDtypeStruct((bsz, seq, d), F32), jax.ShapeDtypeStruct((bsz, seq, d), BF16)],
        compiler_params=_params("parallel", "parallel"),
        name="mix",
    )(yf, yb, xs, z, yfn, ypl, x, dsk, nw, wo, g1, lg, lb, sh2, sc2)


def _ffn_kernel(h_ref, hp_ref, hn_ref, wu_ref, cw_ref, cb_ref, wd_ref, x1_ref, g2_ref, lg_ref, lb_ref, o_ref,
                hbuf, a0, a1, hid, acc, *, tm, on_grid, period):
    i = pl.program_id(1)
    nt = pl.num_programs(1)
    nj = D_FF // FFN_CB
    ext = tm + 2 * FFN_HALO
    ra = ext // FFN_UP_CHUNKS
    rb = tm // FFN_ROW_CHUNKS
    slots = (a0, a1)
    base = FFN_PAD + FFN_HALO
    rows = (-1, 0, 1) if on_grid else (0,)

    def block_cols(j):
        return slice(j * FFN_CB, (j + 1) * FFN_CB), slice(D_FF + j * FFN_CB, D_FF + (j + 1) * FFN_CB)

    def paired(ref, j):
        return jnp.concatenate([ref[:, cols] for cols in block_cols(j)], axis=1)

    def up_project(a_ref, j):
        for r in range(FFN_UP_CHUNKS):
            hb = hbuf[r * ra:(r + 1) * ra, :]
            for half, cols in enumerate(block_cols(j)):
                a = jnp.dot(hb, wu_ref[:, cols], preferred_element_type=F32)
                a_ref[FFN_PAD + r * ra:FFN_PAD + (r + 1) * ra, half * FFN_CB:(half + 1) * FFN_CB] = a.astype(BF16)

    def conv(a_ref, j, r):
        cw = paired(cw_ref, j).astype(BF16)
        m = FFN_PAD
        n = rb + 2 * m
        start = base + r * rb - m
        col = (lax.broadcasted_iota(jnp.int32, (n, 1), 0) + (r * rb - m)) & (period - 1)
        taps = {dr: a_ref[pl.ds(start + dr * GRID_W, n), :] for dr in rows}

        def column(dc, lo, hi):
            acc_dc = None
            for dr in rows:
                k = (dr + 1) * 3 + dc + 1
                term = taps[dr][lo:hi] * cw[k:k + 1]
                acc_dc = term if acc_dc is None else acc_dc + term
            return acc_dc

        left = column(-1, 0, n)
        right = column(1, 0, n)
        zero = jnp.zeros_like(left)
        left = jnp.where(col == period - 1, zero, left)
        right = jnp.where(col == 0, zero, right)
        left2 = jnp.concatenate([left[n - 2:], left[:n - 2]], axis=0)
        u = (left2 + right)[m + 1:m + 1 + rb]
        return column(0, m, m + rb) + u + paired(cb_ref, j).astype(BF16)

    hbuf[0:FFN_HALO] = jnp.where(i > 0, hp_ref[0], jnp.zeros_like(hp_ref[0]))
    hbuf[FFN_HALO:FFN_HALO + tm] = h_ref[0]
    hbuf[FFN_HALO + tm:] = jnp.where(i < nt - 1, hn_ref[0], jnp.zeros_like(hn_ref[0]))
    for a in slots:
        a[0:FFN_PAD] = jnp.zeros((FFN_PAD, 2 * FFN_CB), BF16)
        a[FFN_PAD + ext:] = jnp.zeros((FFN_PAD, 2 * FFN_CB), BF16)

    up_project(slots[0], 0)
    for j in range(nj):
        if j + 1 < nj:
            up_project(slots[(j + 1) % 2], j + 1)
        pos = j % FFN_DOWN_BLOCKS
        for r in range(FFN_ROW_CHUNKS):
            c = conv(slots[j % 2], j, r)
            hidden = jax.nn.gelu(c[:, FFN_CB:], approximate=True) * c[:, :FFN_CB]
            hid[r * rb:(r + 1) * rb, pos * FFN_CB:(pos + 1) * FFN_CB] = hidden
        if pos == FFN_DOWN_BLOCKS - 1 or j == nj - 1:
            first = j - pos
            k = (pos + 1) * FFN_CB
            part = jnp.dot(hid[:, 0:k], wd_ref[first * FFN_CB:first * FFN_CB + k, :], preferred_element_type=F32)
            if first == 0:
                acc[...] = part
            else:
                acc[...] += part

    o_ref[0] = _ln(ALPHA * x1_ref[0] + g2_ref[0] * acc[...]) * lg_ref[...] + lb_ref[...]


def _ffn_call(h, w_up, conv_w9, conv_b, w_down, x1, g2, lg, lb, *, tm, on_grid, period):
    bsz, seq, d = h.shape
    nt = seq // tm
    th = tm // FFN_HALO
    nh = seq // FFN_HALO
    ext = tm + 2 * FFN_HALO
    mod_map = (lambda b, i: (b, 0, 0)) if g2.shape[0] > 1 else (lambda b, i: (0, 0, 0))

    def resident(a):
        return pl.BlockSpec(a.shape, lambda b, i: (0,) * a.ndim, pipeline_mode=pl.Buffered(1))

    a_scratch = pltpu.VMEM((ext + 2 * FFN_PAD, 2 * FFN_CB), BF16)
    return pl.pallas_call(
        functools.partial(_ffn_kernel, tm=tm, on_grid=on_grid, period=period),
        grid=(bsz, nt),
        in_specs=[
            pl.BlockSpec((1, tm, d), lambda b, i: (b, i, 0)),
            pl.BlockSpec((1, FFN_HALO, d), lambda b, i: (b, jnp.maximum(i * th - 1, 0), 0)),
            pl.BlockSpec((1, FFN_HALO, d), lambda b, i: (b, jnp.minimum((i + 1) * th, nh - 1), 0)),
            resident(w_up), resident(conv_w9), resident(conv_b), resident(w_down),
            pl.BlockSpec((1, tm, d), lambda b, i: (b, i, 0)),
            pl.BlockSpec((1, 1, d), mod_map),
            resident(lg), resident(lb),
        ],
        out_specs=pl.BlockSpec((1, tm, d), lambda b, i: (b, i, 0)),
        out_shape=jax.ShapeDtypeStruct((bsz, seq, d), F32),
        scratch_shapes=[pltpu.VMEM((ext, d), BF16), a_scratch, a_scratch,
                        pltpu.VMEM((tm, FFN_DOWN_BLOCKS * FFN_CB), BF16),
                        pltpu.VMEM((tm, d), F32)],
        compiler_params=_params("parallel", "arbitrary"),
        name="ffn",
    )(h, h, h, w_up, conv_w9, conv_b, w_down, x1, g2, lg, lb)


def _block_diag(blocks):
    g, n, m = blocks.shape
    out = jnp.zeros((g * n, g * m), blocks.dtype)
    for k in range(g):
        out = out.at[k * n:(k + 1) * n, k * m:(k + 1) * m].set(blocks[k])
    return out


def _dft_tables(seq):
    n1 = seq // DFT_N2
    k1 = np.arange(n1, dtype=np.int64)[:, None]
    nn1 = np.arange(n1, dtype=np.int64)[None, :]
    tabs = []
    for n2 in range(DFT_N2):
        th = 2.0 * np.pi * ((k1 * (DFT_N2 * nn1 + n2)) % seq) / seq
        tabs.append(np.concatenate([np.cos(th), -np.sin(th)], axis=0))
    mtab = np.stack(tabs)
    k2 = np.arange(DFT_N2, dtype=np.int64)[:, None]
    nn2 = np.arange(DFT_N2, dtype=np.int64)[None, :]
    th2 = 2.0 * np.pi * ((k2 * nn2) % DFT_N2) / DFT_N2
    scale = 1.0 / math.sqrt(seq * FNET_GDIM)
    dmat = scale * np.concatenate([np.cos(th2), np.sin(th2)], axis=1)
    return jnp.asarray(mtab, F32), jnp.asarray(dmat, F32)


def _dense_dft_table(seq):
    k = np.arange(seq, dtype=np.int64)
    th = 2.0 * np.pi * ((k[:, None] * k[None, :]) % seq) / seq
    scale = 1.0 / math.sqrt(seq * FNET_GDIM)
    return jnp.asarray(scale * np.concatenate([np.cos(th), -np.sin(th)], axis=1), F32)


def _channel_dft_tables():
    k = np.arange(FNET_GDIM, dtype=np.int64)
    th = 2.0 * np.pi * ((k[:, None] * k[None, :]) % FNET_GDIM) / FNET_GDIM
    eye = np.eye(FNET_GROUPS)
    return jnp.asarray(np.kron(eye, np.cos(th)), F32), jnp.asarray(np.kron(eye, np.sin(th)), F32)


def _pool_inverse_counts(seq):
    pos = np.arange(seq, dtype=np.int64)
    cols = []
    for win in POOL_WINDOWS:
        left = win // 2
        cnt = np.minimum(pos + (win - left), seq) - np.maximum(pos - left, 0)
        cols.append(np.repeat((1.0 / cnt)[:, None], POOL_GDIM, axis=1))
    return jnp.asarray(np.concatenate(cols, axis=1), F32)


def _scan_matrices():
    r = np.arange(SSD_CHUNK)
    lower = (r[:, None] >= r[None, :]).astype(np.float32)
    return jnp.asarray(np.stack([lower, lower.T]), F32)


def _head_expansion():
    ex = np.zeros((LANE, SSD_WIDTH), np.float32)
    for h in range(SSD_HEADS):
        ex[h, h * SSD_HEADDIM:(h + 1) * SSD_HEADDIM] = 1.0
    return jnp.asarray(ex, BF16)


def _token_mixer(x, sh1, sc1, lw, init, tri, ex, *, tm):
    bsz, seq, _ = x.shape
    z, xs, bc, dt, da, pq, ypl = _inproj_call(
        x, sh1, sc1, lw["w_cat"], lw["ssd_conv_w"], lw["ssd_conv_b"], lw["dt_bias"], lw["a_log"], lw["fmix"],
        lw["pool_bd"], lw["pool_scale"], _pool_inverse_counts(seq), tm=tm, pq_blocked=_two_stage_dft(seq))
    yf, yb, fin = _ssd_call(xs, bc, dt, da, tri, ex, init)
    return z, xs, yf, yb, pq, ypl, fin


def _two_stage_dft(seq):
    return seq % (DFT_N2 * 8) == 0


def _fourier_positions(pq, seq):
    bsz, w = pq.shape[0], FNET_WIDTH
    if _two_stage_dft(seq):
        n1 = seq // DFT_N2
        mtab, dmat = _dft_tables(seq)
        t_re, t_im = _dft1_call(pq, mtab)
        out = _dft2_call(dmat, [t_re, t_im], tn=min(DFT2_TN, n1 * w), split=w)
        return out.reshape(bsz, seq, w)
    return _dft2_call(_dense_dft_table(seq), [pq.reshape(bsz, 2 * seq, w)], tn=w)


def kernel(x, c, ctx, c_ctx, w_ada, b_ada, w_in, ssd_conv_w, ssd_conv_b, ssd_dt_bias, ssd_a_log, ssd_d,
           ssd_norm_w, fnet_w, pool_w, pool_scale, w_out, ln1_g, ln1_b, ffn_w_up, ffn_conv_w, ffn_conv_b,
           ffn_w_down, ln2_g, ln2_b):
    bsz, seq, d = x.shape
    n_ctx = ctx.shape[1]
    assert d == D_MODEL and bsz <= 7 and seq % (GRID_W * 8) == 0 and n_ctx % SSD_CHUNK == 0
    tm_lat = min(1024, seq)
    tm_ffn = min(1024, seq)

    cv = jnp.zeros((8, d), F32).at[0:bsz].set(c).at[bsz].set(c_ctx)
    mods = _ada_call(cv, w_ada, b_ada)
    tri = _scan_matrices()
    ex = _head_expansion()
    fmix = _fmix_call(*_channel_dft_tables(), jnp.stack([_block_diag(fnet_w[l]) for l in range(DEPTH)]))
    zero_state = jnp.zeros((bsz,) + SSD_STATE_SHAPE, F32)
    lane_pad = jnp.zeros((d, LANE - SSD_HEADS), F32)

    for l in range(DEPTH):
        last = l == DEPTH - 1
        m = mods[l].reshape(8, 6, d)
        lat = [m[0:bsz, k][:, None, :] for k in range(6)]
        cx = [m[bsz:bsz + 1, k][:, None, :] for k in range(6)]
        w = w_in[l]
        w_cat = jnp.concatenate(
            [w[:, 0:OFF_DT], w[:, OFF_FNET:N_IN], w[:, OFF_DT:OFF_DT + SSD_HEADS], lane_pad,
             w[:, OFF_DT + SSD_HEADS:OFF_FNET], lane_pad], axis=1).astype(BF16)
        pad16 = ((0, 0), (0, LANE - SSD_HEADS))
        lw = {
            "w_cat": w_cat,
            "ssd_conv_w": ssd_conv_w[l],
            "ssd_conv_b": ssd_conv_b[l][None, :],
            "dt_bias": jnp.pad(ssd_dt_bias[l], pad16),
            "a_log": jnp.pad(ssd_a_log[l], pad16),
            "fmix": fmix[l],
            "pool_bd": _block_diag(pool_w[l]).astype(BF16),
            "pool_scale": pool_scale[l][None, :],
        }
        dsk = jnp.repeat(ssd_d[l], SSD_HEADDIM)[None, :]
        nw = ssd_norm_w[l][None, :]
        wo = w_out[l].astype(BF16)
        lg1, lb1 = ln1_g[l][None, :], ln1_b[l][None, :]
        lg2, lb2 = ln2_g[l][None, :], ln2_b[l][None, :]
        w_up = ffn_w_up[l].astype(BF16)
        w_down = ffn_w_down[l].astype(BF16)
        conv9 = ffn_conv_w[l].reshape(9, 2 * D_FF)
        conv_b = ffn_conv_b[l][None, :]

        cz, cxs, cyf, cyb, cpq, cypl, c_fin = _token_mixer(ctx, cx[0], cx[1], lw, zero_state, tri, ex, tm=n_ctx)
        z, xs, yf, yb, pq, ypl, _ = _token_mixer(x, lat[0], lat[1], lw, c_fin, tri, ex, tm=tm_lat)

        yfn = _fourier_positions(pq, seq)
        x1, h2 = _mix_call(yf, yb, xs, z, yfn, ypl, x, dsk, nw, wo, lat[2], lg1, lb1, lat[3], lat[4], tm=tm_lat)
        x = _ffn_call(h2, w_up, conv9, conv_b, w_down, x1, lat[5], lg2, lb2, tm=tm_ffn, on_grid=True, period=GRID_W)

        if not last:
            cyfn = _fourier_positions(cpq, n_ctx)
            c1, ch2 = _mix_call(cyf, cyb, cxs, cz, cyfn, cypl, ctx, dsk, nw, wo, cx[2], lg1, lb1, cx[3], cx[4],
                                tm=n_ctx)
            ctx = _ffn_call(ch2.reshape(1, bsz * n_ctx, d), w_up, conv9, conv_b, w_down, c1.reshape(1, bsz * n_ctx, d),
                            cx[5], lg2, lb2, tm=bsz * n_ctx, on_grid=False, period=n_ctx).reshape(bsz, n_ctx, d)
    return x
```

```python
import functools
import math

import numpy as np
import jax
import jax.numpy as jnp
from jax import lax
from jax.experimental import pallas as pl
from jax.experimental.pallas import tpu as pltpu

F32 = jnp.float32
BF16 = jnp.bfloat16
HIGHEST = lax.Precision.HIGHEST

D_MODEL = 1024
DEPTH = 2
GRID_W = 64
SSD_WIDTH = 512
SSD_HEADDIM = 64
SSD_HEADS = 8
SSD_GROUPS = 2
HEADS_PER_GROUP = SSD_HEADS // SSD_GROUPS
SSD_STATE = 128
SSD_CHUNK = 128
SSD_CHUNKS_PER_STEP = 8
XBC_WIDTH = SSD_WIDTH + 2 * SSD_GROUPS * SSD_STATE
BC_WIDTH = 2 * SSD_GROUPS * SSD_STATE
SSD_STATE_SHAPE = (2, SSD_GROUPS, SSD_STATE, HEADS_PER_GROUP * SSD_HEADDIM)
FNET_WIDTH = 256
FNET_GROUPS = 4
FNET_GDIM = 64
POOL_WINDOWS = (2, 4, 8, 16)
POOL_WIDTH = 256
POOL_GDIM = 64
POOL_HALO = 8
D_FF = 2816
ALPHA = (2.0 * DEPTH) ** 0.25
EPS = 1e-6

OFF_XBC = SSD_WIDTH
OFF_DT = OFF_XBC + XBC_WIDTH
OFF_FNET = OFF_DT + 2 * SSD_HEADS
OFF_POOL = OFF_FNET + FNET_WIDTH
N_IN = OFF_POOL + POOL_WIDTH

LANE = 128
C_Z = 0
C_XBC = C_Z + SSD_WIDTH
C_FNET = C_XBC + XBC_WIDTH
C_POOL = C_FNET + FNET_WIDTH
C_DT = C_POOL + POOL_WIDTH
N_CAT = C_DT + 2 * LANE

DFT_N2 = 64
DFT_NB = 32
DFT2_TN = 8192
ADA_TN = 1536
FFN_CB = 256
FFN_HALO = GRID_W
FFN_PAD = 16
FFN_ROW_CHUNKS = 1
FFN_UP_CHUNKS = 1
FFN_DOWN_BLOCKS = 4
VMEM_LIMIT = 56 * 1024 * 1024


def _ln(x):
    mu = jnp.mean(x, axis=-1, keepdims=True)
    xc = x - mu
    var = jnp.mean(xc * xc, axis=-1, keepdims=True)
    return xc * lax.rsqrt(var + EPS)


def _bdot(a, b):
    return jnp.dot(a.astype(BF16), b.astype(BF16), preferred_element_type=F32)


def _params(*sem):
    return pltpu.CompilerParams(dimension_semantics=sem, vmem_limit_bytes=VMEM_LIMIT)


def _ada_kernel(c_ref, w_ref, b_ref, o_ref):
    cv = c_ref[...]
    s = cv * jax.nn.sigmoid(cv)
    o_ref[0] = jnp.dot(s, w_ref[0], preferred_element_type=F32, precision=HIGHEST) + b_ref[0]


def _ada_call(cv, w_ada, b_ada):
    depth, d, n = w_ada.shape
    tn = ADA_TN
    return pl.pallas_call(
        _ada_kernel,
        grid=(depth, n // tn),
        in_specs=[
            pl.BlockSpec((8, d), lambda l, j: (0, 0)),
            pl.BlockSpec((1, d, tn), lambda l, j: (l, 0, j)),
            pl.BlockSpec((1, 1, tn), lambda l, j: (l, 0, j)),
        ],
        out_specs=pl.BlockSpec((1, 8, tn), lambda l, j: (l, 0, j)),
        out_shape=jax.ShapeDtypeStruct((depth, 8, n), F32),
        compiler_params=_params("parallel", "parallel"),
        name="ada",
    )(cv, w_ada, b_ada.reshape(depth, 1, n))


def _fmix_kernel(cc_ref, cs_ref, fw_ref, o_ref):
    fw = fw_ref[0]
    o_ref[0, 0] = jnp.dot(cc_ref[...], fw, preferred_element_type=F32, precision=HIGHEST).astype(BF16)
    o_ref[0, 1] = jnp.dot(cs_ref[...], fw, preferred_element_type=F32, precision=HIGHEST).astype(BF16)


def _fmix_call(cc, cs, fw_bd):
    depth, w, _ = fw_bd.shape
    return pl.pallas_call(
        _fmix_kernel,
        grid=(depth,),
        in_specs=[
            pl.BlockSpec((w, w), lambda l: (0, 0)),
            pl.BlockSpec((w, w), lambda l: (0, 0)),
            pl.BlockSpec((1, w, w), lambda l: (l, 0, 0)),
        ],
        out_specs=pl.BlockSpec((1, 2, w, w), lambda l: (l, 0, 0, 0)),
        out_shape=jax.ShapeDtypeStruct((depth, 2, w, w), BF16),
        compiler_params=_params("parallel"),
        name="fmix",
    )(cc, cs, fw_bd)


def _inproj_kernel(x_ref, xp_ref, xn_ref, sh_ref, sc_ref, w_ref, cw_ref, cb_ref, dtb_ref, alog_ref,
                   fm_ref, pw_ref, ps_ref, inv_ref,
                   z_ref, xs_ref, bc_ref, dt_ref, da_ref, pq_ref, yp_ref, *, tm, pq_blocked):
    i = pl.program_id(1)
    nt = pl.num_programs(1)
    sh = sh_ref[0]
    sc = sc_ref[0]

    def modulated(xv):
        return (_ln(xv) * (1.0 + sc) + sh).astype(BF16)

    x_ext = jnp.concatenate([xp_ref[0], x_ref[0], xn_ref[0]], axis=0)
    u_ext = jnp.dot(modulated(x_ext), w_ref[...], preferred_element_type=F32)
    u = u_ext[POOL_HALO:POOL_HALO + tm]
    up = jnp.where(i > 0, u_ext[0:POOL_HALO, C_XBC:C_DT], 0.0)
    un = jnp.where(i < nt - 1, u_ext[POOL_HALO + tm:, C_XBC:C_DT], 0.0)

    z_ref[0] = u[:, C_Z:C_XBC].astype(BF16)

    xbc = u[:, C_XBC:C_FNET]
    row = lax.broadcasted_iota(jnp.int32, (tm, 1), 0)
    prev_row = up[POOL_HALO - 1:POOL_HALO, 0:XBC_WIDTH]
    next_row = un[0:1, 0:XBC_WIDTH]
    xm1 = jnp.where(row == 0, prev_row, pltpu.roll(xbc, 1, 0))
    xp1 = jnp.where(row == tm - 1, next_row, pltpu.roll(xbc, tm - 1, 0))
    cw = cw_ref[...]
    conv = xm1 * cw[0:1] + xbc * cw[1:2] + xp1 * cw[2:3] + cb_ref[...]
    act = conv * jax.nn.sigmoid(conv)
    xs_ref[0] = act[:, 0:SSD_WIDTH].astype(BF16)
    bc_ref[0] = act[:, SSD_WIDTH:XBC_WIDTH].astype(BF16)

    lane = lax.broadcasted_iota(jnp.int32, (1, LANE), 1)
    for d in range(2):
        raw = u[:, C_DT + d * LANE:C_DT + (d + 1) * LANE] + dtb_ref[d:d + 1, :]
        sp = jnp.maximum(raw, 0.0) + jnp.log(1.0 + jnp.exp(-jnp.abs(raw)))
        dt = jnp.where(lane < SSD_HEADS, sp, 0.0)
        dt_ref[0, d] = dt
        da_ref[0, d] = dt * (-jnp.exp(alog_ref[d:d + 1, :]))

    uf = u[:, C_FNET:C_POOL].astype(BF16)
    for part in range(2):
        pq = jnp.dot(uf, fm_ref[part], preferred_element_type=F32)
        if pq_blocked:
            pq = pq.reshape(tm // DFT_N2, DFT_N2, FNET_WIDTH).reshape(tm // DFT_N2, DFT_N2 * FNET_WIDTH)
        pq_ref[0, part] = pq

    off_pool = C_POOL - C_XBC
    ext = jnp.concatenate([up[:, off_pool:off_pool + POOL_WIDTH], u[:, C_POOL:C_DT],
                           un[:, off_pool:off_pool + POOL_WIDTH]], axis=0)
    n_ext = tm + 2 * POOL_HALO
    sums = {}
    cur = ext
    width = 1
    while width < POOL_WINDOWS[-1]:
        cur = cur + pltpu.roll(cur, n_ext - width, 0)
        width *= 2
        sums[width] = cur
    lane_p = lax.broadcasted_iota(jnp.int32, (1, POOL_WIDTH), 1)
    wsum = None
    for gi, win in enumerate(POOL_WINDOWS):
        start = POOL_HALO - win // 2
        s_w = sums[win]
        if start:
            s_w = pltpu.roll(s_w, n_ext - start, 0)
        s_w = s_w[0:tm]
        wsum = s_w if wsum is None else jnp.where(lane_p >= gi * POOL_GDIM, s_w, wsum)
    pooled = wsum * inv_ref[...] - u[:, C_POOL:C_DT]
    yp_ref[0] = (_bdot(pooled, pw_ref[...]) * ps_ref[...]).astype(BF16)


def _inproj_call(x, sh, sc, w_cat, conv_w, conv_b, dt_bias, a_log, fmix, pw, ps, inv, *, tm, pq_blocked):
    bsz, seq, d = x.shape
    nt = seq // tm
    nb8 = seq // 8
    tb = tm // 8
    mod_map = (lambda b, i: (b, 0, 0)) if sh.shape[0] > 1 else (lambda b, i: (0, 0, 0))

    def full(a):
        return pl.BlockSpec(a.shape, lambda b, i: (0,) * a.ndim)

    in_specs = [
        pl.BlockSpec((1, tm, d), lambda b, i: (b, i, 0)),
        pl.BlockSpec((1, 8, d), lambda b, i: (b, jnp.maximum(i * tb - 1, 0), 0)),
        pl.BlockSpec((1, 8, d), lambda b, i: (b, jnp.minimum((i + 1) * tb, nb8 - 1), 0)),
        pl.BlockSpec((1, 1, d), mod_map),
        pl.BlockSpec((1, 1, d), mod_map),
    ] + [full(a) for a in (w_cat, conv_w, conv_b, dt_bias, a_log, fmix, pw, ps)] + [
        pl.BlockSpec((tm, POOL_WIDTH), lambda b, i: (i, 0))]
    if pq_blocked:
        pq_shape = jax.ShapeDtypeStruct((bsz, 2, seq // DFT_N2, DFT_N2 * FNET_WIDTH), F32)
        pq_spec = pl.BlockSpec((1, 2, tm // DFT_N2, DFT_N2 * FNET_WIDTH), lambda b, i: (b, 0, i, 0))
    else:
        pq_shape = jax.ShapeDtypeStruct((bsz, 2, seq, FNET_WIDTH), F32)
        pq_spec = pl.BlockSpec((1, 2, tm, FNET_WIDTH), lambda b, i: (b, 0, i, 0))
    out_shape = [
        jax.ShapeDtypeStruct((bsz, seq, SSD_WIDTH), BF16),
        jax.ShapeDtypeStruct((bsz, seq, SSD_WIDTH), BF16),
        jax.ShapeDtypeStruct((bsz, seq, BC_WIDTH), BF16),
        jax.ShapeDtypeStruct((bsz, 2, seq, LANE), F32),
        jax.ShapeDtypeStruct((bsz, 2, seq, LANE), F32),
        pq_shape,
        jax.ShapeDtypeStruct((bsz, seq, POOL_WIDTH), BF16),
    ]
    out_specs = [
        pl.BlockSpec((1, tm, SSD_WIDTH), lambda b, i: (b, i, 0)),
        pl.BlockSpec((1, tm, SSD_WIDTH), lambda b, i: (b, i, 0)),
        pl.BlockSpec((1, tm, BC_WIDTH), lambda b, i: (b, i, 0)),
        pl.BlockSpec((1, 2, tm, LANE), lambda b, i: (b, 0, i, 0)),
        pl.BlockSpec((1, 2, tm, LANE), lambda b, i: (b, 0, i, 0)),
        pq_spec,
        pl.BlockSpec((1, tm, POOL_WIDTH), lambda b, i: (b, i, 0)),
    ]
    return pl.pallas_call(
        functools.partial(_inproj_kernel, tm=tm, pq_blocked=pq_blocked),
        grid=(bsz, nt),
        in_specs=in_specs,
        out_specs=out_specs,
        out_shape=out_shape,
        compiler_params=_params("parallel", "parallel"),
        name="inproj",
    )(x, x, x, sh, sc, w_cat, conv_w, conv_b, dt_bias, a_log, fmix, pw, ps, inv)


def _bf16_pieces(x, terms):
    pieces = []
    rem = x
    for t in range(terms):
        piece = rem.astype(BF16)
        pieces.append(piece)
        if t + 1 < terms:
            rem = rem - piece.astype(F32)
    return pieces


def _ssd_kernel(xsf_ref, bcf_ref, dtf_ref, daf_ref, xsb_ref, bcb_ref, dtb_ref, dab_ref, tri_ref, ex_ref, init_ref,
                yf_ref, yb_ref, fin_ref, st_ref, *, kc):
    s = pl.program_id(1)
    ns = pl.num_programs(1)

    @pl.when(s == 0)
    def _():
        st_ref[...] = init_ref[0]

    fwd = (xsf_ref, bcf_ref, dtf_ref, daf_ref, yf_ref)
    bwd = (xsb_ref, bcb_ref, dtb_ref, dab_ref, yb_ref)
    jobs = [(0, k, fwd) for k in range(kc)] + [(1, kc - 1 - k, bwd) for k in range(kc)]
    c = SSD_CHUNK
    gn = SSD_GROUPS * SSD_STATE
    pw = 2 * SSD_HEADDIM
    gw = HEADS_PER_GROUP * SSD_HEADDIM
    low_half = lax.broadcasted_iota(jnp.int32, (1, pw), 1) < SSD_HEADDIM
    zero = jnp.zeros((SSD_CHUNK, pw), BF16)
    tri_b = [tri_ref[d].astype(BF16) for d in range(2)]
    mask = [tri_ref[d] > 0.5 for d in range(2)]

    def rows(k):
        return slice(k * c, (k + 1) * c)

    cs3 = [jnp.dot(tri_b[d], jnp.concatenate(_bf16_pieces(r[3][0, 0, rows(k), :], 3), axis=1),
                   preferred_element_type=F32) for d, k, r in jobs]
    cs = [v[:, 0:LANE] + v[:, LANE:2 * LANE] + v[:, 2 * LANE:3 * LANE] for v in cs3]
    cs_t = [v.T for v in cs]
    ex4 = [jnp.dot(jnp.concatenate(_bf16_pieces(cs[n], 2) + _bf16_pieces(r[2][0, 0, rows(k), :], 2), axis=0),
                   ex_ref[...], preferred_element_type=F32) for n, (d, k, r) in enumerate(jobs)]
    cs_x = [v[0:c] + v[c:2 * c] for v in ex4]
    dt_x = [v[2 * c:3 * c] + v[3 * c:4 * c] for v in ex4]
    tot_x = [cs_x[n][(c - 1 if d == 0 else 0):(c if d == 0 else 1), :] for n, (d, k, r) in enumerate(jobs)]
    e_cs = [jnp.exp(v) for v in cs_x]
    e_tot = [jnp.exp(v) for v in tot_x]
    xd32 = [r[0][0, rows(k), :].astype(F32) * dt_x[n] for n, (d, k, r) in enumerate(jobs)]
    xd_dec = [(xd32[n] * jnp.exp(tot_x[n] - cs_x[n])).astype(BF16) for n in range(len(jobs))]
    xd = [v.astype(BF16) for v in xd32]
    bc = [r[1][0, rows(k), :] for d, k, r in jobs]
    bg_t = [[v[:, g * SSD_STATE:(g + 1) * SSD_STATE].astype(F32).T.astype(BF16) for g in range(SSD_GROUPS)]
            for v in bc]
    cg = [[v[:, gn + g * SSD_STATE:gn + (g + 1) * SSD_STATE] for g in range(SSD_GROUPS)] for v in bc]
    cb = [[jnp.dot(cg[n][g], bg_t[n][g], preferred_element_type=F32) for g in range(SSD_GROUPS)]
          for n in range(len(jobs))]

    def y_diagonal(n, d, g):
        parts = []
        for pp in range(HEADS_PER_GROUP // 2):
            p = g * (HEADS_PER_GROUP // 2) + pp
            decay = []
            for h in (2 * p, 2 * p + 1):
                seg = cs[n][:, h:h + 1] - cs_t[n][h:h + 1, :]
                decay.append(cb[n][g] * jnp.where(mask[d], jnp.exp(jnp.minimum(seg, 0.0)), 0.0))
            m = jnp.concatenate(decay, axis=1).astype(BF16)
            xp = xd[n][:, p * pw:(p + 1) * pw]
            x_bd = jnp.concatenate([jnp.where(low_half, xp, zero), jnp.where(low_half, zero, xp)], axis=0)
            parts.append(jnp.dot(m, x_bd, preferred_element_type=F32))
        return jnp.concatenate(parts, axis=1)

    y_diag = [[y_diagonal(n, d, g) for g in range(SSD_GROUPS)] for n, (d, k, r) in enumerate(jobs)]

    for n, (d, k, r) in enumerate(jobs):
        for g in range(SSD_GROUPS):
            glanes = slice(g * gw, (g + 1) * gw)
            st = st_ref[d, g]
            y_off = jnp.dot(cg[n][g], st.astype(BF16), preferred_element_type=F32) * e_cs[n][:, glanes]
            r[4][0, rows(k), glanes] = (y_diag[n][g] + y_off).astype(BF16)
            st_ref[d, g] = st * e_tot[n][:, glanes] + jnp.dot(bg_t[n][g], xd_dec[n][:, glanes],
                                                              preferred_element_type=F32)

    @pl.when(s == ns - 1)
    def _():
        fin_ref[0] = st_ref[...]


def _ssd_call(xs, bc, dt, da, tri, ex, init):
    bsz, seq, _ = xs.shape
    kc = min(SSD_CHUNKS_PER_STEP, seq // SSD_CHUNK)
    rows = kc * SSD_CHUNK
    ns = seq // rows
    st_block = (1,) + SSD_STATE_SHAPE

    def tok(w, rev):
        return pl.BlockSpec((1, rows, w), lambda b, s: (b, ns - 1 - s if rev else s, 0))

    def per_dir(w, d):
        return pl.BlockSpec((1, 1, rows, w), lambda b, s: (b, d, ns - 1 - s if d else s, 0))

    return pl.pallas_call(
        functools.partial(_ssd_kernel, kc=kc),
        grid=(bsz, ns),
        in_specs=[
            tok(SSD_WIDTH, False), tok(BC_WIDTH, False), per_dir(LANE, 0), per_dir(LANE, 0),
            tok(SSD_WIDTH, True), tok(BC_WIDTH, True), per_dir(LANE, 1), per_dir(LANE, 1),
            pl.BlockSpec(tri.shape, lambda b, s: (0, 0, 0)),
            pl.BlockSpec(ex.shape, lambda b, s: (0, 0)),
            pl.BlockSpec(st_block, lambda b, s: (b, 0, 0, 0, 0)),
        ],
        out_specs=[tok(SSD_WIDTH, False), tok(SSD_WIDTH, True), pl.BlockSpec(st_block, lambda b, s: (b, 0, 0, 0, 0))],
        out_shape=[
            jax.ShapeDtypeStruct((bsz, seq, SSD_WIDTH), BF16),
            jax.ShapeDtypeStruct((bsz, seq, SSD_WIDTH), BF16),
            jax.ShapeDtypeStruct((bsz,) + SSD_STATE_SHAPE, F32),
        ],
        scratch_shapes=[pltpu.VMEM(SSD_STATE_SHAPE, F32)],
        compiler_params=_params("parallel", "arbitrary"),
        name="ssd",
    )(xs, bc, dt, da, xs, bc, dt, da, tri, ex, init)


def _dft1_kernel(p_ref, q_ref, m_ref, tr_ref, ti_ref, *, n1):
    re, im = [], []
    for j in range(DFT_NB):
        m = m_ref[j]
        rp = _bdot(m, p_ref[0, 0, :, j * FNET_WIDTH:(j + 1) * FNET_WIDTH])
        rq = _bdot(m, q_ref[0, 0, :, j * FNET_WIDTH:(j + 1) * FNET_WIDTH])
        re.append(rp[:n1] + rq[n1:])
        im.append(rp[n1:] - rq[:n1])
    tr_ref[0] = jnp.stack(re).reshape(DFT_NB, n1 * FNET_WIDTH).astype(BF16)
    ti_ref[0] = jnp.stack(im).reshape(DFT_NB, n1 * FNET_WIDTH).astype(BF16)


def _dft1_call(pqv, mtab):
    bsz, _, n1, cols_all = pqv.shape
    w = cols_all // DFT_N2
    cols = DFT_NB * w
    t_shape = jax.ShapeDtypeStruct((bsz, DFT_N2, n1 * w), BF16)
    t_spec = pl.BlockSpec((1, DFT_NB, n1 * w), lambda b, j: (b, j, 0))
    return pl.pallas_call(
        functools.partial(_dft1_kernel, n1=n1),
        grid=(bsz, DFT_N2 // DFT_NB),
        in_specs=[
            pl.BlockSpec((1, 1, n1, cols), lambda b, j: (b, 0, 0, j)),
            pl.BlockSpec((1, 1, n1, cols), lambda b, j: (b, 1, 0, j)),
            pl.BlockSpec((DFT_NB, 2 * n1, n1), lambda b, j: (j, 0, 0)),
        ],
        out_specs=[t_spec, t_spec],
        out_shape=[t_shape, t_shape],
        compiler_params=_params("parallel", "parallel"),
        name="dft1",
    )(pqv, pqv, mtab)


def _dft2_kernel(d_ref, *refs, split):
    *t_refs, o_ref = refs
    t = jnp.concatenate([r[0] for r in t_refs], axis=0) if len(t_refs) > 1 else t_refs[0][0]
    res = _bdot(d_ref[...], t)
    if split:
        res = res.reshape(res.shape[0], res.shape[1] // split, split)
    o_ref[0] = res.astype(BF16)


def _dft2_call(dmat, parts, *, tn, split=0):
    bsz, _, n = parts[0].shape
    mo = dmat.shape[0]
    if split:
        out_spec = pl.BlockSpec((1, mo, tn // split, split), lambda b, j: (b, 0, j, 0))
        out_shape = jax.ShapeDtypeStruct((bsz, mo, n // split, split), BF16)
    else:
        out_spec = pl.BlockSpec((1, mo, tn), lambda b, j: (b, 0, j))
        out_shape = jax.ShapeDtypeStruct((bsz, mo, n), BF16)
    return pl.pallas_call(
        functools.partial(_dft2_kernel, split=split),
        grid=(bsz, n // tn),
        in_specs=[pl.BlockSpec((mo, dmat.shape[1]), lambda b, j: (0, 0))] + [
            pl.BlockSpec((1, p.shape[1], tn), lambda b, j: (b, 0, j)) for p in parts],
        out_specs=out_spec,
        out_shape=out_shape,
        compiler_params=_params("parallel", "parallel"),
        name="dft2",
    )(dmat, *parts)


def _mix_kernel(yf_ref, yb_ref, xs_ref, z_ref, yfn_ref, ypl_ref, x_ref, dsk_ref, nw_ref, wo_ref, g1_ref,
                lg_ref, lb_ref, sh2_ref, sc2_ref, x1_ref, h2_ref):
    y = yf_ref[0].astype(F32) + yb_ref[0].astype(F32) + xs_ref[0].astype(F32) * dsk_ref[...]
    z = z_ref[0].astype(F32)
    y = y * (z * jax.nn.sigmoid(z))
    gw = SSD_WIDTH // SSD_GROUPS
    parts = []
    for g in range(SSD_GROUPS):
        yg = y[:, g * gw:(g + 1) * gw]
        ms = jnp.mean(yg * yg, axis=-1, keepdims=True)
        parts.append(yg * lax.rsqrt(ms + EPS))
    yn = jnp.concatenate(parts, axis=-1) * nw_ref[...]
    o_f = SSD_WIDTH
    o_p = SSD_WIDTH + FNET_WIDTH
    mix = (_bdot(yn, wo_ref[0:o_f, :]) + _bdot(yfn_ref[0], wo_ref[o_f:o_p, :])
           + _bdot(ypl_ref[0], wo_ref[o_p:, :]))
    x1 = _ln(ALPHA * x_ref[0] + g1_ref[0] * mix) * lg_ref[...] + lb_ref[...]
    x1_ref[0] = x1
    h2_ref[0] = (_ln(x1) * (1.0 + sc2_ref[0]) + sh2_ref[0]).astype(BF16)


def _mix_call(yf, yb, xs, z, yfn, ypl, x, dsk, nw, wo, g1, lg, lb, sh2, sc2, *, tm):
    bsz, seq, d = x.shape
    mod_map = (lambda b, i: (b, 0, 0)) if g1.shape[0] > 1 else (lambda b, i: (0, 0, 0))

    def full(a):
        return pl.BlockSpec(a.shape, lambda b, i: (0,) * a.ndim)

    def tok(w):
        return pl.BlockSpec((1, tm, w), lambda b, i: (b, i, 0))

    mod = pl.BlockSpec((1, 1, d), mod_map)
    return pl.pallas_call(
        _mix_kernel,
        grid=(bsz, seq // tm),
        in_specs=[
            tok(SSD_WIDTH), tok(SSD_WIDTH), tok(SSD_WIDTH), tok(SSD_WIDTH), tok(FNET_WIDTH), tok(POOL_WIDTH), tok(d),
            full(dsk), full(nw), full(wo), mod, full(lg), full(lb), mod, mod,
        ],
        out_specs=[tok(d), tok(d)],
        out_shape=[jax.ShapeDtypeStruct((bsz, seq, d), F32), jax.ShapeDtypeStruct((bsz, seq, d), BF16)],
        compiler_params=_params("parallel", "parallel"),
        name="mix",
    )(yf, yb, xs, z, yfn, ypl, x, dsk, nw, wo, g1, lg, lb, sh2, sc2)


def _ffn_kernel(h_ref, hp_ref, hn_ref, wu_ref, cw_ref, cb_ref, wd_ref, x1_ref, g2_ref, lg_ref, lb_ref, o_ref,
                hbuf, a0, a1, hid, acc, *, tm, on_grid, period):
    i = pl.program_id(1)
    nt = pl.num_programs(1)
    nj = D_FF // FFN_CB
    ext = tm + 2 * FFN_HALO
    ra = ext // FFN_UP_CHUNKS
    rb = tm // FFN_ROW_CHUNKS
    slots = (a0, a1)
    base = FFN_PAD + FFN_HALO
    rows = (-1, 0, 1) if on_grid else (0,)

    def block_cols(j):
        return slice(j * FFN_CB, (j + 1) * FFN_CB), slice(D_FF + j * FFN_CB, D_FF + (j + 1) * FFN_CB)

    def paired(ref, j):
        return jnp.concatenate([ref[:, cols] for cols in block_cols(j)], axis=1)

    def up_project(a_ref, j):
        for r in range(FFN_UP_CHUNKS):
            hb = hbuf[r * ra:(r + 1) * ra, :]
            for half, cols in enumerate(block_cols(j)):
                a = jnp.dot(hb, wu_ref[:, cols], preferred_element_type=F32)
                a_ref[FFN_PAD + r * ra:FFN_PAD + (r + 1) * ra, half * FFN_CB:(half + 1) * FFN_CB] = a.astype(BF16)

    def conv(a_ref, j, r):
        cw = paired(cw_ref, j).astype(BF16)
        m = FFN_PAD
        n = rb + 2 * m
        start = base + r * rb - m
        col = (lax.broadcasted_iota(jnp.int32, (n, 1), 0) + (r * rb - m)) & (period - 1)
        taps = {dr: a_ref[pl.ds(start + dr * GRID_W, n), :] for dr in rows}

        def column(dc, lo, hi):
            acc_dc = None
            for dr in rows:
                k = (dr + 1) * 3 + dc + 1
                term = taps[dr][lo:hi] * cw[k:k + 1]
                acc_dc = term if acc_dc is None else acc_dc + term
            return acc_dc

        left = column(-1, 0, n)
        right = column(1, 0, n)
        zero = jnp.zeros_like(left)
        left = jnp.where(col == period - 1, zero, left)
        right = jnp.where(col == 0, zero, right)
        left2 = jnp.concatenate([left[n - 2:], left[:n - 2]], axis=0)
        u = (left2 + right)[m + 1:m + 1 + rb]
        return column(0, m, m + rb) + u + paired(cb_ref, j).astype(BF16)

    hbuf[0:FFN_HALO] = jnp.where(i > 0, hp_ref[0], jnp.zeros_like(hp_ref[0]))
    hbuf[FFN_HALO:FFN_HALO + tm] = h_ref[0]
    hbuf[FFN_HALO + tm:] = jnp.where(i < nt - 1, hn_ref[0], jnp.zeros_like(hn_ref[0]))
    for a in slots:
        a[0:FFN_PAD] = jnp.zeros((FFN_PAD, 2 * FFN_CB), BF16)
        a[FFN_PAD + ext:] = jnp.zeros((FFN_PAD, 2 * FFN_CB), BF16)

    up_project(slots[0], 0)
    for j in range(nj):
        if j + 1 < nj:
            up_project(slots[(j + 1) % 2], j + 1)
        pos = j % FFN_DOWN_BLOCKS
        for r in range(FFN_ROW_CHUNKS):
            c = conv(slots[j % 2], j, r)
            hidden = jax.nn.gelu(c[:, FFN_CB:], approximate=True) * c[:, :FFN_CB]
            hid[r * rb:(r + 1) * rb, pos * FFN_CB:(pos + 1) * FFN_CB] = hidden
        if pos == FFN_DOWN_BLOCKS - 1 or j == nj - 1:
            first = j - pos
            k = (pos + 1) * FFN_CB
            part = jnp.dot(hid[:, 0:k], wd_ref[first * FFN_CB:first * FFN_CB + k, :], preferred_element_type=F32)
            if first == 0:
                acc[...] = part
            else:
                acc[...] += part

    o_ref[0] = _ln(ALPHA * x1_ref[0] + g2_ref[0] * acc[...]) * lg_ref[...] + lb_ref[...]


def _ffn_call(h, w_up, conv_w9, conv_b, w_down, x1, g2, lg, lb, *, tm, on_grid, period):
    bsz, seq, d = h.shape
    nt = seq // tm
    th = tm // FFN_HALO
    nh = seq // FFN_HALO
    ext = tm + 2 * FFN_HALO
    mod_map = (lambda b, i: (b, 0, 0)) if g2.shape[0] > 1 else (lambda b, i: (0, 0, 0))

    def resident(a):
        return pl.BlockSpec(a.shape, lambda b, i: (0,) * a.ndim, pipeline_mode=pl.Buffered(1))

    a_scratch = pltpu.VMEM((ext + 2 * FFN_PAD, 2 * FFN_CB), BF16)
    return pl.pallas_call(
        functools.partial(_ffn_kernel, tm=tm, on_grid=on_grid, period=period),
        grid=(bsz, nt),
        in_specs=[
            pl.BlockSpec((1, tm, d), lambda b, i: (b, i, 0)),
            pl.BlockSpec((1, FFN_HALO, d), lambda b, i: (b, jnp.maximum(i * th - 1, 0), 0)),
            pl.BlockSpec((1, FFN_HALO, d), lambda b, i: (b, jnp.minimum((i + 1) * th, nh - 1), 0)),
            resident(w_up), resident(conv_w9), resident(conv_b), resident(w_down),
            pl.BlockSpec((1, tm, d), lambda b, i: (b, i, 0)),
            pl.BlockSpec((1, 1, d), mod_map),
            resident(lg), resident(lb),
        ],
        out_specs=pl.BlockSpec((1, tm, d), lambda b, i: (b, i, 0)),
        out_shape=jax.ShapeDtypeStruct((bsz, seq, d), F32),
        scratch_shapes=[pltpu.VMEM((ext, d), BF16), a_scratch, a_scratch,
                        pltpu.VMEM((tm, FFN_DOWN_BLOCKS * FFN_CB), BF16),
                        pltpu.VMEM((tm, d), F32)],
        compiler_params=_params("parallel", "arbitrary"),
        name="ffn",
    )(h, h, h, w_up, conv_w9, conv_b, w_down, x1, g2, lg, lb)


def _block_diag(blocks):
    g, n, m = blocks.shape
    out = jnp.zeros((g * n, g * m), blocks.dtype)
    for k in range(g):
        out = out.at[k * n:(k + 1) * n, k * m:(k + 1) * m].set(blocks[k])
    return out


def _dft_tables(seq):
    n1 = seq // DFT_N2
    k1 = np.arange(n1, dtype=np.int64)[:, None]
    nn1 = np.arange(n1, dtype=np.int64)[None, :]
    tabs = []
    for n2 in range(DFT_N2):
        th = 2.0 * np.pi * ((k1 * (DFT_N2 * nn1 + n2)) % seq) / seq
        tabs.append(np.concatenate([np.cos(th), -np.sin(th)], axis=0))
    mtab = np.stack(tabs)
    k2 = np.arange(DFT_N2, dtype=np.int64)[:, None]
    nn2 = np.arange(DFT_N2, dtype=np.int64)[None, :]
    th2 = 2.0 * np.pi * ((k2 * nn2) % DFT_N2) / DFT_N2
    scale = 1.0 / math.sqrt(seq * FNET_GDIM)
    dmat = scale * np.concatenate([np.cos(th2), np.sin(th2)], axis=1)
    return jnp.asarray(mtab, F32), jnp.asarray(dmat, F32)


def _dense_dft_table(seq):
    k = np.arange(seq, dtype=np.int64)
    th = 2.0 * np.pi * ((k[:, None] * k[None, :]) % seq) / seq
    scale = 1.0 / math.sqrt(seq * FNET_GDIM)
    return jnp.asarray(scale * np.concatenate([np.cos(th), -np.sin(th)], axis=1), F32)


def _channel_dft_tables():
    k = np.arange(FNET_GDIM, dtype=np.int64)
    th = 2.0 * np.pi * ((k[:, None] * k[None, :]) % FNET_GDIM) / FNET_GDIM
    eye = np.eye(FNET_GROUPS)
    return jnp.asarray(np.kron(eye, np.cos(th)), F32), jnp.asarray(np.kron(eye, np.sin(th)), F32)


def _pool_inverse_counts(seq):
    pos = np.arange(seq, dtype=np.int64)
    cols = []
    for win in POOL_WINDOWS:
        left = win // 2
        cnt = np.minimum(pos + (win - left), seq) - np.maximum(pos - left, 0)
        cols.append(np.repeat((1.0 / cnt)[:, None], POOL_GDIM, axis=1))
    return jnp.asarray(np.concatenate(cols, axis=1), F32)


def _scan_matrices():
    r = np.arange(SSD_CHUNK)
    lower = (r[:, None] >= r[None, :]).astype(np.float32)
    return jnp.asarray(np.stack([lower, lower.T]), F32)


def _head_expansion():
    ex = np.zeros((LANE, SSD_WIDTH), np.float32)
    for h in range(SSD_HEADS):
        ex[h, h * SSD_HEADDIM:(h + 1) * SSD_HEADDIM] = 1.0
    return jnp.asarray(ex, BF16)


def _token_mixer(x, sh1, sc1, lw, init, tri, ex, *, tm):
    bsz, seq, _ = x.shape
    z, xs, bc, dt, da, pq, ypl = _inproj_call(
        x, sh1, sc1, lw["w_cat"], lw["ssd_conv_w"], lw["ssd_conv_b"], lw["dt_bias"], lw["a_log"], lw["fmix"],
        lw["pool_bd"], lw["pool_scale"], _pool_inverse_counts(seq), tm=tm, pq_blocked=_two_stage_dft(seq))
    yf, yb, fin = _ssd_call(xs, bc, dt, da, tri, ex, init)
    return z, xs, yf, yb, pq, ypl, fin


def _two_stage_dft(seq):
    return seq % (DFT_N2 * 8) == 0


def _fourier_positions(pq, seq):
    bsz, w = pq.shape[0], FNET_WIDTH
    if _two_stage_dft(seq):
        n1 = seq // DFT_N2
        mtab, dmat = _dft_tables(seq)
        t_re, t_im = _dft1_call(pq, mtab)
        out = _dft2_call(dmat, [t_re, t_im], tn=min(DFT2_TN, n1 * w), split=w)
        return out.reshape(bsz, seq, w)
    return _dft2_call(_dense_dft_table(seq), [pq.reshape(bsz, 2 * seq, w)], tn=w)


def kernel(x, c, ctx, c_ctx, w_ada, b_ada, w_in, ssd_conv_w, ssd_conv_b, ssd_dt_bias, ssd_a_log, ssd_d,
           ssd_norm_w, fnet_w, pool_w, pool_scale, w_out, ln1_g, ln1_b, ffn_w_up, ffn_conv_w, ffn_conv_b,
           ffn_w_down, ln2_g, ln2_b):
    bsz, seq, d = x.shape
    n_ctx = ctx.shape[1]
    assert d == D_MODEL and bsz <= 7 and seq % (GRID_W * 8) == 0 and n_ctx % SSD_CHUNK == 0
    tm_lat = min(1024, seq)
    tm_ffn = min(1024, seq)

    cv = jnp.zeros((8, d), F32).at[0:bsz].set(c).at[bsz].set(c_ctx)
    mods = _ada_call(cv, w_ada, b_ada)
    tri = _scan_matrices()
    ex = _head_expansion()
    fmix = _fmix_call(*_channel_dft_tables(), jnp.stack([_block_diag(fnet_w[l]) for l in range(DEPTH)]))
    zero_state = jnp.zeros((bsz,) + SSD_STATE_SHAPE, F32)
    lane_pad = jnp.zeros((d, LANE - SSD_HEADS), F32)

    for l in range(DEPTH):
        last = l == DEPTH - 1
        m = mods[l].reshape(8, 6, d)
        lat = [m[0:bsz, k][:, None, :] for k in range(6)]
        cx = [m[bsz:bsz + 1, k][:, None, :] for k in range(6)]
        w = w_in[l]
        w_cat = jnp.concatenate(
            [w[:, 0:OFF_DT], w[:, OFF_FNET:N_IN], w[:, OFF_DT:OFF_DT + SSD_HEADS], lane_pad,
             w[:, OFF_DT + SSD_HEADS:OFF_FNET], lane_pad], axis=1).astype(BF16)
        pad16 = ((0, 0), (0, LANE - SSD_HEADS))
        lw = {
            "w_cat": w_cat,
            "ssd_conv_w": ssd_conv_w[l],
            "ssd_conv_b": ssd_conv_b[l][None, :],
            "dt_bias": jnp.pad(ssd_dt_bias[l], pad16),
            "a_log": jnp.pad(ssd_a_log[l], pad16),
            "fmix": fmix[l],
            "pool_bd": _block_diag(pool_w[l]).astype(BF16),
            "pool_scale": pool_scale[l][None, :],
        }
        dsk = jnp.repeat(ssd_d[l], SSD_HEADDIM)[None, :]
        nw = ssd_norm_w[l][None, :]
        wo = w_out[l].astype(BF16)
        lg1, lb1 = ln1_g[l][None, :], ln1_b[l][None, :]
        lg2, lb2 = ln2_g[l][None, :], ln2_b[l][None, :]
        w_up = ffn_w_up[l].astype(BF16)
        w_down = ffn_w_down[l].astype(BF16)
        conv9 = ffn_conv_w[l].reshape(9, 2 * D_FF)
        conv_b = ffn_conv_b[l][None, :]

        cz, cxs, cyf, cyb, cpq, cypl, c_fin = _token_mixer(ctx, cx[0], cx[1], lw, zero_state, tri, ex, tm=n_ctx)
        z, xs, yf, yb, pq, ypl, _ = _token_mixer(x, lat[0], lat[1], lw, c_fin, tri, ex, tm=tm_lat)

        yfn = _fourier_positions(pq, seq)
        x1, h2 = _mix_call(yf, yb, xs, z, yfn, ypl, x, dsk, nw, wo, lat[2], lg1, lb1, lat[3], lat[4], tm=tm_lat)
        x = _ffn_call(h2, w_up, conv9, conv_b, w_down, x1, lat[5], lg2, lb2, tm=tm_ffn, on_grid=True, period=GRID_W)

        if not last:
            cyfn = _fourier_positions(cpq, n_ctx)
            c1, ch2 = _mix_call(cyf, cyb, cxs, cz, cyfn, cypl, ctx, dsk, nw, wo, cx[2], lg1, lb1, cx[3], cx[4],
                                tm=n_ctx)
            ctx = _ffn_call(ch2.reshape(1, bsz * n_ctx, d), w_up, conv9, conv_b, w_down, c1.reshape(1, bsz * n_ctx, d),
                            cx[5], lg2, lb2, tm=bsz * n_ctx, on_grid=False, period=n_ctx).reshape(bsz, n_ctx, d)
    return x
```
